```python
import math
import jax, jax.numpy as jnp
from jax import lax
import numpy as np

D_MODEL = 1024
BATCH = 4
SEQ = 4096
DEPTH = 4

CHUNK = 64
EPS = 1e-6
ROPE_THETA = 10000.0

A_WIDTH = 512
A_GROUPS = 4
A_GROUP_DIM = A_WIDTH // A_GROUPS
A_BLOCK = 128

B_HEADS = 4
B_HEAD_DIM = 128
B_WIDTH = B_HEADS * B_HEAD_DIM
IDX_HEADS = 8
IDX_DIM = 64
TOPK_MAX = 256
Q_BLOCK = 128

SPLIT_SIZES = (A_WIDTH, A_WIDTH, A_WIDTH,
               B_WIDTH, B_WIDTH, B_WIDTH, B_WIDTH,
               IDX_HEADS * IDX_DIM, IDX_DIM, IDX_HEADS,
               D_MODEL, D_MODEL)
IN_WIDTH = sum(SPLIT_SIZES)

kernel_name = "hybrid_gmlp_dsa_gated_trunk"


def rms_norm(x, g):
    x32 = x.astype(jnp.float32)
    y = x32 * lax.rsqrt(jnp.mean(x32 * x32, axis=-1, keepdims=True) + EPS)
    return (y * g.astype(jnp.float32)).astype(x.dtype)


def layer_norm(x, g, b):
    x32 = x.astype(jnp.float32)
    mu = jnp.mean(x32, axis=-1, keepdims=True)
    xc = x32 - mu
    y = xc * lax.rsqrt(jnp.mean(xc * xc, axis=-1, keepdims=True) + EPS)
    return (y * g.astype(jnp.float32) + b.astype(jnp.float32)).astype(x.dtype)


def rope_tables(s, dim, dtype):
    pos = jnp.arange(s, dtype=jnp.float32)
    inv = ROPE_THETA ** (-jnp.arange(0, dim, 2, dtype=jnp.float32) / dim)
    ang = pos[:, None] * inv[None, :]
    return jnp.cos(ang).astype(dtype), jnp.sin(ang).astype(dtype)


def apply_rope(x, cos, sin):
    x1, x2 = jnp.split(x, 2, axis=-1)
    c = cos[:, None, :]
    s = sin[:, None, :]
    return jnp.concatenate([x1 * c - x2 * s, x1 * s + x2 * c], axis=-1)


def split_columns(proj):
    points = []
    acc = 0
    for size in SPLIT_SIZES[:-1]:
        acc += size
        points.append(acc)
    return jnp.split(proj, points, axis=-1)


def spatial_gating_branch(u, v, z, ws, bs, ln_g, ln_b):
    b, s, _ = u.shape
    v = layer_norm(v, ln_g, ln_b)
    chunk_id = jnp.arange(A_BLOCK) // CHUNK
    mask = chunk_id[None, :] <= chunk_id[:, None]
    w = jnp.where(mask[None], ws, jnp.zeros_like(ws))
    vb = v.reshape(b, s // A_BLOCK, A_BLOCK, A_GROUPS, A_GROUP_DIM)
    mixed = jnp.einsum('gij,bnjgc->bnigc', w, vb) + bs.T[:, :, None]
    return u * mixed.reshape(b, s, A_WIDTH) * jax.nn.silu(z)


def sparse_attention_branch(q, k, v, qi, ki, wi, topk):
    b, s, h, dh = q.shape
    nb = s // Q_BLOCK
    key_chunk = jnp.arange(s) // CHUNK
    ki32 = ki.astype(jnp.float32)
    gather = jax.vmap(lambda arr, ix: arr[ix])
    att_scale = B_HEAD_DIM ** -0.5
    idx_scale = (IDX_DIM ** -0.5) * (IDX_HEADS ** -0.5)

    def block(args):
        n, qb, qib, wib = args
        q_chunk = (n * Q_BLOCK + jnp.arange(Q_BLOCK)) // CHUNK
        admissible = key_chunk[None, :] <= q_chunk[:, None]
        logits = jnp.einsum('bqhd,bsd->bqhs', qib.astype(jnp.float32), ki32)
        score = jnp.einsum('bqh,bqhs->bqs', wib.astype(jnp.float32), jax.nn.relu(logits)) * idx_scale
        score = jnp.where(admissible[None], score, -jnp.inf)
        _, idx = lax.top_k(score, topk)
        valid = key_chunk[idx] <= q_chunk[None, :, None]
        kg = gather(k, idx)
        vg = gather(v, idx)
        att = jnp.einsum('bqhd,bqkhd->bqhk', qb, kg).astype(jnp.float32) * att_scale
        att = jnp.where(valid[:, :, None, :], att, -jnp.inf)
        p = jax.nn.softmax(att, axis=-1).astype(v.dtype)
        return jnp.einsum('bqhk,bqkhd->bqhd', p, vg)

    to_blocks = lambda t: jnp.moveaxis(t.reshape((b, nb, Q_BLOCK) + t.shape[2:]), 1, 0)
    out = lax.map(block, (jnp.arange(nb), to_blocks(q), to_blocks(qi), to_blocks(wi)))
    return jnp.moveaxis(out, 0, 1).reshape(b, s, h * dh)


def setup_inputs(seed: int = 0) -> dict:
    key = jax.random.key(seed)
    ks = jax.random.split(key, 16)
    f32 = jnp.float32
    nrm = lambda k, shape, scale: jax.random.normal(k, shape, f32) * scale
    return {
        "x": jax.random.normal(ks[0], (BATCH, SEQ, D_MODEL), f32),
        "norm_g": 1.0 + nrm(ks[1], (DEPTH, D_MODEL), 0.02),
        "w_in": nrm(ks[2], (DEPTH, D_MODEL, IN_WIDTH), D_MODEL ** -0.5),
        "gate_b": nrm(ks[3], (DEPTH, 2, D_MODEL), 0.02),
        "a_ln_g": 1.0 + nrm(ks[4], (DEPTH, A_WIDTH), 0.02),
        "a_ln_b": nrm(ks[5], (DEPTH, A_WIDTH), 0.02),
        "a_ws": nrm(ks[6], (DEPTH, A_GROUPS, A_BLOCK, A_BLOCK), A_BLOCK ** -0.5),
        "a_bs": 1.0 + nrm(ks[7], (DEPTH, A_GROUPS, A_BLOCK), 0.02),
        "q_norm_g": 1.0 + nrm(ks[8], (DEPTH, B_HEAD_DIM), 0.02),
        "k_norm_g": 1.0 + nrm(ks[9], (DEPTH, B_HEAD_DIM), 0.02),
        "w_oa": nrm(ks[10], (DEPTH, A_WIDTH, D_MODEL), A_WIDTH ** -0.5),
        "w_ob": nrm(ks[11], (DEPTH, B_WIDTH, D_MODEL), B_WIDTH ** -0.5),
        "w_out": nrm(ks[12], (DEPTH, D_MODEL, D_MODEL), D_MODEL ** -0.5),
    }


def reference(x, norm_g, w_in, gate_b, a_ln_g, a_ln_b, a_ws, a_bs, q_norm_g, k_norm_g,
              w_oa, w_ob, w_out):
    b, s, _ = x.shape
    topk = min(TOPK_MAX, s // 4)
    cos_q, sin_q = rope_tables(s, B_HEAD_DIM, x.dtype)
    cos_i, sin_i = rope_tables(s, IDX_DIM, x.dtype)
    for l in range(DEPTH):
        h = rms_norm(x, norm_g[l])
        proj = jnp.einsum('bsd,de->bse', h, w_in[l])
        (a_u, a_v, a_z, b_q, b_k, b_v, b_z, i_q, i_k, i_w, g_a, g_b) = split_columns(proj)

        y_a = spatial_gating_branch(jax.nn.gelu(a_u), jax.nn.gelu(a_v), a_z,
                                    a_ws[l], a_bs[l], a_ln_g[l], a_ln_b[l])

        q = apply_rope(rms_norm(b_q.reshape(b, s, B_HEADS, B_HEAD_DIM), q_norm_g[l]), cos_q, sin_q)
        k = apply_rope(rms_norm(b_k.reshape(b, s, B_HEADS, B_HEAD_DIM), k_norm_g[l]), cos_q, sin_q)
        v = b_v.reshape(b, s, B_HEADS, B_HEAD_DIM)
        qi = apply_rope(i_q.reshape(b, s, IDX_HEADS, IDX_DIM), cos_i, sin_i)
        ki = apply_rope(i_k[:, :, None, :], cos_i, sin_i)[:, :, 0, :]
        att = sparse_attention_branch(q, k, v, qi, ki, i_w, topk)
        y_b = att * jax.nn.silu(b_z)

        o_a = jnp.einsum('bse,ed->bsd', y_a, w_oa[l])
        o_b = jnp.einsum('bse,ed->bsd', y_b, w_ob[l])
        merged = jax.nn.sigmoid(g_a + gate_b[l, 0]) * o_a + jax.nn.sigmoid(g_b + gate_b[l, 1]) * o_b
        x = x + jnp.einsum('bsd,de->bse', merged, w_out[l])
    return x
```

```python
import functools

import jax
import jax.numpy as jnp
from jax import lax
from jax.experimental import pallas as pl
from jax.experimental.pallas import tpu as pltpu

F32 = jnp.float32
BF16 = jnp.bfloat16

CHUNK = 64
EPS = 1e-6
ROPE_THETA = 10000.0

A_WIDTH = 512
A_GROUPS = 4
A_BLOCK = 128
B_HEADS = 4
B_HEAD_DIM = 128
B_WIDTH = B_HEADS * B_HEAD_DIM
IDX_HEADS = 8
IDX_DIM = 64
TOPK_MAX = 256

_O_AU, _O_AV, _O_AZ = 0, 512, 1024
_O_BQ, _O_BK, _O_BV, _O_BZ = 1536, 2048, 2560, 3072
_O_IQ, _O_IK, _O_IW = 3584, 4096, 4160
_O_GA, _O_GB, _O_END = 4168, 5192, 6216

LANES = 128
SUBLANES = 8
VMEM_LIMIT_BYTES = 52 * 1024 * 1024

QB = 256
TK = 256
TM_STD = 512
TM_OUT = 512
NEG_BIG = -1e30
F32_LOWEST = -3.0e38


def _rms_rows(x, g):
    ms = jnp.mean(x * x, axis=-1, keepdims=True)
    return x * lax.rsqrt(ms + EPS) * g


def _proj_std_body(x_ref, ng_ref, w_ref, gb_ref, lng_ref, lnb_ref, ws_ref, bias_ref, kng_ref,
                   woa_ref, cos_ref, sin_ref, cosi_ref, sini_ref,
                   ma_ref, sgb_ref, k_ref, ki_ref):
    tm = x_ref.shape[0]
    h = _rms_rows(x_ref[...], ng_ref[...]).astype(BF16)

    def proj(lo, hi):
        return jnp.dot(h, w_ref[:, lo:hi], preferred_element_type=F32)

    gv = jax.nn.gelu(proj(512, 1024))
    mu = jnp.mean(gv, axis=-1, keepdims=True)
    xc = gv - mu
    var = jnp.mean(xc * xc, axis=-1, keepdims=True)
    vn = (xc * lax.rsqrt(var + EPS) * lng_ref[...] + lnb_ref[...]).astype(BF16)

    ci = lax.broadcasted_iota(jnp.int32, (A_BLOCK, A_BLOCK), 0) // CHUNK
    cj = lax.broadcasted_iota(jnp.int32, (A_BLOCK, A_BLOCK), 1) // CHUNK
    causal = cj <= ci
    wm = [jnp.where(causal, ws_ref[g], 0.0).astype(BF16) for g in range(A_GROUPS)]
    row_blocks = []
    for r in range(tm // A_BLOCK):
        cols = []
        for g in range(A_GROUPS):
            vb = vn[r * A_BLOCK:(r + 1) * A_BLOCK, g * LANES:(g + 1) * LANES]
            cols.append(jnp.dot(wm[g], vb, preferred_element_type=F32))
        row_blocks.append(jnp.concatenate(cols, axis=1) + bias_ref[...])
    mixed = jnp.concatenate(row_blocks, axis=0)

    y_a = jax.nn.gelu(proj(0, 512)) * mixed * jax.nn.silu(proj(1024, 1536))
    o_a = jnp.dot(y_a.astype(BF16), woa_ref[...], preferred_element_type=F32)
    ma_ref[...] = (jax.nn.sigmoid(proj(2176, 3200) + gb_ref[0:1, :]) * o_a).astype(BF16)
    sgb_ref[...] = jax.nn.sigmoid(proj(3200, 4224) + gb_ref[1:2, :]).astype(BF16)

    b_k = proj(1536, 2048)
    for hh in range(B_HEADS):
        kh = _rms_rows(b_k[:, hh * LANES:(hh + 1) * LANES], kng_ref[...])
        kh = kh * cos_ref[...] + pltpu.roll(kh, B_HEAD_DIM // 2, 1) * sin_ref[...]
        k_ref[:, hh * LANES:(hh + 1) * LANES] = kh.astype(BF16)

    ik = proj(2048, 2176)
    ik = ik * cosi_ref[...] + pltpu.roll(ik, IDX_DIM // 2, 1) * sini_ref[...]
    ki_ref[...] = ik[:, :IDX_DIM].astype(BF16)


def _proj_std(x2d, ng, w_std, gb, lng, lnb, ws, bias_full, kng, woa, cos2, sin2, cosi2, sini2, seq):
    m, d = x2d.shape
    tm = TM_STD
    nt = seq // tm
    full = lambda shape: pl.BlockSpec(shape, lambda i: (0,) * len(shape))
    tab = lambda: pl.BlockSpec((tm, LANES), lambda i: (i % nt, 0))
    row = lambda width: pl.BlockSpec((tm, width), lambda i: (i, 0))
    return pl.pallas_call(
        _proj_std_body,
        grid=(m // tm,),
        in_specs=[row(d), full(ng.shape), full(w_std.shape), full(gb.shape), full(lng.shape),
                  full(lnb.shape), full(ws.shape), full(bias_full.shape), full(kng.shape),
                  full(woa.shape), tab(), tab(), tab(), tab()],
        out_specs=[row(d), row(d), row(B_WIDTH), row(IDX_DIM)],
        out_shape=[jax.ShapeDtypeStruct((m, d), BF16), jax.ShapeDtypeStruct((m, d), BF16),
                   jax.ShapeDtypeStruct((m, B_WIDTH), BF16),
                   jax.ShapeDtypeStruct((m, IDX_DIM), BF16)],
        compiler_params=pltpu.CompilerParams(dimension_semantics=("arbitrary",),
                                             vmem_limit_bytes=VMEM_LIMIT_BYTES),
        name="proj_std",
    )(x2d, ng, w_std, gb, lng, lnb, ws, bias_full, kng, woa, cos2, sin2, cosi2, sini2)


def _proj_tr_body(x_ref, ng_ref, w_ref, ww_ref, qng_ref, cos_ref, sin_ref, cosi_ref, sini_ref,
                  q_ref, v_ref, zb_ref, qi_ref, wi_ref):
    h = _rms_rows(x_ref[0], ng_ref[...]).astype(BF16)

    def proj_t(w):
        return lax.dot_general(w, h, (((1,), (1,)), ((), ())), preferred_element_type=F32)

    att_scale = B_HEAD_DIM ** -0.5
    half = B_HEAD_DIM // 2
    qt = proj_t(w_ref[0:512, :])
    for hh in range(B_HEADS):
        qh = qt[hh * B_HEAD_DIM:(hh + 1) * B_HEAD_DIM, :]
        ms = jnp.mean(qh * qh, axis=0, keepdims=True)
        qh = qh * lax.rsqrt(ms + EPS) * qng_ref[...]
        x1, x2 = qh[:half, :], qh[half:, :]
        c, s = cos_ref[...], sin_ref[...]
        base = hh * B_HEAD_DIM
        q_ref[0, 0, base:base + half, :] = ((x1 * c - x2 * s) * att_scale).astype(BF16)
        q_ref[0, 0, base + half:base + B_HEAD_DIM, :] = ((x1 * s + x2 * c) * att_scale).astype(BF16)

    v_ref[0, 0] = proj_t(w_ref[512:1024, :]).astype(BF16)
    zb_ref[0, 0] = jax.nn.silu(proj_t(w_ref[1024:1536, :])).astype(BF16)

    qit = proj_t(w_ref[1536:2048, :])
    ih = IDX_DIM // 2
    for hh in range(IDX_HEADS):
        xh = qit[hh * IDX_DIM:(hh + 1) * IDX_DIM, :]
        x1, x2 = xh[:ih, :], xh[ih:, :]
        c, s = cosi_ref[...], sini_ref[...]
        base = hh * IDX_DIM
        qi_ref[0, 0, base:base + ih, :] = (x1 * c - x2 * s).astype(BF16)
        qi_ref[0, 0, base + ih:base + IDX_DIM, :] = (x1 * s + x2 * c).astype(BF16)

    wi_ref[0, 0] = proj_t(ww_ref[...])


def _proj_tr(x, ng, w_tr, w_trw, qng, cos_t, sin_t, cosi_t, sini_t):
    b, s, d = x.shape
    nq = s // QB
    full = lambda shape: pl.BlockSpec(shape, lambda bi, i: (0,) * len(shape))
    tabt = lambda rows: pl.BlockSpec((rows, QB), lambda bi, i: (0, i))
    outt = lambda rows: pl.BlockSpec((1, 1, rows, QB), lambda bi, i: (bi, i, 0, 0))
    wrows = w_trw.shape[0]
    return pl.pallas_call(
        _proj_tr_body,
        grid=(b, nq),
        in_specs=[pl.BlockSpec((1, QB, d), lambda bi, i: (bi, i, 0)), full(ng.shape),
                  full(w_tr.shape), full(w_trw.shape), full(qng.shape),
                  tabt(B_HEAD_DIM // 2), tabt(B_HEAD_DIM // 2), tabt(IDX_DIM // 2), tabt(IDX_DIM // 2)],
        out_specs=[outt(B_WIDTH), outt(B_WIDTH), outt(B_WIDTH), outt(IDX_HEADS * IDX_DIM), outt(wrows)],
        out_shape=[jax.ShapeDtypeStruct((b, nq, B_WIDTH, QB), BF16),
                   jax.ShapeDtypeStruct((b, nq, B_WIDTH, QB), BF16),
                   jax.ShapeDtypeStruct((b, nq, B_WIDTH, QB), BF16),
                   jax.ShapeDtypeStruct((b, nq, IDX_HEADS * IDX_DIM, QB), BF16),
                   jax.ShapeDtypeStruct((b, nq, wrows, QB), F32)],
        compiler_params=pltpu.CompilerParams(dimension_semantics=("arbitrary", "arbitrary"),
                                             vmem_limit_bytes=VMEM_LIMIT_BYTES),
        name="proj_tr",
    )(x, ng, w_tr, w_trw, qng, cos_t, sin_t, cosi_t, sini_t)


def _key_to_float(u):
    bits = jnp.where(u < 0, u & jnp.int32(0x7FFFFFFF), ~u)
    return lax.bitcast_convert_type(bits, F32)


def _attn_body(q_ref, qi_ref, wi_ref, zb_ref, k_ref, ki_ref, v_ref, y_ref, s_scr, acc_scr, *, topk):
    n = pl.program_id(1)
    ntiles = n + 1
    idx_scale = (IDX_DIM ** -0.5) * (IDX_HEADS ** -0.5)
    rows_per_tile = TK // SUBLANES

    def score_tile(t, diagonal):
        ki_t = ki_ref[0, pl.ds(pl.multiple_of(t * TK, TK), TK), :]
        acc = jnp.zeros((TK, QB), F32)
        for hh in range(IDX_HEADS):
            logit = jnp.dot(ki_t, qi_ref[0, 0, hh * IDX_DIM:(hh + 1) * IDX_DIM, :],
                            preferred_element_type=F32)
            acc = acc + wi_ref[0, 0, hh:hh + 1, :] * jnp.maximum(logit, 0.0)
        sc = acc * idx_scale
        if diagonal:
            kc = lax.broadcasted_iota(jnp.int32, (TK, QB), 0) // CHUNK
            qc = lax.broadcasted_iota(jnp.int32, (TK, QB), 1) // CHUNK
            sc = jnp.where(kc <= qc, sc, -jnp.inf)
        s_scr[t] = sc

    def score_loop(t, carry):
        score_tile(t, False)
        return carry

    lax.fori_loop(0, n, score_loop, 0)
    score_tile(n, True)

    def count_ge(cand):
        cand_b = jnp.broadcast_to(cand, (SUBLANES, QB))

        def tile_body(t, acc):
            for r in range(rows_per_tile):
                sl = s_scr[t, r * SUBLANES:(r + 1) * SUBLANES, :]
                acc = acc + jnp.where(sl >= cand_b, 1.0, 0.0)
            return acc

        acc = lax.fori_loop(0, ntiles, tile_body, jnp.zeros((SUBLANES, QB), F32))
        return jnp.sum(acc, axis=0, keepdims=True)

    def bit_body(i, carry):
        prefix, cnt_at = carry
        trial = prefix | jnp.left_shift(jnp.int32(1), 31 - i)
        cnt = count_ge(_key_to_float(trial))
        ok = cnt >= float(topk)
        return jnp.where(ok, trial, prefix), jnp.where(ok, cnt, cnt_at)

    prefix, cnt_at = lax.fori_loop(
        0, 32, bit_body, (jnp.zeros((1, QB), jnp.int32), jnp.zeros((1, QB), F32)))
    thr = _key_to_float(prefix)

    excess0 = jnp.where(thr > -jnp.inf, cnt_at - float(topk), 0.0)

    @pl.when(jnp.max(excess0) > 0.0)
    def _():
        thr_b = jnp.broadcast_to(thr, (SUBLANES, QB))
        sub_iota = lax.broadcasted_iota(jnp.int32, (SUBLANES, QB), 0).astype(F32)

        def rows(r):
            return pl.ds(pl.multiple_of(r * SUBLANES, SUBLANES), SUBLANES)

        def drop_one(excess):
            active_b = jnp.broadcast_to(excess > 0.0, (SUBLANES, QB))

            def min_tile(t, m):
                def row_body(r, m):
                    sl = s_scr[t, rows(r), :]
                    return jnp.minimum(m, jnp.where(sl >= thr_b, sl, jnp.inf))
                return lax.fori_loop(0, rows_per_tile, row_body, m)

            m8 = lax.fori_loop(0, ntiles, min_tile, jnp.full((SUBLANES, QB), jnp.inf, F32))
            m_b = jnp.broadcast_to(jnp.min(m8, axis=0, keepdims=True), (SUBLANES, QB))

            def key_index(t, r):
                return sub_iota + (t * TK + r * SUBLANES).astype(F32)

            def idx_tile(t, j):
                def row_body(r, j):
                    sl = s_scr[t, rows(r), :]
                    return jnp.maximum(j, jnp.where(sl == m_b, key_index(t, r), -1.0))
                return lax.fori_loop(0, rows_per_tile, row_body, j)

            j8 = lax.fori_loop(0, ntiles, idx_tile, jnp.full((SUBLANES, QB), -1.0, F32))
            j_b = jnp.broadcast_to(jnp.max(j8, axis=0, keepdims=True), (SUBLANES, QB))

            def drop_tile(t, carry):
                def row_body(r, carry):
                    sl = s_scr[t, rows(r), :]
                    hit = (key_index(t, r) == j_b) & active_b
                    s_scr[t, rows(r), :] = jnp.where(hit, -jnp.inf, sl)
                    return carry
                return lax.fori_loop(0, rows_per_tile, row_body, carry)

            lax.fori_loop(0, ntiles, drop_tile, 0)
            return excess - jnp.where(excess > 0.0, 1.0, 0.0)

        lax.while_loop(lambda e: jnp.max(e) > 0.0, drop_one, excess0)

    thr_fin = jnp.maximum(thr, F32_LOWEST)
    acc_scr[...] = jnp.zeros_like(acc_scr)

    def att_tile(t, carry):
        ms, ls = carry
        sel = s_scr[t] >= thr_fin
        krows = pl.ds(pl.multiple_of(t * TK, TK), TK)
        new_ms, new_ls = [], []
        for hh in range(B_HEADS):
            hs = slice(hh * B_HEAD_DIM, (hh + 1) * B_HEAD_DIM)
            att = jnp.dot(k_ref[0, krows, hs], q_ref[0, 0, hs, :], preferred_element_type=F32)
            att = jnp.where(sel, att, NEG_BIG)
            m_new = jnp.maximum(ms[hh], jnp.max(att, axis=0, keepdims=True))
            alpha = jnp.exp(ms[hh] - m_new)
            p = jnp.exp(att - m_new)
            new_ls.append(alpha * ls[hh] + jnp.sum(p, axis=0, keepdims=True))
            new_ms.append(m_new)
            pv = jnp.dot(v_ref[0, t, hs, :], p.astype(BF16), preferred_element_type=F32)
            acc_scr[hs, :] = alpha * acc_scr[hs, :] + pv
        return tuple(new_ms), tuple(new_ls)

    init = (tuple(jnp.full((1, QB), NEG_BIG, F32) for _ in range(B_HEADS)),
            tuple(jnp.zeros((1, QB), F32) for _ in range(B_HEADS)))
    _, ls = lax.fori_loop(0, ntiles, att_tile, init)

    parts = []
    for hh in range(B_HEADS):
        hs = slice(hh * B_HEAD_DIM, (hh + 1) * B_HEAD_DIM)
        parts.append(acc_scr[hs, :] / ls[hh] * zb_ref[0, 0, hs, :].astype(F32))
    y_ref[0] = jnp.concatenate(parts, axis=0).T.astype(BF16)


def _attn(qT, qiT, wiT, zbT, k, ki, vT, topk):
    b, nq = qT.shape[0], qT.shape[1]
    s = k.shape[1]
    blk = lambda rows: pl.BlockSpec((1, 1, rows, QB), lambda bi, i: (bi, i, 0, 0))
    return pl.pallas_call(
        functools.partial(_attn_body, topk=topk),
        grid=(b, nq),
        in_specs=[blk(B_WIDTH), blk(IDX_HEADS * IDX_DIM), blk(wiT.shape[2]), blk(B_WIDTH),
                  pl.BlockSpec((1, s, B_WIDTH), lambda bi, i: (bi, 0, 0)),
                  pl.BlockSpec((1, s, IDX_DIM), lambda bi, i: (bi, 0, 0)),
                  pl.BlockSpec((1, nq, B_WIDTH, QB), lambda bi, i: (bi, 0, 0, 0))],
        out_specs=pl.BlockSpec((1, QB, B_WIDTH), lambda bi, i: (bi, i, 0)),
        out_shape=jax.ShapeDtypeStruct((b, s, B_WIDTH), BF16),
        scratch_shapes=[pltpu.VMEM((s // TK, TK, QB), F32), pltpu.VMEM((B_WIDTH, QB), F32)],
        compiler_params=pltpu.CompilerParams(dimension_semantics=("arbitrary", "arbitrary"),
                                             vmem_limit_bytes=VMEM_LIMIT_BYTES),
        name="attn",
    )(qT, qiT, wiT, zbT, k, ki, vT)


def _out_body(x_ref, yb_ref, ma_ref, sgb_ref, wob_ref, wout_ref, o_ref):
    o_b = jnp.dot(yb_ref[...], wob_ref[...], preferred_element_type=F32)
    merged = ma_ref[...].astype(F32) + sgb_ref[...].astype(F32) * o_b
    o_ref[...] = x_ref[...] + jnp.dot(merged.astype(BF16), wout_ref[...], preferred_element_type=F32)


def _out(x2d, yb, ma, sgb, wob, wout):
    m, d = x2d.shape
    tm = TM_OUT
    row = lambda width: pl.BlockSpec((tm, width), lambda i: (i, 0))
    full = lambda shape: pl.BlockSpec(shape, lambda i: (0,) * len(shape))
    return pl.pallas_call(
        _out_body,
        grid=(m // tm,),
        in_specs=[row(d), row(B_WIDTH), row(d), row(d), full(wob.shape), full(wout.shape)],
        out_specs=row(d),
        out_shape=jax.ShapeDtypeStruct((m, d), F32),
        compiler_params=pltpu.CompilerParams(dimension_semantics=("arbitrary",),
                                             vmem_limit_bytes=VMEM_LIMIT_BYTES),
        name="out_proj",
    )(x2d, yb, ma, sgb, wob, wout)


def _rope_tables(s, dim):
    pos = jnp.arange(s, dtype=F32)
    inv = ROPE_THETA ** (-jnp.arange(0, dim, 2, dtype=F32) / dim)
    ang = pos[:, None] * inv[None, :]
    return jnp.cos(ang), jnp.sin(ang)


def kernel(x, norm_g, w_in, gate_b, a_ln_g, a_ln_b, a_ws, a_bs, q_norm_g, k_norm_g, w_oa, w_ob, w_out):
    b, s, d = x.shape
    depth = w_in.shape[0]
    topk = min(TOPK_MAX, s // 4)
    assert s % TM_STD == 0 and s % QB == 0 and (b * s) % TM_OUT == 0 and QB == TK
    assert w_in.shape[2] == _O_END and topk <= TK

    cos_q, sin_q = _rope_tables(s, B_HEAD_DIM)
    cos_i, sin_i = _rope_tables(s, IDX_DIM)
    cos2 = jnp.concatenate([cos_q, cos_q], axis=1)
    sin2 = jnp.concatenate([-sin_q, sin_q], axis=1)
    cosi2 = jnp.concatenate([cos_i, cos_i, cos_i, cos_i], axis=1)
    sini2 = jnp.concatenate([-sin_i, sin_i, -sin_i, sin_i], axis=1)
    cos_t, sin_t, cosi_t, sini_t = cos_q.T, sin_q.T, cos_i.T, sin_i.T

    ik_cols = w_in[:, :, _O_IK:_O_IW]
    w_std = jnp.concatenate([w_in[:, :, _O_AU:_O_BQ], w_in[:, :, _O_BK:_O_BV], ik_cols, ik_cols,
                             w_in[:, :, _O_GA:_O_END]], axis=2).astype(BF16)
    w_tr = jnp.swapaxes(jnp.concatenate([w_in[:, :, _O_BQ:_O_BK], w_in[:, :, _O_BV:_O_IK]], axis=2),
                        1, 2).astype(BF16)
    iw_t = jnp.swapaxes(w_in[:, :, _O_IW:_O_GA], 1, 2)
    w_trw = jnp.concatenate([iw_t, jnp.zeros_like(iw_t)], axis=1).astype(BF16)
    woa, wob, wout = w_oa.astype(BF16), w_ob.astype(BF16), w_out.astype(BF16)
    bias_full = jnp.repeat(jnp.swapaxes(a_bs, 1, 2), A_WIDTH // A_GROUPS, axis=2)
    kng = jnp.reshape(k_norm_g, (depth, 1, B_HEAD_DIM))
    qng = jnp.reshape(q_norm_g, (depth, B_HEAD_DIM, 1))

    x2d = jnp.reshape(x, (b * s, d))
    for l in range(depth):
        ng = norm_g[l][None, :]
        ma, sgb, k, ki = _proj_std(x2d, ng, w_std[l], gate_b[l], a_ln_g[l][None, :], a_ln_b[l][None, :],
                                   a_ws[l], bias_full[l], kng[l], woa[l], cos2, sin2, cosi2, sini2, s)
        qT, vT, zbT, qiT, wiT = _proj_tr(jnp.reshape(x2d, (b, s, d)), ng, w_tr[l], w_trw[l], qng[l],
                                         cos_t, sin_t, cosi_t, sini_t)
        yb = _attn(qT, qiT, wiT, zbT, jnp.reshape(k, (b, s, B_WIDTH)), jnp.reshape(ki, (b, s, IDX_DIM)),
                   vT, topk)
        x2d = _out(x2d, jnp.reshape(yb, (b * s, B_WIDTH)), ma, sgb, wob[l], wout[l])
    return jnp.reshape(x2d, (b, s, d))
```

```python
import functools

import jax
import jax.numpy as jnp
from jax import lax
from jax.experimental import pallas as pl
from jax.experimental.pallas import tpu as pltpu

F32 = jnp.float32
BF16 = jnp.bfloat16

CHUNK = 64
EPS = 1e-6
ROPE_THETA = 10000.0

A_WIDTH = 512
A_GROUPS = 4
A_BLOCK = 128
B_HEADS = 4
B_HEAD_DIM = 128
B_WIDTH = B_HEADS * B_HEAD_DIM
IDX_HEADS = 8
IDX_DIM = 64
TOPK_MAX = 256

_O_AU, _O_AV, _O_AZ = 0, 512, 1024
_O_BQ, _O_BK, _O_BV, _O_BZ = 1536, 2048, 2560, 3072
_O_IQ, _O_IK, _O_IW = 3584, 4096, 4160
_O_GA, _O_GB, _O_END = 4168, 5192, 6216

LANES = 128
SUBLANES = 8
VMEM_LIMIT_BYTES = 52 * 1024 * 1024

QB = 256
TK = 256
TM_STD = 512
TM_OUT = 512
NEG_BIG = -1e30
F32_LOWEST = -3.0e38


def _rms_rows(x, g):
    ms = jnp.mean(x * x, axis=-1, keepdims=True)
    return x * lax.rsqrt(ms + EPS) * g


def _proj_std_body(x_ref, ng_ref, w_ref, gb_ref, lng_ref, lnb_ref, ws_ref, bias_ref, kng_ref,
                   woa_ref, cos_ref, sin_ref, cosi_ref, sini_ref,
                   ma_ref, sgb_ref, k_ref, ki_ref):
    tm = x_ref.shape[0]
    h = _rms_rows(x_ref[...], ng_ref[...]).astype(BF16)

    def proj(lo, hi):
        return jnp.dot(h, w_ref[:, lo:hi], preferred_element_type=F32)

    gv = jax.nn.gelu(proj(512, 1024))
    mu = jnp.mean(gv, axis=-1, keepdims=True)
    xc = gv - mu
    var = jnp.mean(xc * xc, axis=-1, keepdims=True)
    vn = (xc * lax.rsqrt(var + EPS) * lng_ref[...] + lnb_ref[...]).astype(BF16)

    ci = lax.broadcasted_iota(jnp.int32, (A_BLOCK, A_BLOCK), 0) // CHUNK
    cj = lax.broadcasted_iota(jnp.int32, (A_BLOCK, A_BLOCK), 1) // CHUNK
    causal = cj <= ci
    wm = [jnp.where(causal, ws_ref[g], 0.0).astype(BF16) for g in range(A_GROUPS)]
    row_blocks = []
    for r in range(tm // A_BLOCK):
        cols = []
        for g in range(A_GROUPS):
            vb = vn[r * A_BLOCK:(r + 1) * A_BLOCK, g * LANES:(g + 1) * LANES]
            cols.append(jnp.dot(wm[g], vb, preferred_element_type=F32))
        row_blocks.append(jnp.concatenate(cols, axis=1) + bias_ref[...])
    mixed = jnp.concatenate(row_blocks, axis=0)

    y_a = jax.nn.gelu(proj(0, 512)) * mixed * jax.nn.silu(proj(1024, 1536))
    o_a = jnp.dot(y_a.astype(BF16), woa_ref[...], preferred_element_type=F32)
    ma_ref[...] = (jax.nn.sigmoid(proj(2176, 3200) + gb_ref[0:1, :]) * o_a).astype(BF16)
    sgb_ref[...] = jax.nn.sigmoid(proj(3200, 4224) + gb_ref[1:2, :]).astype(BF16)

    b_k = proj(1536, 2048)
    for hh in range(B_HEADS):
        kh = _rms_rows(b_k[:, hh * LANES:(hh + 1) * LANES], kng_ref[...])
        kh = kh * cos_ref[...] + pltpu.roll(kh, B_HEAD_DIM // 2, 1) * sin_ref[...]
        k_ref[:, hh * LANES:(hh + 1) * LANES] = kh.astype(BF16)

    ik = proj(2048, 2176)
    ik = ik * cosi_ref[...] + pltpu.roll(ik, IDX_DIM // 2, 1) * sini_ref[...]
    ki_ref[...] = ik[:, :IDX_DIM].astype(BF16)


def _proj_std(x2d, ng, w_std, gb, lng, lnb, ws, bias_full, kng, woa, cos2, sin2, cosi2, sini2, seq):
    m, d = x2d.shape
    tm = TM_STD
    nt = seq // tm
    full = lambda shape: pl.BlockSpec(shape, lambda i: (0,) * len(shape))
    tab = lambda: pl.BlockSpec((tm, LANES), lambda i: (i % nt, 0))
    row = lambda width: pl.BlockSpec((tm, width), lambda i: (i, 0))
    return pl.pallas_call(
        _proj_std_body,
        grid=(m // tm,),
        in_specs=[row(d), full(ng.shape), full(w_std.shape), full(gb.shape), full(lng.shape),
                  full(lnb.shape), full(ws.shape), full(bias_full.shape), full(kng.shape),
                  full(woa.shape), tab(), tab(), tab(), tab()],
        out_specs=[row(d), row(d), row(B_WIDTH), row(IDX_DIM)],
        out_shape=[jax.ShapeDtypeStruct((m, d), BF16), jax.ShapeDtypeStruct((m, d), BF16),
                   jax.ShapeDtypeStruct((m, B_WIDTH), BF16),
                   jax.ShapeDtypeStruct((m, IDX_DIM), BF16)],
        compiler_params=pltpu.CompilerParams(dimension_semantics=("arbitrary",),
                                             vmem_limit_bytes=VMEM_LIMIT_BYTES),
        name="proj_std",
    )(x2d, ng, w_std, gb, lng, lnb, ws, bias_full, kng, woa, cos2, sin2, cosi2, sini2)


def _proj_tr_body(x_ref, ng_ref, w_ref, ww_ref, qng_ref, cos_ref, sin_ref, cosi_ref, sini_ref,
                  q_ref, v_ref, zb_ref, qi_ref, wi_ref):
    h = _rms_rows(x_ref[0], ng_ref[...]).astype(BF16)

    def proj_t(w):
        return lax.dot_general(w, h, (((1,), (1,)), ((), ())), preferred_element_type=F32)

    att_scale = B_HEAD_DIM ** -0.5
    half = B_HEAD_DIM // 2
    qt = proj_t(w_ref[0:512, :])
    for hh in range(B_HEADS):
        qh = qt[hh * B_HEAD_DIM:(hh + 1) * B_HEAD_DIM, :]
        ms = jnp.mean(qh * qh, axis=0, keepdims=True)
        qh = qh * lax.rsqrt(ms + EPS) * qng_ref[...]
        x1, x2 = qh[:half, :], qh[half:, :]
        c, s = cos_ref[...], sin_ref[...]
        base = hh * B_HEAD_DIM
        q_ref[0, 0, base:base + half, :] = ((x1 * c - x2 * s) * att_scale).astype(BF16)
        q_ref[0, 0, base + half:base + B_HEAD_DIM, :] = ((x1 * s + x2 * c) * att_scale).astype(BF16)

    v_ref[0, 0] = proj_t(w_ref[512:1024, :]).astype(BF16)
    zb_ref[0, 0] = jax.nn.silu(proj_t(w_ref[1024:1536, :])).astype(BF16)

    qit = proj_t(w_ref[1536:2048, :])
    ih = IDX_DIM // 2
    for hh in range(IDX_HEADS):
        xh = qit[hh * IDX_DIM:(hh + 1) * IDX_DIM, :]
        x1, x2 = xh[:ih, :], xh[ih:, :]
        c, s = cosi_ref[...], sini_ref[...]
        base = hh * IDX_DIM
        qi_ref[0, 0, base:base + ih, :] = (x1 * c - x2 * s).astype(BF16)
        qi_ref[0, 0, base + ih:base + IDX_DIM, :] = (x1 * s + x2 * c).astype(BF16)

    wi_ref[0, 0] = proj_t(ww_ref[...])


def _proj_tr(x, ng, w_tr, w_trw, qng, cos_t, sin_t, cosi_t, sini_t):
    b, s, d = x.shape
    nq = s // QB
    full = lambda shape: pl.BlockSpec(shape, lambda bi, i: (0,) * len(shape))
    tabt = lambda rows: pl.BlockSpec((rows, QB), lambda bi, i: (0, i))
    outt = lambda rows: pl.BlockSpec((1, 1, rows, QB), lambda bi, i: (bi, i, 0, 0))
    wrows = w_trw.shape[0]
    return pl.pallas_call(
        _proj_tr_body,
        grid=(b, nq),
        in_specs=[pl.BlockSpec((1, QB, d), lambda bi, i: (bi, i, 0)), full(ng.shape),
                  full(w_tr.shape), full(w_trw.shape), full(qng.shape),
                  tabt(B_HEAD_DIM // 2), tabt(B_HEAD_DIM // 2), tabt(IDX_DIM // 2), tabt(IDX_DIM // 2)],
        out_specs=[outt(B_WIDTH), outt(B_WIDTH), outt(B_WIDTH), outt(IDX_HEADS * IDX_DIM), outt(wrows)],
        out_shape=[jax.ShapeDtypeStruct((b, nq, B_WIDTH, QB), BF16),
                   jax.ShapeDtypeStruct((b, nq, B_WIDTH, QB), BF16),
                   jax.ShapeDtypeStruct((b, nq, B_WIDTH, QB), BF16),
                   jax.ShapeDtypeStruct((b, nq, IDX_HEADS * IDX_DIM, QB), BF16),
                   jax.ShapeDtypeStruct((b, nq, wrows, QB), F32)],
        compiler_params=pltpu.CompilerParams(dimension_semantics=("arbitrary", "arbitrary"),
                                             vmem_limit_bytes=VMEM_LIMIT_BYTES),
        name="proj_tr",
    )(x, ng, w_tr, w_trw, qng, cos_t, sin_t, cosi_t, sini_t)


def _key_to_float(u):
    bits = jnp.where(u < 0, u & jnp.int32(0x7FFFFFFF), ~u)
    return lax.bitcast_convert_type(bits, F32)


def _attn_body(q_ref, qi_ref, wi_ref, zb_ref, k_ref, ki_ref, v_ref, y_ref, s_scr, acc_scr, *, topk):
    n = pl.program_id(1)
    ntiles = n + 1
    idx_scale = (IDX_DIM ** -0.5) * (IDX_HEADS ** -0.5)
    rows_per_tile = TK // SUBLANES

    def score_tile(t, diagonal):
        ki_t = ki_ref[0, pl.ds(pl.multiple_of(t * TK, TK), TK), :]
        acc = jnp.zeros((TK, QB), F32)
        for hh in range(IDX_HEADS):
            logit = jnp.dot(ki_t, qi_ref[0, 0, hh * IDX_DIM:(hh + 1) * IDX_DIM, :],
                            preferred_element_type=F32)
            acc = acc + wi_ref[0, 0, hh:hh + 1, :] * jnp.maximum(logit, 0.0)
        sc = acc * idx_scale
        if diagonal:
            kc = lax.broadcasted_iota(jnp.int32, (TK, QB), 0) // CHUNK
            qc = lax.broadcasted_iota(jnp.int32, (TK, QB), 1) // CHUNK
            sc = jnp.where(kc <= qc, sc, -jnp.inf)
        s_scr[t] = sc

    def score_loop(t, carry):
        score_tile(t, False)
        return carry

    lax.fori_loop(0, n, score_loop, 0)
    score_tile(n, True)

    def count_ge(cand):
        cand_b = jnp.broadcast_to(cand, (SUBLANES, QB))

        def tile_body(t, acc):
            for r in range(rows_per_tile):
                sl = s_scr[t, r * SUBLANES:(r + 1) * SUBLANES, :]
                acc = acc + jnp.where(sl >= cand_b, 1.0, 0.0)
            return acc

        acc = lax.fori_loop(0, ntiles, tile_body, jnp.zeros((SUBLANES, QB), F32))
        return jnp.sum(acc, axis=0, keepdims=True)

    def bit_body(i, carry):
        prefix, cnt_at = carry
        trial = prefix | jnp.left_shift(jnp.int32(1), 31 - i)
        cnt = count_ge(_key_to_float(trial))
        below_neg_inf = (trial >= 0) & (trial < jnp.int32(0x007FFFFF))
        ok = (cnt >= float(topk)) | below_neg_inf
        return jnp.where(ok, trial, prefix), jnp.where(ok, cnt, cnt_at)

    prefix, cnt_at = lax.fori_loop(
        0, 32, bit_body, (jnp.zeros((1, QB), jnp.int32), jnp.zeros((1, QB), F32)))
    thr = _key_to_float(prefix)

    excess0 = jnp.where(thr > -jnp.inf, cnt_at - float(topk), 0.0)

    @pl.when(jnp.max(excess0) > 0.0)
    def _():
        thr_b = jnp.broadcast_to(thr, (SUBLANES, QB))
        sub_iota = lax.broadcasted_iota(jnp.int32, (SUBLANES, QB), 0).astype(F32)

        def rows(r):
            return pl.ds(pl.multiple_of(r * SUBLANES, SUBLANES), SUBLANES)

        def drop_one(excess):
            active_b = jnp.broadcast_to(excess > 0.0, (SUBLANES, QB))

            def min_tile(t, m):
                def row_body(r, m):
                    sl = s_scr[t, rows(r), :]
                    return jnp.minimum(m, jnp.where(sl >= thr_b, sl, jnp.inf))
                return lax.fori_loop(0, rows_per_tile, row_body, m)

            m8 = lax.fori_loop(0, ntiles, min_tile, jnp.full((SUBLANES, QB), jnp.inf, F32))
            m_b = jnp.broadcast_to(jnp.min(m8, axis=0, keepdims=True), (SUBLANES, QB))

            def key_index(t, r):
                return sub_iota + (t * TK + r * SUBLANES).astype(F32)

            def idx_tile(t, j):
                def row_body(r, j):
                    sl = s_scr[t, rows(r), :]
                    return jnp.maximum(j, jnp.where(sl == m_b, key_index(t, r), -1.0))
                return lax.fori_loop(0, rows_per_tile, row_body, j)

            j8 = lax.fori_loop(0, ntiles, idx_tile, jnp.full((SUBLANES, QB), -1.0, F32))
            j_b = jnp.broadcast_to(jnp.max(j8, axis=0, keepdims=True), (SUBLANES, QB))

            def drop_tile(t, carry):
                def row_body(r, carry):
                    sl = s_scr[t, rows(r), :]
                    hit = (key_index(t, r) == j_b) & active_b
                    s_scr[t, rows(r), :] = jnp.where(hit, -jnp.inf, sl)
                    return carry
                return lax.fori_loop(0, rows_per_tile, row_body, carry)

            lax.fori_loop(0, ntiles, drop_tile, 0)
            return excess - jnp.where(excess > 0.0, 1.0, 0.0)

        lax.while_loop(lambda e: jnp.max(e) > 0.0, drop_one, excess0)

    thr_fin = jnp.maximum(thr, F32_LOWEST)
    acc_scr[...] = jnp.zeros_like(acc_scr)

    def att_tile(t, carry):
        ms, ls = carry
        sel = s_scr[t] >= thr_fin
        krows = pl.ds(pl.multiple_of(t * TK, TK), TK)
        new_ms, new_ls = [], []
        for hh in range(B_HEADS):
            hs = slice(hh * B_HEAD_DIM, (hh + 1) * B_HEAD_DIM)
            att = jnp.dot(k_ref[0, krows, hs], q_ref[0, 0, hs, :], preferred_element_type=F32)
            att = jnp.where(sel, att, NEG_BIG)
            m_new = jnp.maximum(ms[hh], jnp.max(att, axis=0, keepdims=True))
            alpha = jnp.exp(ms[hh] - m_new)
            p = jnp.exp(att - m_new)
            new_ls.append(alpha * ls[hh] + jnp.sum(p, axis=0, keepdims=True))
            new_ms.append(m_new)
            pv = jnp.dot(v_ref[0, t, hs, :], p.astype(BF16), preferred_element_type=F32)
            acc_scr[hs, :] = alpha * acc_scr[hs, :] + pv
        return tuple(new_ms), tuple(new_ls)

    init = (tuple(jnp.full((1, QB), NEG_BIG, F32) for _ in range(B_HEADS)),
            tuple(jnp.zeros((1, QB), F32) for _ in range(B_HEADS)))
    _, ls = lax.fori_loop(0, ntiles, att_tile, init)

    parts = []
    for hh in range(B_HEADS):
        hs = slice(hh * B_HEAD_DIM, (hh + 1) * B_HEAD_DIM)
        parts.append(acc_scr[hs, :] / ls[hh] * zb_ref[0, 0, hs, :].astype(F32))
    y_ref[0] = jnp.concatenate(parts, axis=0).T.astype(BF16)


def _attn(qT, qiT, wiT, zbT, k, ki, vT, topk):
    b, nq = qT.shape[0], qT.shape[1]
    s = k.shape[1]
    blk = lambda rows: pl.BlockSpec((1, 1, rows, QB), lambda bi, i: (bi, i, 0, 0))
    return pl.pallas_call(
        functools.partial(_attn_body, topk=topk),
        grid=(b, nq),
        in_specs=[blk(B_WIDTH), blk(IDX_HEADS * IDX_DIM), blk(wiT.shape[2]), blk(B_WIDTH),
                  pl.BlockSpec((1, s, B_WIDTH), lambda bi, i: (bi, 0, 0)),
                  pl.BlockSpec((1, s, IDX_DIM), lambda bi, i: (bi, 0, 0)),
                  pl.BlockSpec((1, nq, B_WIDTH, QB), lambda bi, i: (bi, 0, 0, 0))],
        out_specs=pl.BlockSpec((1, QB, B_WIDTH), lambda bi, i: (bi, i, 0)),
        out_shape=jax.ShapeDtypeStruct((b, s, B_WIDTH), BF16),
        scratch_shapes=[pltpu.VMEM((s // TK, TK, QB), F32), pltpu.VMEM((B_WIDTH, QB), F32)],
        compiler_params=pltpu.CompilerParams(dimension_semantics=("arbitrary", "arbitrary"),
                                             vmem_limit_bytes=VMEM_LIMIT_BYTES),
        name="attn",
    )(qT, qiT, wiT, zbT, k, ki, vT)


def _out_body(x_ref, yb_ref, ma_ref, sgb_ref, wob_ref, wout_ref, o_ref):
    o_b = jnp.dot(yb_ref[...], wob_ref[...], preferred_element_type=F32)
    merged = ma_ref[...].astype(F32) + sgb_ref[...].astype(F32) * o_b
    o_ref[...] = x_ref[...] + jnp.dot(merged.astype(BF16), wout_ref[...], preferred_element_type=F32)


def _out(x2d, yb, ma, sgb, wob, wout):
    m, d = x2d.shape
    tm = TM_OUT
    row = lambda width: pl.BlockSpec((tm, width), lambda i: (i, 0))
    full = lambda shape: pl.BlockSpec(shape, lambda i: (0,) * len(shape))
    return pl.pallas_call(
        _out_body,
        grid=(m // tm,),
        in_specs=[row(d), row(B_WIDTH), row(d), row(d), full(wob.shape), full(wout.shape)],
        out_specs=row(d),
        out_shape=jax.ShapeDtypeStruct((m, d), F32),
        compiler_params=pltpu.CompilerParams(dimension_semantics=("arbitrary",),
                                             vmem_limit_bytes=VMEM_LIMIT_BYTES),
        name="out_proj",
    )(x2d, yb, ma, sgb, wob, wout)


def _rope_tables(s, dim):
    pos = jnp.arange(s, dtype=F32)
    inv = ROPE_THETA ** (-jnp.arange(0, dim, 2, dtype=F32) / dim)
    ang = pos[:, None] * inv[None, :]
    return jnp.cos(ang), jnp.sin(ang)


def kernel(x, norm_g, w_in, gate_b, a_ln_g, a_ln_b, a_ws, a_bs, q_norm_g, k_norm_g, w_oa, w_ob, w_out):
    b, s, d = x.shape
    depth = w_in.shape[0]
    topk = min(TOPK_MAX, s // 4)
    assert s % TM_STD == 0 and s % QB == 0 and (b * s) % TM_OUT == 0 and QB == TK
    assert w_in.shape[2] == _O_END and topk <= TK

    cos_q, sin_q = _rope_tables(s, B_HEAD_DIM)
    cos_i, sin_i = _rope_tables(s, IDX_DIM)
    cos2 = jnp.concatenate([cos_q, cos_q], axis=1)
    sin2 = jnp.concatenate([-sin_q, sin_q], axis=1)
    cosi2 = jnp.concatenate([cos_i, cos_i, cos_i, cos_i], axis=1)
    sini2 = jnp.concatenate([-sin_i, sin_i, -sin_i, sin_i], axis=1)
    cos_t, sin_t, cosi_t, sini_t = cos_q.T, sin_q.T, cos_i.T, sin_i.T

    ik_cols = w_in[:, :, _O_IK:_O_IW]
    w_std = jnp.concatenate([w_in[:, :, _O_AU:_O_BQ], w_in[:, :, _O_BK:_O_BV], ik_cols, ik_cols,
                             w_in[:, :, _O_GA:_O_END]], axis=2).astype(BF16)
    w_tr = jnp.swapaxes(jnp.concatenate([w_in[:, :, _O_BQ:_O_BK], w_in[:, :, _O_BV:_O_IK]], axis=2),
                        1, 2).astype(BF16)
    iw_t = jnp.swapaxes(w_in[:, :, _O_IW:_O_GA], 1, 2)
    w_trw = jnp.concatenate([iw_t, jnp.zeros_like(iw_t)], axis=1).astype(BF16)
    woa, wob, wout = w_oa.astype(BF16), w_ob.astype(BF16), w_out.astype(BF16)
    bias_full = jnp.repeat(jnp.swapaxes(a_bs, 1, 2), A_WIDTH // A_GROUPS, axis=2)
    kng = jnp.reshape(k_norm_g, (depth, 1, B_HEAD_DIM))
    qng = jnp.reshape(q_norm_g, (depth, B_HEAD_DIM, 1))

    x2d = jnp.reshape(x, (b * s, d))
    for l in range(depth):
        ng = norm_g[l][None, :]
        ma, sgb, k, ki = _proj_std(x2d, ng, w_std[l], gate_b[l], a_ln_g[l][None, :], a_ln_b[l][None, :],
                                   a_ws[l], bias_full[l], kng[l], woa[l], cos2, sin2, cosi2, sini2, s)
        qT, vT, zbT, qiT, wiT = _proj_tr(jnp.reshape(x2d, (b, s, d)), ng, w_tr[l], w_trw[l], qng[l],
                                         cos_t, sin_t, cosi_t, sini_t)
        yb = _attn(qT, qiT, wiT, zbT, jnp.reshape(k, (b, s, B_WIDTH)), jnp.reshape(ki, (b, s, IDX_DIM)),
                   vT, topk)
        x2d = _out(x2d, jnp.reshape(yb, (b * s, B_WIDTH)), ma, sgb, wob[l], wout[l])
    return jnp.reshape(x2d, (b, s, d))
```

```python
import functools

import jax
import jax.numpy as jnp
from jax import lax
from jax.experimental import pallas as pl
from jax.experimental.pallas import tpu as pltpu

F32 = jnp.float32
BF16 = jnp.bfloat16

CHUNK = 64
EPS = 1e-6
ROPE_THETA = 10000.0

A_WIDTH = 512
A_GROUPS = 4
A_BLOCK = 128
B_HEADS = 4
B_HEAD_DIM = 128
B_WIDTH = B_HEADS * B_HEAD_DIM
IDX_HEADS = 8
IDX_DIM = 64
TOPK_MAX = 256

_O_AU, _O_AV, _O_AZ = 0, 512, 1024
_O_BQ, _O_BK, _O_BV, _O_BZ = 1536, 2048, 2560, 3072
_O_IQ, _O_IK, _O_IW = 3584, 4096, 4160
_O_GA, _O_GB, _O_END = 4168, 5192, 6216

LANES = 128
SUBLANES = 8
VMEM_LIMIT_BYTES = 52 * 1024 * 1024

QB = 256
TK = 256
TM_STD = 512
TM_OUT = 512
NEG_BIG = -1e30
F32_LOWEST = -3.0e38


def _rms_rows(x, g):
    ms = jnp.mean(x * x, axis=-1, keepdims=True)
    return x * lax.rsqrt(ms + EPS) * g


def _proj_std_body(x_ref, ng_ref, w_ref, gb_ref, lng_ref, lnb_ref, ws_ref, bias_ref, kng_ref,
                   woa_ref, cos_ref, sin_ref, cosi_ref, sini_ref,
                   ma_ref, sgb_ref, k_ref, ki_ref):
    tm = x_ref.shape[0]
    h = _rms_rows(x_ref[...], ng_ref[...]).astype(BF16)

    def proj(lo, hi):
        return jnp.dot(h, w_ref[:, lo:hi], preferred_element_type=F32)

    gv = jax.nn.gelu(proj(512, 1024))
    mu = jnp.mean(gv, axis=-1, keepdims=True)
    xc = gv - mu
    var = jnp.mean(xc * xc, axis=-1, keepdims=True)
    vn = (xc * lax.rsqrt(var + EPS) * lng_ref[...] + lnb_ref[...]).astype(BF16)

    ci = lax.broadcasted_iota(jnp.int32, (A_BLOCK, A_BLOCK), 0) // CHUNK
    cj = lax.broadcasted_iota(jnp.int32, (A_BLOCK, A_BLOCK), 1) // CHUNK
    causal = cj <= ci
    wm = [jnp.where(causal, ws_ref[g], 0.0).astype(BF16) for g in range(A_GROUPS)]
    row_blocks = []
    for r in range(tm // A_BLOCK):
        cols = []
        for g in range(A_GROUPS):
            vb = vn[r * A_BLOCK:(r + 1) * A_BLOCK, g * LANES:(g + 1) * LANES]
            cols.append(jnp.dot(wm[g], vb, preferred_element_type=F32))
        row_blocks.append(jnp.concatenate(cols, axis=1) + bias_ref[...])
    mixed = jnp.concatenate(row_blocks, axis=0)

    y_a = jax.nn.gelu(proj(0, 512)) * mixed * jax.nn.silu(proj(1024, 1536))
    o_a = jnp.dot(y_a.astype(BF16), woa_ref[...], preferred_element_type=F32)
    ma_ref[...] = (jax.nn.sigmoid(proj(2176, 3200) + gb_ref[0:1, :]) * o_a).astype(BF16)
    sgb_ref[...] = jax.nn.sigmoid(proj(3200, 4224) + gb_ref[1:2, :]).astype(BF16)

    b_k = proj(1536, 2048)
    for hh in range(B_HEADS):
        kh = _rms_rows(b_k[:, hh * LANES:(hh + 1) * LANES], kng_ref[...])
        kh = kh * cos_ref[...] + pltpu.roll(kh, B_HEAD_DIM // 2, 1) * sin_ref[...]
        k_ref[:, hh * LANES:(hh + 1) * LANES] = kh.astype(BF16)

    ik = proj(2048, 2176)
    ik = ik * cosi_ref[...] + pltpu.roll(ik, IDX_DIM // 2, 1) * sini_ref[...]
    ki_ref[...] = ik[:, :IDX_DIM].astype(BF16)


def _proj_std(x2d, ng, w_std, gb, lng, lnb, ws, bias_full, kng, woa, cos2, sin2, cosi2, sini2, seq):
    m, d = x2d.shape
    tm = TM_STD
    nt = seq // tm
    full = lambda shape: pl.BlockSpec(shape, lambda i: (0,) * len(shape))
    tab = lambda: pl.BlockSpec((tm, LANES), lambda i: (i % nt, 0))
    row = lambda width: pl.BlockSpec((tm, width), lambda i: (i, 0))
    return pl.pallas_call(
        _proj_std_body,
        grid=(m // tm,),
        in_specs=[row(d), full(ng.shape), full(w_std.shape), full(gb.shape), full(lng.shape),
                  full(lnb.shape), full(ws.shape), full(bias_full.shape), full(kng.shape),
                  full(woa.shape), tab(), tab(), tab(), tab()],
        out_specs=[row(d), row(d), row(B_WIDTH), row(IDX_DIM)],
        out_shape=[jax.ShapeDtypeStruct((m, d), BF16), jax.ShapeDtypeStruct((m, d), BF16),
                   jax.ShapeDtypeStruct((m, B_WIDTH), BF16),
                   jax.ShapeDtypeStruct((m, IDX_DIM), BF16)],
        compiler_params=pltpu.CompilerParams(dimension_semantics=("arbitrary",),
                                             vmem_limit_bytes=VMEM_LIMIT_BYTES),
        name="proj_std",
    )(x2d, ng, w_std, gb, lng, lnb, ws, bias_full, kng, woa, cos2, sin2, cosi2, sini2)


def _proj_tr_body(x_ref, ng_ref, w_ref, ww_ref, qng_ref, cos_ref, sin_ref, cosi_ref, sini_ref,
                  q_ref, v_ref, zb_ref, qi_ref, wi_ref):
    h = _rms_rows(x_ref[0], ng_ref[...]).astype(BF16)

    def proj_t(w):
        return lax.dot_general(w, h, (((1,), (1,)), ((), ())), preferred_element_type=F32)

    att_scale = B_HEAD_DIM ** -0.5
    half = B_HEAD_DIM // 2
    qt = proj_t(w_ref[0:512, :])
    for hh in range(B_HEADS):
        qh = qt[hh * B_HEAD_DIM:(hh + 1) * B_HEAD_DIM, :]
        ms = jnp.mean(qh * qh, axis=0, keepdims=True)
        qh = qh * lax.rsqrt(ms + EPS) * qng_ref[...]
        x1, x2 = qh[:half, :], qh[half:, :]
        c, s = cos_ref[...], sin_ref[...]
        base = hh * B_HEAD_DIM
        q_ref[0, 0, base:base + half, :] = ((x1 * c - x2 * s) * att_scale).astype(BF16)
        q_ref[0, 0, base + half:base + B_HEAD_DIM, :] = ((x1 * s + x2 * c) * att_scale).astype(BF16)

    v_ref[0, 0] = proj_t(w_ref[512:1024, :]).astype(BF16)
    zb_ref[0, 0] = jax.nn.silu(proj_t(w_ref[1024:1536, :])).astype(BF16)

    qit = proj_t(w_ref[1536:2048, :])
    ih = IDX_DIM // 2
    for hh in range(IDX_HEADS):
        xh = qit[hh * IDX_DIM:(hh + 1) * IDX_DIM, :]
        x1, x2 = xh[:ih, :], xh[ih:, :]
        c, s = cosi_ref[...], sini_ref[...]
        base = hh * IDX_DIM
        qi_ref[0, 0, base:base + ih, :] = (x1 * c - x2 * s).astype(BF16)
        qi_ref[0, 0, base + ih:base + IDX_DIM, :] = (x1 * s + x2 * c).astype(BF16)

    wi_ref[0, 0] = proj_t(ww_ref[...])


def _proj_tr(x, ng, w_tr, w_trw, qng, cos_t, sin_t, cosi_t, sini_t):
    b, s, d = x.shape
    nq = s // QB
    full = lambda shape: pl.BlockSpec(shape, lambda bi, i: (0,) * len(shape))
    tabt = lambda rows: pl.BlockSpec((rows, QB), lambda bi, i: (0, i))
    outt = lambda rows: pl.BlockSpec((1, 1, rows, QB), lambda bi, i: (bi, i, 0, 0))
    wrows = w_trw.shape[0]
    return pl.pallas_call(
        _proj_tr_body,
        grid=(b, nq),
        in_specs=[pl.BlockSpec((1, QB, d), lambda bi, i: (bi, i, 0)), full(ng.shape),
                  full(w_tr.shape), full(w_trw.shape), full(qng.shape),
                  tabt(B_HEAD_DIM // 2), tabt(B_HEAD_DIM // 2), tabt(IDX_DIM // 2), tabt(IDX_DIM // 2)],
        out_specs=[outt(B_WIDTH), outt(B_WIDTH), outt(B_WIDTH), outt(IDX_HEADS * IDX_DIM), outt(wrows)],
        out_shape=[jax.ShapeDtypeStruct((b, nq, B_WIDTH, QB), BF16),
                   jax.ShapeDtypeStruct((b, nq, B_WIDTH, QB), BF16),
                   jax.ShapeDtypeStruct((b, nq, B_WIDTH, QB), BF16),
                   jax.ShapeDtypeStruct((b, nq, IDX_HEADS * IDX_DIM, QB), BF16),
                   jax.ShapeDtypeStruct((b, nq, wrows, QB), F32)],
        compiler_params=pltpu.CompilerParams(dimension_semantics=("arbitrary", "arbitrary"),
                                             vmem_limit_bytes=VMEM_LIMIT_BYTES),
        name="proj_tr",
    )(x, ng, w_tr, w_trw, qng, cos_t, sin_t, cosi_t, sini_t)


def _key_to_float(u):
    bits = jnp.where(u < 0, u & jnp.int32(0x7FFFFFFF), ~u)
    return lax.bitcast_convert_type(bits, F32)


def _attn_body(q_ref, qi_ref, wi_ref, zb_ref, k_ref, ki_ref, v_ref, y_ref, s_scr, att_scr, acc_scr,
               *, topk):
    n = pl.program_id(1)
    ntiles = n + 1
    idx_scale = (IDX_DIM ** -0.5) * (IDX_HEADS ** -0.5)
    rows_per_tile = TK // SUBLANES

    def score_tile(t, diagonal):
        ki_t = ki_ref[0, pl.ds(pl.multiple_of(t * TK, TK), TK), :]
        acc = jnp.zeros((TK, QB), F32)
        for hh in range(IDX_HEADS):
            logit = jnp.dot(ki_t, qi_ref[0, 0, hh * IDX_DIM:(hh + 1) * IDX_DIM, :],
                            preferred_element_type=F32)
            acc = acc + wi_ref[0, 0, hh:hh + 1, :] * jnp.maximum(logit, 0.0)
        sc = acc * idx_scale
        if diagonal:
            kc = lax.broadcasted_iota(jnp.int32, (TK, QB), 0) // CHUNK
            qc = lax.broadcasted_iota(jnp.int32, (TK, QB), 1) // CHUNK
            sc = jnp.where(kc <= qc, sc, -jnp.inf)
        s_scr[t] = sc

    def score_loop(t, carry):
        score_tile(t, False)
        return carry

    lax.fori_loop(0, n, score_loop, 0)
    score_tile(n, True)

    def count_ge(cand):
        cand_b = jnp.broadcast_to(cand, (SUBLANES, QB))

        def tile_body(t, acc):
            for r in range(rows_per_tile):
                sl = s_scr[t, r * SUBLANES:(r + 1) * SUBLANES, :]
                acc = acc + jnp.where(sl >= cand_b, 1.0, 0.0)
            return acc

        acc = lax.fori_loop(0, ntiles, tile_body, jnp.zeros((SUBLANES, QB), F32))
        return jnp.sum(acc, axis=0, keepdims=True)

    def bit_body(i, carry):
        prefix, cnt_at = carry
        trial = prefix | jnp.left_shift(jnp.int32(1), 31 - i)
        cnt = count_ge(_key_to_float(trial))
        below_neg_inf = (trial >= 0) & (trial < jnp.int32(0x007FFFFF))
        ok = (cnt >= float(topk)) | below_neg_inf
        return jnp.where(ok, trial, prefix), jnp.where(ok, cnt, cnt_at)

    prefix, cnt_at = lax.fori_loop(
        0, 32, bit_body, (jnp.zeros((1, QB), jnp.int32), jnp.zeros((1, QB), F32)))
    thr = _key_to_float(prefix)

    excess0 = jnp.where(thr > -jnp.inf, cnt_at - float(topk), 0.0)

    @pl.when(jnp.max(excess0) > 0.0)
    def _():
        thr_b = jnp.broadcast_to(thr, (SUBLANES, QB))
        sub_iota = lax.broadcasted_iota(jnp.int32, (SUBLANES, QB), 0).astype(F32)

        def slab(r):
            return slice(r * SUBLANES, (r + 1) * SUBLANES)

        def key_index(t, r):
            return sub_iota + ((t * TK).astype(F32) + float(r * SUBLANES))

        def drop_one(excess):
            active_b = jnp.broadcast_to(excess > 0.0, (SUBLANES, QB))

            def min_tile(t, m):
                for r in range(rows_per_tile):
                    sl = s_scr[t, slab(r), :]
                    m = jnp.minimum(m, jnp.where(sl >= thr_b, sl, jnp.inf))
                return m

            m8 = lax.fori_loop(0, ntiles, min_tile, jnp.full((SUBLANES, QB), jnp.inf, F32))
            m_b = jnp.broadcast_to(jnp.min(m8, axis=0, keepdims=True), (SUBLANES, QB))

            def idx_tile(t, j):
                for r in range(rows_per_tile):
                    sl = s_scr[t, slab(r), :]
                    j = jnp.maximum(j, jnp.where(sl == m_b, key_index(t, r), -1.0))
                return j

            j8 = lax.fori_loop(0, ntiles, idx_tile, jnp.full((SUBLANES, QB), -1.0, F32))
            j_b = jnp.broadcast_to(jnp.max(j8, axis=0, keepdims=True), (SUBLANES, QB))

            def drop_tile(t, carry):
                for r in range(rows_per_tile):
                    sl = s_scr[t, slab(r), :]
                    hit = (key_index(t, r) == j_b) & active_b
                    s_scr[t, slab(r), :] = jnp.where(hit, -jnp.inf, sl)
                return carry

            lax.fori_loop(0, ntiles, drop_tile, 0)
            return excess - jnp.where(excess > 0.0, 1.0, 0.0)

        lax.while_loop(lambda e: jnp.max(e) > 0.0, drop_one, excess0)

    thr_fin = jnp.maximum(thr, F32_LOWEST)

    def slab_reduce(x, op):
        parts = [x[r * SUBLANES:(r + 1) * SUBLANES, :] for r in range(rows_per_tile)]
        while len(parts) > 1:
            parts = [op(parts[i], parts[i + 1]) for i in range(0, len(parts), 2)]
        return parts[0]

    def logit_tile(t, m8):
        sel = s_scr[t] >= thr_fin
        krows = pl.ds(pl.multiple_of(t * TK, TK), TK)
        new_m8 = []
        for hh in range(B_HEADS):
            hs = slice(hh * B_HEAD_DIM, (hh + 1) * B_HEAD_DIM)
            att = jnp.dot(k_ref[0, krows, hs], q_ref[0, 0, hs, :], preferred_element_type=F32)
            att = jnp.where(sel, att, NEG_BIG)
            att_scr[t * B_HEADS + hh] = att
            new_m8.append(jnp.maximum(m8[hh], slab_reduce(att, jnp.maximum)))
        return tuple(new_m8)

    m8 = lax.fori_loop(0, ntiles, logit_tile,
                       tuple(jnp.full((SUBLANES, QB), NEG_BIG, F32) for _ in range(B_HEADS)))
    m_row = [jnp.max(m, axis=0, keepdims=True) for m in m8]

    acc_scr[...] = jnp.zeros_like(acc_scr)

    def value_tile(t, l8):
        new_l8 = []
        for hh in range(B_HEADS):
            hs = slice(hh * B_HEAD_DIM, (hh + 1) * B_HEAD_DIM)
            p = jnp.exp(att_scr[t * B_HEADS + hh] - m_row[hh])
            new_l8.append(l8[hh] + slab_reduce(p, jnp.add))
            acc_scr[hs, :] += jnp.dot(v_ref[0, t, hs, :], p.astype(BF16), preferred_element_type=F32)
        return tuple(new_l8)

    l8 = lax.fori_loop(0, ntiles, value_tile,
                       tuple(jnp.zeros((SUBLANES, QB), F32) for _ in range(B_HEADS)))

    parts = []
    for hh in range(B_HEADS):
        hs = slice(hh * B_HEAD_DIM, (hh + 1) * B_HEAD_DIM)
        l_row = jnp.sum(l8[hh], axis=0, keepdims=True)
        parts.append(acc_scr[hs, :] / l_row * zb_ref[0, 0, hs, :].astype(F32))
    y_ref[0] = jnp.concatenate(parts, axis=0).T.astype(BF16)


def _attn(qT, qiT, wiT, zbT, k, ki, vT, topk):
    b, nq = qT.shape[0], qT.shape[1]
    s = k.shape[1]
    blk = lambda rows: pl.BlockSpec((1, 1, rows, QB), lambda bi, i: (bi, i, 0, 0))
    return pl.pallas_call(
        functools.partial(_attn_body, topk=topk),
        grid=(b, nq),
        in_specs=[blk(B_WIDTH), blk(IDX_HEADS * IDX_DIM), blk(wiT.shape[2]), blk(B_WIDTH),
                  pl.BlockSpec((1, s, B_WIDTH), lambda bi, i: (bi, 0, 0)),
                  pl.BlockSpec((1, s, IDX_DIM), lambda bi, i: (bi, 0, 0)),
                  pl.BlockSpec((1, nq, B_WIDTH, QB), lambda bi, i: (bi, 0, 0, 0))],
        out_specs=pl.BlockSpec((1, QB, B_WIDTH), lambda bi, i: (bi, i, 0)),
        out_shape=jax.ShapeDtypeStruct((b, s, B_WIDTH), BF16),
        scratch_shapes=[pltpu.VMEM((s // TK, TK, QB), F32),
                        pltpu.VMEM((s // TK * B_HEADS, TK, QB), F32),
                        pltpu.VMEM((B_WIDTH, QB), F32)],
        compiler_params=pltpu.CompilerParams(dimension_semantics=("arbitrary", "arbitrary"),
                                             vmem_limit_bytes=VMEM_LIMIT_BYTES),
        name="attn",
    )(qT, qiT, wiT, zbT, k, ki, vT)


def _out_body(x_ref, yb_ref, ma_ref, sgb_ref, wob_ref, wout_ref, o_ref):
    o_b = jnp.dot(yb_ref[...], wob_ref[...], preferred_element_type=F32)
    merged = ma_ref[...].astype(F32) + sgb_ref[...].astype(F32) * o_b
    o_ref[...] = x_ref[...] + jnp.dot(merged.astype(BF16), wout_ref[...], preferred_element_type=F32)


def _out(x2d, yb, ma, sgb, wob, wout):
    m, d = x2d.shape
    tm = TM_OUT
    row = lambda width: pl.BlockSpec((tm, width), lambda i: (i, 0))
    full = lambda shape: pl.BlockSpec(shape, lambda i: (0,) * len(shape))
    return pl.pallas_call(
        _out_body,
        grid=(m // tm,),
        in_specs=[row(d), row(B_WIDTH), row(d), row(d), full(wob.shape), full(wout.shape)],
        out_specs=row(d),
        out_shape=jax.ShapeDtypeStruct((m, d), F32),
        compiler_params=pltpu.CompilerParams(dimension_semantics=("arbitrary",),
                                             vmem_limit_bytes=VMEM_LIMIT_BYTES),
        name="out_proj",
    )(x2d, yb, ma, sgb, wob, wout)


def _rope_tables(s, dim):
    pos = jnp.arange(s, dtype=F32)
    inv = ROPE_THETA ** (-jnp.arange(0, dim, 2, dtype=F32) / dim)
    ang = pos[:, None] * inv[None, :]
    return jnp.cos(ang), jnp.sin(ang)


def kernel(x, norm_g, w_in, gate_b, a_ln_g, a_ln_b, a_ws, a_bs, q_norm_g, k_norm_g, w_oa, w_ob, w_out):
    b, s, d = x.shape
    depth = w_in.shape[0]
    topk = min(TOPK_MAX, s // 4)
    assert s % TM_STD == 0 and s % QB == 0 and (b * s) % TM_OUT == 0 and QB == TK
    assert w_in.shape[2] == _O_END and topk <= TK

    cos_q, sin_q = _rope_tables(s, B_HEAD_DIM)
    cos_i, sin_i = _rope_tables(s, IDX_DIM)
    cos2 = jnp.concatenate([cos_q, cos_q], axis=1)
    sin2 = jnp.concatenate([-sin_q, sin_q], axis=1)
    cosi2 = jnp.concatenate([cos_i, cos_i, cos_i, cos_i], axis=1)
    sini2 = jnp.concatenate([-sin_i, sin_i, -sin_i, sin_i], axis=1)
    cos_t, sin_t, cosi_t, sini_t = cos_q.T, sin_q.T, cos_i.T, sin_i.T

    ik_cols = w_in[:, :, _O_IK:_O_IW]
    w_std = jnp.concatenate([w_in[:, :, _O_AU:_O_BQ], w_in[:, :, _O_BK:_O_BV], ik_cols, ik_cols,
                             w_in[:, :, _O_GA:_O_END]], axis=2).astype(BF16)
    w_tr = jnp.swapaxes(jnp.concatenate([w_in[:, :, _O_BQ:_O_BK], w_in[:, :, _O_BV:_O_IK]], axis=2),
                        1, 2).astype(BF16)
    iw_t = jnp.swapaxes(w_in[:, :, _O_IW:_O_GA], 1, 2)
    w_trw = jnp.concatenate([iw_t, jnp.zeros_like(iw_t)], axis=1).astype(BF16)
    woa, wob, wout = w_oa.astype(BF16), w_ob.astype(BF16), w_out.astype(BF16)
    bias_full = jnp.repeat(jnp.swapaxes(a_bs, 1, 2), A_WIDTH // A_GROUPS, axis=2)
    kng = jnp.reshape(k_norm_g, (depth, 1, B_HEAD_DIM))
    qng = jnp.reshape(q_norm_g, (depth, B_HEAD_DIM, 1))

    x2d = jnp.reshape(x, (b * s, d))
    for l in range(depth):
        ng = norm_g[l][None, :]
        ma, sgb, k, ki = _proj_std(x2d, ng, w_std[l], gate_b[l], a_ln_g[l][None, :], a_ln_b[l][None, :],
                                   a_ws[l], bias_full[l], kng[l], woa[l], cos2, sin2, cosi2, sini2, s)
        qT, vT, zbT, qiT, wiT = _proj_tr(jnp.reshape(x2d, (b, s, d)), ng, w_tr[l], w_trw[l], qng[l],
                                         cos_t, sin_t, cosi_t, sini_t)
        yb = _attn(qT, qiT, wiT, zbT, jnp.reshape(k, (b, s, B_WIDTH)), jnp.reshape(ki, (b, s, IDX_DIM)),
                   vT, topk)
        x2d = _out(x2d, jnp.reshape(yb, (b * s, B_WIDTH)), ma, sgb, wob[l], wout[l])
    return jnp.reshape(x2d, (b, s, d))
```

```python
import functools

import jax
import jax.numpy as jnp
from jax import lax
from jax.experimental import pallas as pl
from jax.experimental.pallas import tpu as pltpu

F32 = jnp.float32
BF16 = jnp.bfloat16

CHUNK = 64
EPS = 1e-6
ROPE_THETA = 10000.0

A_WIDTH = 512
A_GROUPS = 4
A_BLOCK = 128
B_HEADS = 4
B_HEAD_DIM = 128
B_WIDTH = B_HEADS * B_HEAD_DIM
IDX_HEADS = 8
IDX_DIM = 64
TOPK_MAX = 256

_O_AU, _O_AV, _O_AZ = 0, 512, 1024
_O_BQ, _O_BK, _O_BV, _O_BZ = 1536, 2048, 2560, 3072
_O_IQ, _O_IK, _O_IW = 3584, 4096, 4160
_O_GA, _O_GB, _O_END = 4168, 5192, 6216

LANES = 128
SUBLANES = 8
VMEM_LIMIT_BYTES = 52 * 1024 * 1024

QB = 256
TK = 256
TM_STD = 512
TM_OUT = 512
NEG_BIG = -1e30
RANK_BITS = 23
RANK_MAX = float(2 ** RANK_BITS - 1)
COUNT_CHAINS = 4


def _rms_rows(x, g):
    ms = jnp.mean(x * x, axis=-1, keepdims=True)
    return x * lax.rsqrt(ms + EPS) * g


def _proj_std_body(x_ref, ng_ref, w_ref, gb_ref, lng_ref, lnb_ref, ws_ref, bias_ref, kng_ref,
                   woa_ref, cos_ref, sin_ref, cosi_ref, sini_ref,
                   ma_ref, sgb_ref, k_ref, ki_ref):
    tm = x_ref.shape[0]
    h = _rms_rows(x_ref[...], ng_ref[...]).astype(BF16)

    def proj(lo, hi):
        return jnp.dot(h, w_ref[:, lo:hi], preferred_element_type=F32)

    gv = jax.nn.gelu(proj(512, 1024))
    mu = jnp.mean(gv, axis=-1, keepdims=True)
    xc = gv - mu
    var = jnp.mean(xc * xc, axis=-1, keepdims=True)
    vn = (xc * lax.rsqrt(var + EPS) * lng_ref[...] + lnb_ref[...]).astype(BF16)

    ci = lax.broadcasted_iota(jnp.int32, (A_BLOCK, A_BLOCK), 0) // CHUNK
    cj = lax.broadcasted_iota(jnp.int32, (A_BLOCK, A_BLOCK), 1) // CHUNK
    causal = cj <= ci
    wm = [jnp.where(causal, ws_ref[g], 0.0).astype(BF16) for g in range(A_GROUPS)]
    row_blocks = []
    for r in range(tm // A_BLOCK):
        cols = []
        for g in range(A_GROUPS):
            vb = vn[r * A_BLOCK:(r + 1) * A_BLOCK, g * LANES:(g + 1) * LANES]
            cols.append(jnp.dot(wm[g], vb, preferred_element_type=F32))
        row_blocks.append(jnp.concatenate(cols, axis=1) + bias_ref[...])
    mixed = jnp.concatenate(row_blocks, axis=0)

    y_a = jax.nn.gelu(proj(0, 512)) * mixed * jax.nn.silu(proj(1024, 1536))
    o_a = jnp.dot(y_a.astype(BF16), woa_ref[...], preferred_element_type=F32)
    ma_ref[...] = (jax.nn.sigmoid(proj(2176, 3200) + gb_ref[0:1, :]) * o_a).astype(BF16)
    sgb_ref[...] = jax.nn.sigmoid(proj(3200, 4224) + gb_ref[1:2, :]).astype(BF16)

    b_k = proj(1536, 2048)
    for hh in range(B_HEADS):
        kh = _rms_rows(b_k[:, hh * LANES:(hh + 1) * LANES], kng_ref[...])
        kh = kh * cos_ref[...] + pltpu.roll(kh, B_HEAD_DIM // 2, 1) * sin_ref[...]
        k_ref[:, hh * LANES:(hh + 1) * LANES] = kh.astype(BF16)

    ik = proj(2048, 2176)
    ik = ik * cosi_ref[...] + pltpu.roll(ik, IDX_DIM // 2, 1) * sini_ref[...]
    ki_ref[...] = ik[:, :IDX_DIM].astype(BF16)


def _proj_std(x2d, ng, w_std, gb, lng, lnb, ws, bias_full, kng, woa, cos2, sin2, cosi2, sini2, seq):
    m, d = x2d.shape
    tm = TM_STD
    nt = seq // tm
    full = lambda shape: pl.BlockSpec(shape, lambda i: (0,) * len(shape))
    tab = lambda: pl.BlockSpec((tm, LANES), lambda i: (i % nt, 0))
    row = lambda width: pl.BlockSpec((tm, width), lambda i: (i, 0))
    return pl.pallas_call(
        _proj_std_body,
        grid=(m // tm,),
        in_specs=[row(d), full(ng.shape), full(w_std.shape), full(gb.shape), full(lng.shape),
                  full(lnb.shape), full(ws.shape), full(bias_full.shape), full(kng.shape),
                  full(woa.shape), tab(), tab(), tab(), tab()],
        out_specs=[row(d), row(d), row(B_WIDTH), row(IDX_DIM)],
        out_shape=[jax.ShapeDtypeStruct((m, d), BF16), jax.ShapeDtypeStruct((m, d), BF16),
                   jax.ShapeDtypeStruct((m, B_WIDTH), BF16),
                   jax.ShapeDtypeStruct((m, IDX_DIM), BF16)],
        compiler_params=pltpu.CompilerParams(dimension_semantics=("arbitrary",),
                                             vmem_limit_bytes=VMEM_LIMIT_BYTES),
        name="proj_std",
    )(x2d, ng, w_std, gb, lng, lnb, ws, bias_full, kng, woa, cos2, sin2, cosi2, sini2)


def _proj_tr_body(x_ref, ng_ref, w_ref, ww_ref, qng_ref, cos_ref, sin_ref, cosi_ref, sini_ref,
                  q_ref, v_ref, zb_ref, qi_ref, wi_ref):
    h = _rms_rows(x_ref[0], ng_ref[...]).astype(BF16)

    def proj_t(w):
        return lax.dot_general(w, h, (((1,), (1,)), ((), ())), preferred_element_type=F32)

    att_scale = B_HEAD_DIM ** -0.5
    half = B_HEAD_DIM // 2
    qt = proj_t(w_ref[0:512, :])
    for hh in range(B_HEADS):
        qh = qt[hh * B_HEAD_DIM:(hh + 1) * B_HEAD_DIM, :]
        ms = jnp.mean(qh * qh, axis=0, keepdims=True)
        qh = qh * lax.rsqrt(ms + EPS) * qng_ref[...]
        x1, x2 = qh[:half, :], qh[half:, :]
        c, s = cos_ref[...], sin_ref[...]
        base = hh * B_HEAD_DIM
        q_ref[0, 0, base:base + half, :] = ((x1 * c - x2 * s) * att_scale).astype(BF16)
        q_ref[0, 0, base + half:base + B_HEAD_DIM, :] = ((x1 * s + x2 * c) * att_scale).astype(BF16)

    v_ref[0, 0] = proj_t(w_ref[512:1024, :]).astype(BF16)
    zb_ref[0, 0] = jax.nn.silu(proj_t(w_ref[1024:1536, :])).astype(BF16)

    qit = proj_t(w_ref[1536:2048, :])
    ih = IDX_DIM // 2
    for hh in range(IDX_HEADS):
        xh = qit[hh * IDX_DIM:(hh + 1) * IDX_DIM, :]
        x1, x2 = xh[:ih, :], xh[ih:, :]
        c, s = cosi_ref[...], sini_ref[...]
        base = hh * IDX_DIM
        qi_ref[0, 0, base:base + ih, :] = (x1 * c - x2 * s).astype(BF16)
        qi_ref[0, 0, base + ih:base + IDX_DIM, :] = (x1 * s + x2 * c).astype(BF16)

    wi_ref[0, 0] = proj_t(ww_ref[...])


def _proj_tr(x, ng, w_tr, w_trw, qng, cos_t, sin_t, cosi_t, sini_t):
    b, s, d = x.shape
    nq = s // QB
    full = lambda shape: pl.BlockSpec(shape, lambda bi, i: (0,) * len(shape))
    tabt = lambda rows: pl.BlockSpec((rows, QB), lambda bi, i: (0, i))
    outt = lambda rows: pl.BlockSpec((1, 1, rows, QB), lambda bi, i: (bi, i, 0, 0))
    wrows = w_trw.shape[0]
    return pl.pallas_call(
        _proj_tr_body,
        grid=(b, nq),
        in_specs=[pl.BlockSpec((1, QB, d), lambda bi, i: (bi, i, 0)), full(ng.shape),
                  full(w_tr.shape), full(w_trw.shape), full(qng.shape),
                  tabt(B_HEAD_DIM // 2), tabt(B_HEAD_DIM // 2), tabt(IDX_DIM // 2), tabt(IDX_DIM // 2)],
        out_specs=[outt(B_WIDTH), outt(B_WIDTH), outt(B_WIDTH), outt(IDX_HEADS * IDX_DIM), outt(wrows)],
        out_shape=[jax.ShapeDtypeStruct((b, nq, B_WIDTH, QB), BF16),
                   jax.ShapeDtypeStruct((b, nq, B_WIDTH, QB), BF16),
                   jax.ShapeDtypeStruct((b, nq, B_WIDTH, QB), BF16),
                   jax.ShapeDtypeStruct((b, nq, IDX_HEADS * IDX_DIM, QB), BF16),
                   jax.ShapeDtypeStruct((b, nq, wrows, QB), F32)],
        compiler_params=pltpu.CompilerParams(dimension_semantics=("arbitrary", "arbitrary"),
                                             vmem_limit_bytes=VMEM_LIMIT_BYTES),
        name="proj_tr",
    )(x, ng, w_tr, w_trw, qng, cos_t, sin_t, cosi_t, sini_t)


def _slab_reduce(x, op):
    parts = [x[r:r + SUBLANES, :] for r in range(0, x.shape[0], SUBLANES)]
    while len(parts) > 1:
        parts = [op(parts[i], parts[i + 1]) for i in range(0, len(parts), 2)]
    return parts[0]


def _attn_body(q_ref, qi_ref, wi_ref, zb_ref, k_ref, ki_ref, v_ref, y_ref,
               s_scr, r_scr, att_scr, acc_scr, *, topk):
    n = pl.program_id(1)
    ntiles = n + 1
    idx_scale = (IDX_DIM ** -0.5) * (IDX_HEADS ** -0.5)
    rows_per_tile = TK // SUBLANES

    def score_tile(t, lo8, hi8, diagonal):
        ki_t = ki_ref[0, pl.ds(pl.multiple_of(t * TK, TK), TK), :]
        acc = jnp.zeros((TK, QB), F32)
        for hh in range(IDX_HEADS):
            logit = jnp.dot(ki_t, qi_ref[0, 0, hh * IDX_DIM:(hh + 1) * IDX_DIM, :],
                            preferred_element_type=F32)
            acc = acc + wi_ref[0, 0, hh:hh + 1, :] * jnp.maximum(logit, 0.0)
        sc = acc * idx_scale
        sc_for_min = sc
        if diagonal:
            kc = lax.broadcasted_iota(jnp.int32, (TK, QB), 0) // CHUNK
            qc = lax.broadcasted_iota(jnp.int32, (TK, QB), 1) // CHUNK
            admissible = kc <= qc
            sc_for_min = jnp.where(admissible, sc, jnp.inf)
            sc = jnp.where(admissible, sc, -jnp.inf)
        s_scr[t] = sc
        return (jnp.minimum(lo8, _slab_reduce(sc_for_min, jnp.minimum)),
                jnp.maximum(hi8, _slab_reduce(sc, jnp.maximum)))

    lo8, hi8 = lax.fori_loop(
        0, n, lambda t, c: score_tile(t, c[0], c[1], False),
        (jnp.full((SUBLANES, QB), jnp.inf, F32), jnp.full((SUBLANES, QB), -jnp.inf, F32)))
    lo8, hi8 = score_tile(n, lo8, hi8, True)
    s_lo = jnp.min(lo8, axis=0, keepdims=True)
    s_hi = jnp.max(hi8, axis=0, keepdims=True)

    span = s_hi - s_lo
    rank_scale = jnp.where(span > 0.0, RANK_MAX / jnp.where(span > 0.0, span, 1.0), 0.0)

    def rank_tile(t, carry):
        r_scr[t] = jnp.minimum(jnp.floor((s_scr[t] - s_lo) * rank_scale), RANK_MAX)
        return carry

    lax.fori_loop(0, ntiles, rank_tile, 0)

    @pl.when((ntiles & 1) == 1)
    def _():
        r_scr[ntiles] = jnp.full((TK, QB), -jnp.inf, F32)

    npairs = lax.shift_right_logical(ntiles + 1, 1)

    def count_ge(cand):
        cand_b = jnp.broadcast_to(cand, (SUBLANES, QB))

        def pair_body(p, accs):
            accs = list(accs)
            for half in range(2):
                for r in range(rows_per_tile):
                    sl = r_scr[2 * p + half, r * SUBLANES:(r + 1) * SUBLANES, :]
                    accs[r % COUNT_CHAINS] = accs[r % COUNT_CHAINS] + jnp.where(sl >= cand_b, 1.0, 0.0)
            return tuple(accs)

        accs = lax.fori_loop(0, npairs, pair_body,
                             tuple(jnp.zeros((SUBLANES, QB), F32) for _ in range(COUNT_CHAINS)))
        total = accs[0]
        for a in accs[1:]:
            total = total + a
        return jnp.sum(total, axis=0, keepdims=True)

    def bit_body(i, carry):
        prefix, cnt_at, step = carry
        trial = prefix + step
        cnt = count_ge(trial)
        ok = cnt >= float(topk)
        return jnp.where(ok, trial, prefix), jnp.where(ok, cnt, cnt_at), step * 0.5

    q_chunk = (n * QB + lax.broadcasted_iota(jnp.int32, (1, QB), 1)) // CHUNK
    n_admissible = ((q_chunk + 1) * CHUNK).astype(F32)
    thr, cnt_at, _ = lax.fori_loop(
        0, RANK_BITS, bit_body,
        (jnp.zeros((1, QB), F32), n_admissible, jnp.full((1, QB), (RANK_MAX + 1.0) / 2.0, F32)))

    excess0 = jnp.maximum(cnt_at - float(topk), 0.0)

    @pl.when(jnp.max(excess0) > 0.0)
    def _():
        thr_b = jnp.broadcast_to(thr, (SUBLANES, QB))
        sub_iota = lax.broadcasted_iota(jnp.int32, (SUBLANES, QB), 0).astype(F32)

        def slab(r):
            return slice(r * SUBLANES, (r + 1) * SUBLANES)

        def key_index(t, r):
            return sub_iota + ((t * TK).astype(F32) + float(r * SUBLANES))

        def drop_one(excess):
            active_b = jnp.broadcast_to(excess > 0.0, (SUBLANES, QB))

            def min_tile(t, m):
                for r in range(rows_per_tile):
                    selected = r_scr[t, slab(r), :] >= thr_b
                    m = jnp.minimum(m, jnp.where(selected, s_scr[t, slab(r), :], jnp.inf))
                return m

            m8 = lax.fori_loop(0, ntiles, min_tile, jnp.full((SUBLANES, QB), jnp.inf, F32))
            m_b = jnp.broadcast_to(jnp.min(m8, axis=0, keepdims=True), (SUBLANES, QB))

            def idx_tile(t, j):
                for r in range(rows_per_tile):
                    hit = (r_scr[t, slab(r), :] >= thr_b) & (s_scr[t, slab(r), :] == m_b)
                    j = jnp.maximum(j, jnp.where(hit, key_index(t, r), -1.0))
                return j

            j8 = lax.fori_loop(0, ntiles, idx_tile, jnp.full((SUBLANES, QB), -1.0, F32))
            j_b = jnp.broadcast_to(jnp.max(j8, axis=0, keepdims=True), (SUBLANES, QB))

            def drop_tile(t, carry):
                for r in range(rows_per_tile):
                    hit = (key_index(t, r) == j_b) & active_b
                    r_scr[t, slab(r), :] = jnp.where(hit, -jnp.inf, r_scr[t, slab(r), :])
                return carry

            lax.fori_loop(0, ntiles, drop_tile, 0)
            return excess - jnp.where(excess > 0.0, 1.0, 0.0)

        lax.while_loop(lambda e: jnp.max(e) > 0.0, drop_one, excess0)

    slab_reduce = _slab_reduce

    def logit_tile(t, m8):
        sel = r_scr[t] >= thr
        krows = pl.ds(pl.multiple_of(t * TK, TK), TK)
        new_m8 = []
        for hh in range(B_HEADS):
            hs = slice(hh * B_HEAD_DIM, (hh + 1) * B_HEAD_DIM)
            att = jnp.dot(k_ref[0, krows, hs], q_ref[0, 0, hs, :], preferred_element_type=F32)
            att = jnp.where(sel, att, NEG_BIG)
            att_scr[t * B_HEADS + hh] = att
            new_m8.append(jnp.maximum(m8[hh], slab_reduce(att, jnp.maximum)))
        return tuple(new_m8)

    m8 = lax.fori_loop(0, ntiles, logit_tile,
                       tuple(jnp.full((SUBLANES, QB), NEG_BIG, F32) for _ in range(B_HEADS)))
    m_row = [jnp.max(m, axis=0, keepdims=True) for m in m8]

    acc_scr[...] = jnp.zeros_like(acc_scr)

    def value_tile(t, l8):
        new_l8 = []
        for hh in range(B_HEADS):
            hs = slice(hh * B_HEAD_DIM, (hh + 1) * B_HEAD_DIM)
            p = jnp.exp(att_scr[t * B_HEADS + hh] - m_row[hh])
            new_l8.append(l8[hh] + slab_reduce(p, jnp.add))
            acc_scr[hs, :] += jnp.dot(v_ref[0, t, hs, :], p.astype(BF16), preferred_element_type=F32)
        return tuple(new_l8)

    l8 = lax.fori_loop(0, ntiles, value_tile,
                       tuple(jnp.zeros((SUBLANES, QB), F32) for _ in range(B_HEADS)))

    parts = []
    for hh in range(B_HEADS):
        hs = slice(hh * B_HEAD_DIM, (hh + 1) * B_HEAD_DIM)
        l_row = jnp.sum(l8[hh], axis=0, keepdims=True)
        parts.append(acc_scr[hs, :] / l_row * zb_ref[0, 0, hs, :].astype(F32))
    y_ref[0] = jnp.concatenate(parts, axis=0).T.astype(BF16)


def _attn(qT, qiT, wiT, zbT, k, ki, vT, topk):
    b, nq = qT.shape[0], qT.shape[1]
    s = k.shape[1]
    blk = lambda rows: pl.BlockSpec((1, 1, rows, QB), lambda bi, i: (bi, i, 0, 0))
    return pl.pallas_call(
        functools.partial(_attn_body, topk=topk),
        grid=(b, nq),
        in_specs=[blk(B_WIDTH), blk(IDX_HEADS * IDX_DIM), blk(wiT.shape[2]), blk(B_WIDTH),
                  pl.BlockSpec((1, s, B_WIDTH), lambda bi, i: (bi, 0, 0)),
                  pl.BlockSpec((1, s, IDX_DIM), lambda bi, i: (bi, 0, 0)),
                  pl.BlockSpec((1, nq, B_WIDTH, QB), lambda bi, i: (bi, 0, 0, 0))],
        out_specs=pl.BlockSpec((1, QB, B_WIDTH), lambda bi, i: (bi, i, 0)),
        out_shape=jax.ShapeDtypeStruct((b, s, B_WIDTH), BF16),
        scratch_shapes=[pltpu.VMEM((s // TK, TK, QB), F32),
                        pltpu.VMEM((s // TK, TK, QB), F32),
                        pltpu.VMEM((s // TK * B_HEADS, TK, QB), F32),
                        pltpu.VMEM((B_WIDTH, QB), F32)],
        compiler_params=pltpu.CompilerParams(dimension_semantics=("arbitrary", "arbitrary"),
                                             vmem_limit_bytes=VMEM_LIMIT_BYTES),
        name="attn",
    )(qT, qiT, wiT, zbT, k, ki, vT)


def _out_body(x_ref, yb_ref, ma_ref, sgb_ref, wob_ref, wout_ref, o_ref):
    o_b = jnp.dot(yb_ref[...], wob_ref[...], preferred_element_type=F32)
    merged = ma_ref[...].astype(F32) + sgb_ref[...].astype(F32) * o_b
    o_ref[...] = x_ref[...] + jnp.dot(merged.astype(BF16), wout_ref[...], preferred_element_type=F32)


def _out(x2d, yb, ma, sgb, wob, wout):
    m, d = x2d.shape
    tm = TM_OUT
    row = lambda width: pl.BlockSpec((tm, width), lambda i: (i, 0))
    full = lambda shape: pl.BlockSpec(shape, lambda i: (0,) * len(shape))
    return pl.pallas_call(
        _out_body,
        grid=(m // tm,),
        in_specs=[row(d), row(B_WIDTH), row(d), row(d), full(wob.shape), full(wout.shape)],
        out_specs=row(d),
        out_shape=jax.ShapeDtypeStruct((m, d), F32),
        compiler_params=pltpu.CompilerParams(dimension_semantics=("arbitrary",),
                                             vmem_limit_bytes=VMEM_LIMIT_BYTES),
        name="out_proj",
    )(x2d, yb, ma, sgb, wob, wout)


def _rope_tables(s, dim):
    pos = jnp.arange(s, dtype=F32)
    inv = ROPE_THETA ** (-jnp.arange(0, dim, 2, dtype=F32) / dim)
    ang = pos[:, None] * inv[None, :]
    return jnp.cos(ang), jnp.sin(ang)


def kernel(x, norm_g, w_in, gate_b, a_ln_g, a_ln_b, a_ws, a_bs, q_norm_g, k_norm_g, w_oa, w_ob, w_out):
    b, s, d = x.shape
    depth = w_in.shape[0]
    topk = min(TOPK_MAX, s // 4)
    assert s % TM_STD == 0 and s % QB == 0 and (b * s) % TM_OUT == 0 and QB == TK
    assert w_in.shape[2] == _O_END and topk <= TK

    cos_q, sin_q = _rope_tables(s, B_HEAD_DIM)
    cos_i, sin_i = _rope_tables(s, IDX_DIM)
    cos2 = jnp.concatenate([cos_q, cos_q], axis=1)
    sin2 = jnp.concatenate([-sin_q, sin_q], axis=1)
    cosi2 = jnp.concatenate([cos_i, cos_i, cos_i, cos_i], axis=1)
    sini2 = jnp.concatenate([-sin_i, sin_i, -sin_i, sin_i], axis=1)
    cos_t, sin_t, cosi_t, sini_t = cos_q.T, sin_q.T, cos_i.T, sin_i.T

    ik_cols = w_in[:, :, _O_IK:_O_IW]
    w_std = jnp.concatenate([w_in[:, :, _O_AU:_O_BQ], w_in[:, :, _O_BK:_O_BV], ik_cols, ik_cols,
                             w_in[:, :, _O_GA:_O_END]], axis=2).astype(BF16)
    w_tr = jnp.swapaxes(jnp.concatenate([w_in[:, :, _O_BQ:_O_BK], w_in[:, :, _O_BV:_O_IK]], axis=2),
                        1, 2).astype(BF16)
    iw_t = jnp.swapaxes(w_in[:, :, _O_IW:_O_GA], 1, 2)
    w_trw = jnp.concatenate([iw_t, jnp.zeros_like(iw_t)], axis=1).astype(BF16)
    woa, wob, wout = w_oa.astype(BF16), w_ob.astype(BF16), w_out.astype(BF16)
    bias_full = jnp.repeat(jnp.swapaxes(a_bs, 1, 2), A_WIDTH // A_GROUPS, axis=2)
    kng = jnp.reshape(k_norm_g, (depth, 1, B_HEAD_DIM))
    qng = jnp.reshape(q_norm_g, (depth, B_HEAD_DIM, 1))

    x2d = jnp.reshape(x, (b * s, d))
    for l in range(depth):
        ng = norm_g[l][None, :]
        ma, sgb, k, ki = _proj_std(x2d, ng, w_std[l], gate_b[l], a_ln_g[l][None, :], a_ln_b[l][None, :],
                                   a_ws[l], bias_full[l], kng[l], woa[l], cos2, sin2, cosi2, sini2, s)
        qT, vT, zbT, qiT, wiT = _proj_tr(jnp.reshape(x2d, (b, s, d)), ng, w_tr[l], w_trw[l], qng[l],
                                         cos_t, sin_t, cosi_t, sini_t)
        yb = _attn(qT, qiT, wiT, zbT, jnp.reshape(k, (b, s, B_WIDTH)), jnp.reshape(ki, (b, s, IDX_DIM)),
                   vT, topk)
        x2d = _out(x2d, jnp.reshape(yb, (b * s, B_WIDTH)), ma, sgb, wob[l], wout[l])
    return jnp.reshape(x2d, (b, s, d))
```

```python
import functools

import jax
import jax.numpy as jnp
from jax import lax
from jax.experimental import pallas as pl
from jax.experimental.pallas import tpu as pltpu

F32 = jnp.float32
BF16 = jnp.bfloat16

CHUNK = 64
EPS = 1e-6
ROPE_THETA = 10000.0

A_WIDTH = 512
A_GROUPS = 4
A_BLOCK = 128
B_HEADS = 4
B_HEAD_DIM = 128
B_WIDTH = B_HEADS * B_HEAD_DIM
IDX_HEADS = 8
IDX_DIM = 64
TOPK_MAX = 256

_O_AU, _O_AV, _O_AZ = 0, 512, 1024
_O_BQ, _O_BK, _O_BV, _O_BZ = 1536, 2048, 2560, 3072
_O_IQ, _O_IK, _O_IW = 3584, 4096, 4160
_O_GA, _O_GB, _O_END = 4168, 5192, 6216

LANES = 128
SUBLANES = 8
VMEM_LIMIT_BYTES = 52 * 1024 * 1024

QB = 256
TK = 256
TM_STD = 512
TM_OUT = 512
NEG_BIG = -1e30
RANK_BITS = 23
RANK_MAX = float(2 ** RANK_BITS - 1)
COUNT_CHAINS = 4


def _rms_rows(x, g):
    ms = jnp.mean(x * x, axis=-1, keepdims=True)
    return x * lax.rsqrt(ms + EPS) * g


def _proj_std_body(x_ref, ng_ref, w_ref, gb_ref, lng_ref, lnb_ref, ws_ref, bias_ref, kng_ref,
                   woa_ref, cos_ref, sin_ref, cosi_ref, sini_ref,
                   ma_ref, sgb_ref, k_ref, ki_ref):
    tm = x_ref.shape[0]
    h = _rms_rows(x_ref[...], ng_ref[...]).astype(BF16)

    def proj(lo, hi):
        return jnp.dot(h, w_ref[:, lo:hi], preferred_element_type=F32)

    gv = jax.nn.gelu(proj(512, 1024))
    mu = jnp.mean(gv, axis=-1, keepdims=True)
    xc = gv - mu
    var = jnp.mean(xc * xc, axis=-1, keepdims=True)
    vn = (xc * lax.rsqrt(var + EPS) * lng_ref[...] + lnb_ref[...]).astype(BF16)

    ci = lax.broadcasted_iota(jnp.int32, (A_BLOCK, A_BLOCK), 0) // CHUNK
    cj = lax.broadcasted_iota(jnp.int32, (A_BLOCK, A_BLOCK), 1) // CHUNK
    causal = cj <= ci
    wm = [jnp.where(causal, ws_ref[g], 0.0).astype(BF16) for g in range(A_GROUPS)]
    row_blocks = []
    for r in range(tm // A_BLOCK):
        cols = []
        for g in range(A_GROUPS):
            vb = vn[r * A_BLOCK:(r + 1) * A_BLOCK, g * LANES:(g + 1) * LANES]
            cols.append(jnp.dot(wm[g], vb, preferred_element_type=F32))
        row_blocks.append(jnp.concatenate(cols, axis=1) + bias_ref[...])
    mixed = jnp.concatenate(row_blocks, axis=0)

    y_a = jax.nn.gelu(proj(0, 512)) * mixed * jax.nn.silu(proj(1024, 1536))
    o_a = jnp.dot(y_a.astype(BF16), woa_ref[...], preferred_element_type=F32)
    ma_ref[...] = (jax.nn.sigmoid(proj(2176, 3200) + gb_ref[0:1, :]) * o_a).astype(BF16)
    sgb_ref[...] = jax.nn.sigmoid(proj(3200, 4224) + gb_ref[1:2, :]).astype(BF16)

    b_k = proj(1536, 2048)
    for hh in range(B_HEADS):
        kh = _rms_rows(b_k[:, hh * LANES:(hh + 1) * LANES], kng_ref[...])
        kh = kh * cos_ref[...] + pltpu.roll(kh, B_HEAD_DIM // 2, 1) * sin_ref[...]
        k_ref[:, hh * LANES:(hh + 1) * LANES] = kh.astype(BF16)

    ik = proj(2048, 2176)
    ik = ik * cosi_ref[...] + pltpu.roll(ik, IDX_DIM // 2, 1) * sini_ref[...]
    ki_ref[...] = ik[:, :IDX_DIM].astype(BF16)


def _proj_std(x2d, ng, w_std, gb, lng, lnb, ws, bias_full, kng, woa, cos2, sin2, cosi2, sini2, seq):
    m, d = x2d.shape
    tm = TM_STD
    nt = seq // tm
    full = lambda shape: pl.BlockSpec(shape, lambda i: (0,) * len(shape))
    tab = lambda: pl.BlockSpec((tm, LANES), lambda i: (i % nt, 0))
    row = lambda width: pl.BlockSpec((tm, width), lambda i: (i, 0))
    return pl.pallas_call(
        _proj_std_body,
        grid=(m // tm,),
        in_specs=[row(d), full(ng.shape), full(w_std.shape), full(gb.shape), full(lng.shape),
                  full(lnb.shape), full(ws.shape), full(bias_full.shape), full(kng.shape),
                  full(woa.shape), tab(), tab(), tab(), tab()],
        out_specs=[row(d), row(d), row(B_WIDTH), row(IDX_DIM)],
        out_shape=[jax.ShapeDtypeStruct((m, d), BF16), jax.ShapeDtypeStruct((m, d), BF16),
                   jax.ShapeDtypeStruct((m, B_WIDTH), BF16),
                   jax.ShapeDtypeStruct((m, IDX_DIM), BF16)],
        compiler_params=pltpu.CompilerParams(dimension_semantics=("arbitrary",),
                                             vmem_limit_bytes=VMEM_LIMIT_BYTES),
        name="proj_std",
    )(x2d, ng, w_std, gb, lng, lnb, ws, bias_full, kng, woa, cos2, sin2, cosi2, sini2)


def _proj_tr_body(x_ref, ng_ref, w_ref, ww_ref, qng_ref, cos_ref, sin_ref, cosi_ref, sini_ref,
                  q_ref, v_ref, zb_ref, qi_ref, wi_ref):
    h = _rms_rows(x_ref[0], ng_ref[...]).astype(BF16)

    def proj_t(w):
        return lax.dot_general(w, h, (((1,), (1,)), ((), ())), preferred_element_type=F32)

    att_scale = B_HEAD_DIM ** -0.5
    half = B_HEAD_DIM // 2
    qt = proj_t(w_ref[0:512, :])
    for hh in range(B_HEADS):
        qh = qt[hh * B_HEAD_DIM:(hh + 1) * B_HEAD_DIM, :]
        ms = jnp.mean(qh * qh, axis=0, keepdims=True)
        qh = qh * lax.rsqrt(ms + EPS) * qng_ref[...]
        x1, x2 = qh[:half, :], qh[half:, :]
        c, s = cos_ref[...], sin_ref[...]
        base = hh * B_HEAD_DIM
        q_ref[0, 0, base:base + half, :] = ((x1 * c - x2 * s) * att_scale).astype(BF16)
        q_ref[0, 0, base + half:base + B_HEAD_DIM, :] = ((x1 * s + x2 * c) * att_scale).astype(BF16)

    v_ref[0, 0] = proj_t(w_ref[512:1024, :]).astype(BF16)
    zb_ref[0, 0] = jax.nn.silu(proj_t(w_ref[1024:1536, :])).astype(BF16)

    qit = proj_t(w_ref[1536:2048, :])
    ih = IDX_DIM // 2
    for hh in range(IDX_HEADS):
        xh = qit[hh * IDX_DIM:(hh + 1) * IDX_DIM, :]
        x1, x2 = xh[:ih, :], xh[ih:, :]
        c, s = cosi_ref[...], sini_ref[...]
        base = hh * IDX_DIM
        qi_ref[0, 0, base:base + ih, :] = (x1 * c - x2 * s).astype(BF16)
        qi_ref[0, 0, base + ih:base + IDX_DIM, :] = (x1 * s + x2 * c).astype(BF16)

    wi_ref[0, 0] = proj_t(ww_ref[...])


def _proj_tr(x, ng, w_tr, w_trw, qng, cos_t, sin_t, cosi_t, sini_t):
    b, s, d = x.shape
    nq = s // QB
    full = lambda shape: pl.BlockSpec(shape, lambda bi, i: (0,) * len(shape))
    tabt = lambda rows: pl.BlockSpec((rows, QB), lambda bi, i: (0, i))
    outt = lambda rows: pl.BlockSpec((1, 1, rows, QB), lambda bi, i: (bi, i, 0, 0))
    wrows = w_trw.shape[0]
    return pl.pallas_call(
        _proj_tr_body,
        grid=(b, nq),
        in_specs=[pl.BlockSpec((1, QB, d), lambda bi, i: (bi, i, 0)), full(ng.shape),
                  full(w_tr.shape), full(w_trw.shape), full(qng.shape),
                  tabt(B_HEAD_DIM // 2), tabt(B_HEAD_DIM // 2), tabt(IDX_DIM // 2), tabt(IDX_DIM // 2)],
        out_specs=[outt(B_WIDTH), outt(B_WIDTH), outt(B_WIDTH), outt(IDX_HEADS * IDX_DIM), outt(wrows)],
        out_shape=[jax.ShapeDtypeStruct((b, nq, B_WIDTH, QB), BF16),
                   jax.ShapeDtypeStruct((b, nq, B_WIDTH, QB), BF16),
                   jax.ShapeDtypeStruct((b, nq, B_WIDTH, QB), BF16),
                   jax.ShapeDtypeStruct((b, nq, IDX_HEADS * IDX_DIM, QB), BF16),
                   jax.ShapeDtypeStruct((b, nq, wrows, QB), F32)],
        compiler_params=pltpu.CompilerParams(dimension_semantics=("arbitrary", "arbitrary"),
                                             vmem_limit_bytes=VMEM_LIMIT_BYTES),
        name="proj_tr",
    )(x, ng, w_tr, w_trw, qng, cos_t, sin_t, cosi_t, sini_t)


def _slab_reduce(x, op):
    parts = [x[r:r + SUBLANES, :] for r in range(0, x.shape[0], SUBLANES)]
    while len(parts) > 1:
        parts = [op(parts[i], parts[i + 1]) for i in range(0, len(parts), 2)]
    return parts[0]


def _attn_body(q_ref, qi_ref, wi_ref, zb_ref, k_ref, ki_ref, v_ref, y_ref,
               s_scr, r_scr, att_scr, acc_scr, *, topk):
    n = pl.program_id(1)
    ntiles = n + 1
    idx_scale = (IDX_DIM ** -0.5) * (IDX_HEADS ** -0.5)
    rows_per_tile = TK // SUBLANES

    def score_tile(t, lo8, hi8, diagonal):
        ki_t = ki_ref[0, pl.ds(pl.multiple_of(t * TK, TK), TK), :]
        acc = jnp.zeros((TK, QB), F32)
        for hh in range(IDX_HEADS):
            logit = jnp.dot(ki_t, qi_ref[0, 0, hh * IDX_DIM:(hh + 1) * IDX_DIM, :],
                            preferred_element_type=F32)
            acc = acc + wi_ref[0, 0, hh:hh + 1, :] * jnp.maximum(logit, 0.0)
        sc = acc * idx_scale
        sc_for_min = sc
        if diagonal:
            kc = lax.broadcasted_iota(jnp.int32, (TK, QB), 0) // CHUNK
            qc = lax.broadcasted_iota(jnp.int32, (TK, QB), 1) // CHUNK
            admissible = kc <= qc
            sc_for_min = jnp.where(admissible, sc, jnp.inf)
            sc = jnp.where(admissible, sc, -jnp.inf)
        s_scr[t] = sc
        return (jnp.minimum(lo8, _slab_reduce(sc_for_min, jnp.minimum)),
                jnp.maximum(hi8, _slab_reduce(sc, jnp.maximum)))

    def tiles_in_pairs(count, tile_fn, carry):
        def pair(p, c):
            return tile_fn(2 * p + 1, tile_fn(2 * p, c))
        carry = lax.fori_loop(0, lax.shift_right_logical(count, 1), pair, carry)
        return lax.cond((count & 1) == 1, lambda c: tile_fn(count - 1, c), lambda c: c, carry)

    lo8, hi8 = tiles_in_pairs(
        n, lambda t, c: score_tile(t, c[0], c[1], False),
        (jnp.full((SUBLANES, QB), jnp.inf, F32), jnp.full((SUBLANES, QB), -jnp.inf, F32)))
    lo8, hi8 = score_tile(n, lo8, hi8, True)
    s_lo = jnp.min(lo8, axis=0, keepdims=True)
    s_hi = jnp.max(hi8, axis=0, keepdims=True)

    span = s_hi - s_lo
    rank_scale = jnp.where(span > 0.0, RANK_MAX / jnp.where(span > 0.0, span, 1.0), 0.0)

    def rank_tile(t, carry):
        r_scr[t] = jnp.minimum(jnp.floor((s_scr[t] - s_lo) * rank_scale), RANK_MAX)
        return carry

    lax.fori_loop(0, ntiles, rank_tile, 0)

    @pl.when((ntiles & 1) == 1)
    def _():
        r_scr[ntiles] = jnp.full((TK, QB), -jnp.inf, F32)

    npairs = lax.shift_right_logical(ntiles + 1, 1)

    def count_ge(cand):
        cand_b = jnp.broadcast_to(cand, (SUBLANES, QB))

        def pair_body(p, accs):
            accs = list(accs)
            for half in range(2):
                for r in range(rows_per_tile):
                    sl = r_scr[2 * p + half, r * SUBLANES:(r + 1) * SUBLANES, :]
                    accs[r % COUNT_CHAINS] = accs[r % COUNT_CHAINS] + jnp.where(sl >= cand_b, 1.0, 0.0)
            return tuple(accs)

        accs = lax.fori_loop(0, npairs, pair_body,
                             tuple(jnp.zeros((SUBLANES, QB), F32) for _ in range(COUNT_CHAINS)))
        total = accs[0]
        for a in accs[1:]:
            total = total + a
        return jnp.sum(total, axis=0, keepdims=True)

    def bit_body(i, carry):
        prefix, cnt_at, step = carry
        trial = prefix + step
        cnt = count_ge(trial)
        ok = cnt >= float(topk)
        return jnp.where(ok, trial, prefix), jnp.where(ok, cnt, cnt_at), step * 0.5

    q_chunk = (n * QB + lax.broadcasted_iota(jnp.int32, (1, QB), 1)) // CHUNK
    n_admissible = ((q_chunk + 1) * CHUNK).astype(F32)
    thr, cnt_at, _ = lax.fori_loop(
        0, RANK_BITS, bit_body,
        (jnp.zeros((1, QB), F32), n_admissible, jnp.full((1, QB), (RANK_MAX + 1.0) / 2.0, F32)))

    excess0 = jnp.maximum(cnt_at - float(topk), 0.0)

    @pl.when(jnp.max(excess0) > 0.0)
    def _():
        thr_b = jnp.broadcast_to(thr, (SUBLANES, QB))
        sub_iota = lax.broadcasted_iota(jnp.int32, (SUBLANES, QB), 0).astype(F32)

        def slab(r):
            return slice(r * SUBLANES, (r + 1) * SUBLANES)

        def key_index(t, r):
            return sub_iota + ((t * TK).astype(F32) + float(r * SUBLANES))

        def drop_one(excess):
            active_b = jnp.broadcast_to(excess > 0.0, (SUBLANES, QB))

            def min_tile(t, ms):
                ms = list(ms)
                for r in range(rows_per_tile):
                    selected = r_scr[t, slab(r), :] >= thr_b
                    c = r % COUNT_CHAINS
                    ms[c] = jnp.minimum(ms[c], jnp.where(selected, s_scr[t, slab(r), :], jnp.inf))
                return tuple(ms)

            ms = lax.fori_loop(0, ntiles, min_tile,
                               tuple(jnp.full((SUBLANES, QB), jnp.inf, F32) for _ in range(COUNT_CHAINS)))
            m8 = functools.reduce(jnp.minimum, ms)
            m_b = jnp.broadcast_to(jnp.min(m8, axis=0, keepdims=True), (SUBLANES, QB))

            def idx_tile(t, js):
                js = list(js)
                for r in range(rows_per_tile):
                    hit = (r_scr[t, slab(r), :] >= thr_b) & (s_scr[t, slab(r), :] == m_b)
                    c = r % COUNT_CHAINS
                    js[c] = jnp.maximum(js[c], jnp.where(hit, key_index(t, r), -1.0))
                return tuple(js)

            js = lax.fori_loop(0, ntiles, idx_tile,
                               tuple(jnp.full((SUBLANES, QB), -1.0, F32) for _ in range(COUNT_CHAINS)))
            j8 = functools.reduce(jnp.maximum, js)
            j_b = jnp.broadcast_to(jnp.max(j8, axis=0, keepdims=True), (SUBLANES, QB))

            def drop_tile(t, carry):
                for r in range(rows_per_tile):
                    hit = (key_index(t, r) == j_b) & active_b
                    r_scr[t, slab(r), :] = jnp.where(hit, -jnp.inf, r_scr[t, slab(r), :])
                return carry

            lax.fori_loop(0, ntiles, drop_tile, 0)
            return excess - jnp.where(excess > 0.0, 1.0, 0.0)

        lax.while_loop(lambda e: jnp.max(e) > 0.0, drop_one, excess0)

    slab_reduce = _slab_reduce

    def logit_tile(t, m8):
        sel = r_scr[t] >= thr
        krows = pl.ds(pl.multiple_of(t * TK, TK), TK)
        new_m8 = []
        for hh in range(B_HEADS):
            hs = slice(hh * B_HEAD_DIM, (hh + 1) * B_HEAD_DIM)
            att = jnp.dot(k_ref[0, krows, hs], q_ref[0, 0, hs, :], preferred_element_type=F32)
            att = jnp.where(sel, att, NEG_BIG)
            att_scr[t * B_HEADS + hh] = att
            new_m8.append(jnp.maximum(m8[hh], slab_reduce(att, jnp.maximum)))
        return tuple(new_m8)

    m8 = lax.fori_loop(0, ntiles, logit_tile,
                       tuple(jnp.full((SUBLANES, QB), NEG_BIG, F32) for _ in range(B_HEADS)))
    m_row = [jnp.max(m, axis=0, keepdims=True) for m in m8]

    acc_scr[...] = jnp.zeros_like(acc_scr)

    def value_tile(t, l8):
        new_l8 = []
        for hh in range(B_HEADS):
            hs = slice(hh * B_HEAD_DIM, (hh + 1) * B_HEAD_DIM)
            p = jnp.exp(att_scr[t * B_HEADS + hh] - m_row[hh])
            new_l8.append(l8[hh] + slab_reduce(p, jnp.add))
            acc_scr[hs, :] += jnp.dot(v_ref[0, t, hs, :], p.astype(BF16), preferred_element_type=F32)
        return tuple(new_l8)

    l8 = tiles_in_pairs(ntiles, value_tile,
                        tuple(jnp.zeros((SUBLANES, QB), F32) for _ in range(B_HEADS)))

    parts = []
    for hh in range(B_HEADS):
        hs = slice(hh * B_HEAD_DIM, (hh + 1) * B_HEAD_DIM)
        l_row = jnp.sum(l8[hh], axis=0, keepdims=True)
        parts.append(acc_scr[hs, :] / l_row * zb_ref[0, 0, hs, :].astype(F32))
    y_ref[0] = jnp.concatenate(parts, axis=0).T.astype(BF16)


def _attn(qT, qiT, wiT, zbT, k, ki, vT, topk):
    b, nq = qT.shape[0], qT.shape[1]
    s = k.shape[1]
    blk = lambda rows: pl.BlockSpec((1, 1, rows, QB), lambda bi, i: (bi, i, 0, 0))
    return pl.pallas_call(
        functools.partial(_attn_body, topk=topk),
        grid=(b, nq),
        in_specs=[blk(B_WIDTH), blk(IDX_HEADS * IDX_DIM), blk(wiT.shape[2]), blk(B_WIDTH),
                  pl.BlockSpec((1, s, B_WIDTH), lambda bi, i: (bi, 0, 0)),
                  pl.BlockSpec((1, s, IDX_DIM), lambda bi, i: (bi, 0, 0)),
                  pl.BlockSpec((1, nq, B_WIDTH, QB), lambda bi, i: (bi, 0, 0, 0))],
        out_specs=pl.BlockSpec((1, QB, B_WIDTH), lambda bi, i: (bi, i, 0)),
        out_shape=jax.ShapeDtypeStruct((b, s, B_WIDTH), BF16),
        scratch_shapes=[pltpu.VMEM((s // TK, TK, QB), F32),
                        pltpu.VMEM((s // TK, TK, QB), F32),
                        pltpu.VMEM((s // TK * B_HEADS, TK, QB), F32),
                        pltpu.VMEM((B_WIDTH, QB), F32)],
        compiler_params=pltpu.CompilerParams(dimension_semantics=("arbitrary", "arbitrary"),
                                             vmem_limit_bytes=VMEM_LIMIT_BYTES),
        name="attn",
    )(qT, qiT, wiT, zbT, k, ki, vT)


def _out_body(x_ref, yb_ref, ma_ref, sgb_ref, wob_ref, wout_ref, o_ref):
    o_b = jnp.dot(yb_ref[...], wob_ref[...], preferred_element_type=F32)
    merged = ma_ref[...].astype(F32) + sgb_ref[...].astype(F32) * o_b
    o_ref[...] = x_ref[...] + jnp.dot(merged.astype(BF16), wout_ref[...], preferred_element_type=F32)


def _out(x2d, yb, ma, sgb, wob, wout):
    m, d = x2d.shape
    tm = TM_OUT
    row = lambda width: pl.BlockSpec((tm, width), lambda i: (i, 0))
    full = lambda shape: pl.BlockSpec(shape, lambda i: (0,) * len(shape))
    return pl.pallas_call(
        _out_body,
        grid=(m // tm,),
        in_specs=[row(d), row(B_WIDTH), row(d), row(d), full(wob.shape), full(wout.shape)],
        out_specs=row(d),
        out_shape=jax.ShapeDtypeStruct((m, d), F32),
        compiler_params=pltpu.CompilerParams(dimension_semantics=("arbitrary",),
                                             vmem_limit_bytes=VMEM_LIMIT_BYTES),
        name="out_proj",
    )(x2d, yb, ma, sgb, wob, wout)


def _rope_tables(s, dim):
    pos = jnp.arange(s, dtype=F32)
    inv = ROPE_THETA ** (-jnp.arange(0, dim, 2, dtype=F32) / dim)
    ang = pos[:, None] * inv[None, :]
    return jnp.cos(ang), jnp.sin(ang)


def kernel(x, norm_g, w_in, gate_b, a_ln_g, a_ln_b, a_ws, a_bs, q_norm_g, k_norm_g, w_oa, w_ob, w_out):
    b, s, d = x.shape
    depth = w_in.shape[0]
    topk = min(TOPK_MAX, s // 4)
    assert s % TM_STD == 0 and s % QB == 0 and (b * s) % TM_OUT == 0 and QB == TK
    assert w_in.shape[2] == _O_END and topk <= TK

    cos_q, sin_q = _rope_tables(s, B_HEAD_DIM)
    cos_i, sin_i = _rope_tables(s, IDX_DIM)
    cos2 = jnp.concatenate([cos_q, cos_q], axis=1)
    sin2 = jnp.concatenate([-sin_q, sin_q], axis=1)
    cosi2 = jnp.concatenate([cos_i, cos_i, cos_i, cos_i], axis=1)
    sini2 = jnp.concatenate([-sin_i, sin_i, -sin_i, sin_i], axis=1)
    cos_t, sin_t, cosi_t, sini_t = cos_q.T, sin_q.T, cos_i.T, sin_i.T

    ik_cols = w_in[:, :, _O_IK:_O_IW]
    w_std = jnp.concatenate([w_in[:, :, _O_AU:_O_BQ], w_in[:, :, _O_BK:_O_BV], ik_cols, ik_cols,
                             w_in[:, :, _O_GA:_O_END]], axis=2).astype(BF16)
    w_tr = jnp.swapaxes(jnp.concatenate([w_in[:, :, _O_BQ:_O_BK], w_in[:, :, _O_BV:_O_IK]], axis=2),
                        1, 2).astype(BF16)
    iw_t = jnp.swapaxes(w_in[:, :, _O_IW:_O_GA], 1, 2)
    w_trw = jnp.concatenate([iw_t, jnp.zeros_like(iw_t)], axis=1).astype(BF16)
    woa, wob, wout = w_oa.astype(BF16), w_ob.astype(BF16), w_out.astype(BF16)
    bias_full = jnp.repeat(jnp.swapaxes(a_bs, 1, 2), A_WIDTH // A_GROUPS, axis=2)
    kng = jnp.reshape(k_norm_g, (depth, 1, B_HEAD_DIM))
    qng = jnp.reshape(q_norm_g, (depth, B_HEAD_DIM, 1))

    x2d = jnp.reshape(x, (b * s, d))
    for l in range(depth):
        ng = norm_g[l][None, :]
        ma, sgb, k, ki = _proj_std(x2d, ng, w_std[l], gate_b[l], a_ln_g[l][None, :], a_ln_b[l][None, :],
                                   a_ws[l], bias_full[l], kng[l], woa[l], cos2, sin2, cosi2, sini2, s)
        qT, vT, zbT, qiT, wiT = _proj_tr(jnp.reshape(x2d, (b, s, d)), ng, w_tr[l], w_trw[l], qng[l],
                                         cos_t, sin_t, cosi_t, sini_t)
        yb = _attn(qT, qiT, wiT, zbT, jnp.reshape(k, (b, s, B_WIDTH)), jnp.reshape(ki, (b, s, IDX_DIM)),
                   vT, topk)
        x2d = _out(x2d, jnp.reshape(yb, (b * s, B_WIDTH)), ma, sgb, wob[l], wout[l])
    return jnp.reshape(x2d, (b, s, d))
```

```python
import functools

import jax
import jax.numpy as jnp
from jax import lax
from jax.experimental import pallas as pl
from jax.experimental.pallas import tpu as pltpu

F32 = jnp.float32
BF16 = jnp.bfloat16

CHUNK = 64
EPS = 1e-6
ROPE_THETA = 10000.0

A_WIDTH = 512
A_GROUPS = 4
A_BLOCK = 128
B_HEADS = 4
B_HEAD_DIM = 128
B_WIDTH = B_HEADS * B_HEAD_DIM
IDX_HEADS = 8
IDX_DIM = 64
TOPK_MAX = 256

_O_AU, _O_AV, _O_AZ = 0, 512, 1024
_O_BQ, _O_BK, _O_BV, _O_BZ = 1536, 2048, 2560, 3072
_O_IQ, _O_IK, _O_IW = 3584, 4096, 4160
_O_GA, _O_GB, _O_END = 4168, 5192, 6216

LANES = 128
SUBLANES = 8
VMEM_LIMIT_BYTES = 52 * 1024 * 1024

QB = 256
TK = 256
TM_STD = 512
TM_OUT = 512
NEG_BIG = -1e30
DIGIT_BITS = 8
DIGIT_BASE = 2 ** DIGIT_BITS
RANK_MAX = float(DIGIT_BASE ** 3 - 1)
COUNT_CHAINS = 4


def _rms_rows(x, g):
    ms = jnp.mean(x * x, axis=-1, keepdims=True)
    return x * lax.rsqrt(ms + EPS) * g


def _proj_std_body(x_ref, ng_ref, w_ref, gb_ref, lng_ref, lnb_ref, ws_ref, bias_ref, kng_ref,
                   woa_ref, cos_ref, sin_ref, cosi_ref, sini_ref,
                   ma_ref, sgb_ref, k_ref, ki_ref):
    tm = x_ref.shape[0]
    h = _rms_rows(x_ref[...], ng_ref[...]).astype(BF16)

    def proj(lo, hi):
        return jnp.dot(h, w_ref[:, lo:hi], preferred_element_type=F32)

    gv = jax.nn.gelu(proj(512, 1024))
    mu = jnp.mean(gv, axis=-1, keepdims=True)
    xc = gv - mu
    var = jnp.mean(xc * xc, axis=-1, keepdims=True)
    vn = (xc * lax.rsqrt(var + EPS) * lng_ref[...] + lnb_ref[...]).astype(BF16)

    ci = lax.broadcasted_iota(jnp.int32, (A_BLOCK, A_BLOCK), 0) // CHUNK
    cj = lax.broadcasted_iota(jnp.int32, (A_BLOCK, A_BLOCK), 1) // CHUNK
    causal = cj <= ci
    wm = [jnp.where(causal, ws_ref[g], 0.0).astype(BF16) for g in range(A_GROUPS)]
    row_blocks = []
    for r in range(tm // A_BLOCK):
        cols = []
        for g in range(A_GROUPS):
            vb = vn[r * A_BLOCK:(r + 1) * A_BLOCK, g * LANES:(g + 1) * LANES]
            cols.append(jnp.dot(wm[g], vb, preferred_element_type=F32))
        row_blocks.append(jnp.concatenate(cols, axis=1) + bias_ref[...])
    mixed = jnp.concatenate(row_blocks, axis=0)

    y_a = jax.nn.gelu(proj(0, 512)) * mixed * jax.nn.silu(proj(1024, 1536))
    o_a = jnp.dot(y_a.astype(BF16), woa_ref[...], preferred_element_type=F32)
    ma_ref[...] = (jax.nn.sigmoid(proj(2176, 3200) + gb_ref[0:1, :]) * o_a).astype(BF16)
    sgb_ref[...] = jax.nn.sigmoid(proj(3200, 4224) + gb_ref[1:2, :]).astype(BF16)

    b_k = proj(1536, 2048)
    for hh in range(B_HEADS):
        kh = _rms_rows(b_k[:, hh * LANES:(hh + 1) * LANES], kng_ref[...])
        kh = kh * cos_ref[...] + pltpu.roll(kh, B_HEAD_DIM // 2, 1) * sin_ref[...]
        k_ref[:, hh * LANES:(hh + 1) * LANES] = kh.astype(BF16)

    ik = proj(2048, 2176)
    ik = ik * cosi_ref[...] + pltpu.roll(ik, IDX_DIM // 2, 1) * sini_ref[...]
    ki_ref[...] = ik[:, :IDX_DIM].astype(BF16)


def _proj_std(x2d, ng, w_std, gb, lng, lnb, ws, bias_full, kng, woa, cos2, sin2, cosi2, sini2, seq):
    m, d = x2d.shape
    tm = TM_STD
    nt = seq // tm
    full = lambda shape: pl.BlockSpec(shape, lambda i: (0,) * len(shape))
    tab = lambda: pl.BlockSpec((tm, LANES), lambda i: (i % nt, 0))
    row = lambda width: pl.BlockSpec((tm, width), lambda i: (i, 0))
    return pl.pallas_call(
        _proj_std_body,
        grid=(m // tm,),
        in_specs=[row(d), full(ng.shape), full(w_std.shape), full(gb.shape), full(lng.shape),
                  full(lnb.shape), full(ws.shape), full(bias_full.shape), full(kng.shape),
                  full(woa.shape), tab(), tab(), tab(), tab()],
        out_specs=[row(d), row(d), row(B_WIDTH), row(IDX_DIM)],
        out_shape=[jax.ShapeDtypeStruct((m, d), BF16), jax.ShapeDtypeStruct((m, d), BF16),
                   jax.ShapeDtypeStruct((m, B_WIDTH), BF16),
                   jax.ShapeDtypeStruct((m, IDX_DIM), BF16)],
        compiler_params=pltpu.CompilerParams(dimension_semantics=("arbitrary",),
                                             vmem_limit_bytes=VMEM_LIMIT_BYTES),
        name="proj_std",
    )(x2d, ng, w_std, gb, lng, lnb, ws, bias_full, kng, woa, cos2, sin2, cosi2, sini2)


def _proj_tr_body(x_ref, ng_ref, w_ref, ww_ref, qng_ref, cos_ref, sin_ref, cosi_ref, sini_ref,
                  q_ref, v_ref, zb_ref, qi_ref, wi_ref):
    h = _rms_rows(x_ref[0], ng_ref[...]).astype(BF16)

    def proj_t(w):
        return lax.dot_general(w, h, (((1,), (1,)), ((), ())), preferred_element_type=F32)

    att_scale = B_HEAD_DIM ** -0.5
    half = B_HEAD_DIM // 2
    qt = proj_t(w_ref[0:512, :])
    for hh in range(B_HEADS):
        qh = qt[hh * B_HEAD_DIM:(hh + 1) * B_HEAD_DIM, :]
        ms = jnp.mean(qh * qh, axis=0, keepdims=True)
        qh = qh * lax.rsqrt(ms + EPS) * qng_ref[...]
        x1, x2 = qh[:half, :], qh[half:, :]
        c, s = cos_ref[...], sin_ref[...]
        base = hh * B_HEAD_DIM
        q_ref[0, 0, base:base + half, :] = ((x1 * c - x2 * s) * att_scale).astype(BF16)
        q_ref[0, 0, base + half:base + B_HEAD_DIM, :] = ((x1 * s + x2 * c) * att_scale).astype(BF16)

    v_ref[0, 0] = proj_t(w_ref[512:1024, :]).astype(BF16)
    zb_ref[0, 0] = jax.nn.silu(proj_t(w_ref[1024:1536, :])).astype(BF16)

    qit = proj_t(w_ref[1536:2048, :])
    ih = IDX_DIM // 2
    for hh in range(IDX_HEADS):
        xh = qit[hh * IDX_DIM:(hh + 1) * IDX_DIM, :]
        x1, x2 = xh[:ih, :], xh[ih:, :]
        c, s = cosi_ref[...], sini_ref[...]
        base = hh * IDX_DIM
        qi_ref[0, 0, base:base + ih, :] = (x1 * c - x2 * s).astype(BF16)
        qi_ref[0, 0, base + ih:base + IDX_DIM, :] = (x1 * s + x2 * c).astype(BF16)

    wi_ref[0, 0] = proj_t(ww_ref[...])


def _proj_tr(x, ng, w_tr, w_trw, qng, cos_t, sin_t, cosi_t, sini_t):
    b, s, d = x.shape
    nq = s // QB
    full = lambda shape: pl.BlockSpec(shape, lambda bi, i: (0,) * len(shape))
    tabt = lambda rows: pl.BlockSpec((rows, QB), lambda bi, i: (0, i))
    outt = lambda rows: pl.BlockSpec((1, 1, rows, QB), lambda bi, i: (bi, i, 0, 0))
    wrows = w_trw.shape[0]
    return pl.pallas_call(
        _proj_tr_body,
        grid=(b, nq),
        in_specs=[pl.BlockSpec((1, QB, d), lambda bi, i: (bi, i, 0)), full(ng.shape),
                  full(w_tr.shape), full(w_trw.shape), full(qng.shape),
                  tabt(B_HEAD_DIM // 2), tabt(B_HEAD_DIM // 2), tabt(IDX_DIM // 2), tabt(IDX_DIM // 2)],
        out_specs=[outt(B_WIDTH), outt(B_WIDTH), outt(B_WIDTH), outt(IDX_HEADS * IDX_DIM), outt(wrows)],
        out_shape=[jax.ShapeDtypeStruct((b, nq, B_WIDTH, QB), BF16),
                   jax.ShapeDtypeStruct((b, nq, B_WIDTH, QB), BF16),
                   jax.ShapeDtypeStruct((b, nq, B_WIDTH, QB), BF16),
                   jax.ShapeDtypeStruct((b, nq, IDX_HEADS * IDX_DIM, QB), BF16),
                   jax.ShapeDtypeStruct((b, nq, wrows, QB), F32)],
        compiler_params=pltpu.CompilerParams(dimension_semantics=("arbitrary", "arbitrary"),
                                             vmem_limit_bytes=VMEM_LIMIT_BYTES),
        name="proj_tr",
    )(x, ng, w_tr, w_trw, qng, cos_t, sin_t, cosi_t, sini_t)


def _slab_reduce(x, op):
    parts = [x[r:r + SUBLANES, :] for r in range(0, x.shape[0], SUBLANES)]
    while len(parts) > 1:
        parts = [op(parts[i], parts[i + 1]) for i in range(0, len(parts), 2)]
    return parts[0]


def _attn_body(q_ref, qi_ref, wi_ref, zb_ref, k_ref, ki_ref, v_ref, y_ref,
               s_scr, r_scr, d_scr, att_scr, acc_scr, *, topk):
    n = pl.program_id(1)
    ntiles = n + 1
    idx_scale = (IDX_DIM ** -0.5) * (IDX_HEADS ** -0.5)
    rows_per_tile = TK // SUBLANES

    def score_tile(t, lo8, hi8, diagonal):
        ki_t = ki_ref[0, pl.ds(pl.multiple_of(t * TK, TK), TK), :]
        acc = jnp.zeros((TK, QB), F32)
        for hh in range(IDX_HEADS):
            logit = jnp.dot(ki_t, qi_ref[0, 0, hh * IDX_DIM:(hh + 1) * IDX_DIM, :],
                            preferred_element_type=F32)
            acc = acc + wi_ref[0, 0, hh:hh + 1, :] * jnp.maximum(logit, 0.0)
        sc = acc * idx_scale
        sc_for_min = sc
        if diagonal:
            kc = lax.broadcasted_iota(jnp.int32, (TK, QB), 0) // CHUNK
            qc = lax.broadcasted_iota(jnp.int32, (TK, QB), 1) // CHUNK
            admissible = kc <= qc
            sc_for_min = jnp.where(admissible, sc, jnp.inf)
            sc = jnp.where(admissible, sc, -jnp.inf)
        s_scr[t] = sc
        return (jnp.minimum(lo8, _slab_reduce(sc_for_min, jnp.minimum)),
                jnp.maximum(hi8, _slab_reduce(sc, jnp.maximum)))

    def tiles_in_pairs(count, tile_fn, carry):
        def pair(p, c):
            return tile_fn(2 * p + 1, tile_fn(2 * p, c))
        carry = lax.fori_loop(0, lax.shift_right_logical(count, 1), pair, carry)
        return lax.cond((count & 1) == 1, lambda c: tile_fn(count - 1, c), lambda c: c, carry)

    lo8, hi8 = tiles_in_pairs(
        n, lambda t, c: score_tile(t, c[0], c[1], False),
        (jnp.full((SUBLANES, QB), jnp.inf, F32), jnp.full((SUBLANES, QB), -jnp.inf, F32)))
    lo8, hi8 = score_tile(n, lo8, hi8, True)
    s_lo = jnp.min(lo8, axis=0, keepdims=True)
    s_hi = jnp.max(hi8, axis=0, keepdims=True)

    span = s_hi - s_lo
    rank_scale = jnp.where(span > 0.0, RANK_MAX / jnp.where(span > 0.0, span, 1.0), 0.0)

    def rank_tile(t, carry):
        r = jnp.minimum(jnp.floor((s_scr[t] - s_lo) * rank_scale), RANK_MAX)
        r = jnp.maximum(r, -1.0)
        r_scr[t] = r
        d1 = jnp.floor(r * (1.0 / DIGIT_BASE ** 2))
        rem = r - d1 * float(DIGIT_BASE ** 2)
        d2 = jnp.floor(rem * (1.0 / DIGIT_BASE))
        d_scr[0, t] = d1.astype(BF16)
        d_scr[1, t] = d2.astype(BF16)
        d_scr[2, t] = (rem - d2 * float(DIGIT_BASE)).astype(BF16)
        return carry

    lax.fori_loop(0, ntiles, rank_tile, 0)

    @pl.when((ntiles & 1) == 1)
    def _():
        for plane in range(3):
            d_scr[plane, ntiles] = jnp.full((TK, QB), -1.0, BF16)

    npairs = lax.shift_right_logical(ntiles + 1, 1)
    packed_rows = 2 * SUBLANES
    slabs_per_tile = TK // packed_rows
    one_bf, zero_bf = jnp.ones((), BF16), jnp.zeros((), BF16)

    def count_ge(plane, cand):
        cand_b = jnp.broadcast_to(cand.astype(BF16), (packed_rows, QB))

        def pair_body(p, accs):
            accs = list(accs)
            for half in range(2):
                for r in range(slabs_per_tile):
                    sl = d_scr[plane, 2 * p + half, r * packed_rows:(r + 1) * packed_rows, :]
                    c = r % COUNT_CHAINS
                    accs[c] = accs[c] + jnp.where(sl >= cand_b, one_bf, zero_bf)
            return tuple(accs)

        accs = lax.fori_loop(0, npairs, pair_body,
                             tuple(jnp.zeros((packed_rows, QB), BF16) for _ in range(COUNT_CHAINS)))
        total = functools.reduce(jnp.add, [a.astype(F32) for a in accs])
        return jnp.sum(total, axis=0, keepdims=True)

    def digit_search(plane, cnt_at):
        def bit_body(i, carry):
            prefix, cnt_at, step = carry
            trial = prefix + step
            cnt = count_ge(plane, trial)
            ok = cnt >= float(topk)
            return jnp.where(ok, trial, prefix), jnp.where(ok, cnt, cnt_at), step * 0.5
        digit, cnt_at, _ = lax.fori_loop(
            0, DIGIT_BITS, bit_body,
            (jnp.zeros((1, QB), F32), cnt_at, jnp.full((1, QB), DIGIT_BASE / 2.0, F32)))
        return digit, cnt_at

    def restrict_plane(plane, digit):
        digit_b = jnp.broadcast_to(digit.astype(BF16), (TK, QB))
        above, below = jnp.full((), float(DIGIT_BASE), BF16), jnp.full((), -1.0, BF16)

        def tile_body(t, carry):
            prev = d_scr[plane, t]
            d_scr[plane + 1, t] = jnp.where(prev > digit_b, above,
                                            jnp.where(prev < digit_b, below, d_scr[plane + 1, t]))
            return carry

        lax.fori_loop(0, ntiles, tile_body, 0)

    q_chunk = (n * QB + lax.broadcasted_iota(jnp.int32, (1, QB), 1)) // CHUNK
    n_admissible = ((q_chunk + 1) * CHUNK).astype(F32)
    digit1, cnt_at = digit_search(0, n_admissible)
    restrict_plane(0, digit1)
    digit2, cnt_at = digit_search(1, cnt_at)
    restrict_plane(1, digit2)
    digit3, cnt_at = digit_search(2, cnt_at)
    thr = (digit1 * float(DIGIT_BASE) + digit2) * float(DIGIT_BASE) + digit3

    excess0 = jnp.maximum(cnt_at - float(topk), 0.0)

    @pl.when(jnp.max(excess0) > 0.0)
    def _():
        thr_b = jnp.broadcast_to(thr, (SUBLANES, QB))
        sub_iota = lax.broadcasted_iota(jnp.int32, (SUBLANES, QB), 0).astype(F32)

        def slab(r):
            return slice(r * SUBLANES, (r + 1) * SUBLANES)

        def key_index(t, r):
            return sub_iota + ((t * TK).astype(F32) + float(r * SUBLANES))

        def drop_one(excess):
            active_b = jnp.broadcast_to(excess > 0.0, (SUBLANES, QB))

            def min_tile(t, ms):
                ms = list(ms)
                for r in range(rows_per_tile):
                    selected = r_scr[t, slab(r), :] >= thr_b
                    c = r % COUNT_CHAINS
                    ms[c] = jnp.minimum(ms[c], jnp.where(selected, s_scr[t, slab(r), :], jnp.inf))
                return tuple(ms)

            ms = lax.fori_loop(0, ntiles, min_tile,
                               tuple(jnp.full((SUBLANES, QB), jnp.inf, F32) for _ in range(COUNT_CHAINS)))
            m8 = functools.reduce(jnp.minimum, ms)
            m_b = jnp.broadcast_to(jnp.min(m8, axis=0, keepdims=True), (SUBLANES, QB))

            def idx_tile(t, js):
                js = list(js)
                for r in range(rows_per_tile):
                    hit = (r_scr[t, slab(r), :] >= thr_b) & (s_scr[t, slab(r), :] == m_b)
                    c = r % COUNT_CHAINS
                    js[c] = jnp.maximum(js[c], jnp.where(hit, key_index(t, r), -1.0))
                return tuple(js)

            js = lax.fori_loop(0, ntiles, idx_tile,
                               tuple(jnp.full((SUBLANES, QB), -1.0, F32) for _ in range(COUNT_CHAINS)))
            j8 = functools.reduce(jnp.maximum, js)
            j_b = jnp.broadcast_to(jnp.max(j8, axis=0, keepdims=True), (SUBLANES, QB))

            def drop_tile(t, carry):
                for r in range(rows_per_tile):
                    hit = (key_index(t, r) == j_b) & active_b
                    r_scr[t, slab(r), :] = jnp.where(hit, -jnp.inf, r_scr[t, slab(r), :])
                return carry

            lax.fori_loop(0, ntiles, drop_tile, 0)
            return excess - jnp.where(excess > 0.0, 1.0, 0.0)

        lax.while_loop(lambda e: jnp.max(e) > 0.0, drop_one, excess0)

    slab_reduce = _slab_reduce

    def logit_tile(t, m8):
        sel = r_scr[t] >= thr
        krows = pl.ds(pl.multiple_of(t * TK, TK), TK)
        new_m8 = []
        for hh in range(B_HEADS):
            hs = slice(hh * B_HEAD_DIM, (hh + 1) * B_HEAD_DIM)
            att = jnp.dot(k_ref[0, krows, hs], q_ref[0, 0, hs, :], preferred_element_type=F32)
            att = jnp.where(sel, att, NEG_BIG)
            att_scr[t * B_HEADS + hh] = att
            new_m8.append(jnp.maximum(m8[hh], slab_reduce(att, jnp.maximum)))
        return tuple(new_m8)

    m8 = lax.fori_loop(0, ntiles, logit_tile,
                       tuple(jnp.full((SUBLANES, QB), NEG_BIG, F32) for _ in range(B_HEADS)))
    m_row = [jnp.max(m, axis=0, keepdims=True) for m in m8]

    acc_scr[...] = jnp.zeros_like(acc_scr)

    def value_tile(t, l8):
        new_l8 = []
        for hh in range(B_HEADS):
            hs = slice(hh * B_HEAD_DIM, (hh + 1) * B_HEAD_DIM)
            p = jnp.exp(att_scr[t * B_HEADS + hh] - m_row[hh])
            new_l8.append(l8[hh] + slab_reduce(p, jnp.add))
            acc_scr[hs, :] += jnp.dot(v_ref[0, t, hs, :], p.astype(BF16), preferred_element_type=F32)
        return tuple(new_l8)

    l8 = tiles_in_pairs(ntiles, value_tile,
                        tuple(jnp.zeros((SUBLANES, QB), F32) for _ in range(B_HEADS)))

    parts = []
    for hh in range(B_HEADS):
        hs = slice(hh * B_HEAD_DIM, (hh + 1) * B_HEAD_DIM)
        l_row = jnp.sum(l8[hh], axis=0, keepdims=True)
        parts.append(acc_scr[hs, :] / l_row * zb_ref[0, 0, hs, :].astype(F32))
    y_ref[0] = jnp.concatenate(parts, axis=0).T.astype(BF16)


def _attn(qT, qiT, wiT, zbT, k, ki, vT, topk):
    b, nq = qT.shape[0], qT.shape[1]
    s = k.shape[1]
    blk = lambda rows: pl.BlockSpec((1, 1, rows, QB), lambda bi, i: (bi, i, 0, 0))
    return pl.pallas_call(
        functools.partial(_attn_body, topk=topk),
        grid=(b, nq),
        in_specs=[blk(B_WIDTH), blk(IDX_HEADS * IDX_DIM), blk(wiT.shape[2]), blk(B_WIDTH),
                  pl.BlockSpec((1, s, B_WIDTH), lambda bi, i: (bi, 0, 0), pipeline_mode=pl.Buffered(1)),
                  pl.BlockSpec((1, s, IDX_DIM), lambda bi, i: (bi, 0, 0), pipeline_mode=pl.Buffered(1)),
                  pl.BlockSpec((1, nq, B_WIDTH, QB), lambda bi, i: (bi, 0, 0, 0),
                               pipeline_mode=pl.Buffered(1))],
        out_specs=pl.BlockSpec((1, QB, B_WIDTH), lambda bi, i: (bi, i, 0)),
        out_shape=jax.ShapeDtypeStruct((b, s, B_WIDTH), BF16),
        scratch_shapes=[pltpu.VMEM((s // TK, TK, QB), F32),
                        pltpu.VMEM((s // TK, TK, QB), F32),
                        pltpu.VMEM((3, s // TK, TK, QB), BF16),
                        pltpu.VMEM((s // TK * B_HEADS, TK, QB), F32),
                        pltpu.VMEM((B_WIDTH, QB), F32)],
        compiler_params=pltpu.CompilerParams(dimension_semantics=("arbitrary", "arbitrary"),
                                             vmem_limit_bytes=VMEM_LIMIT_BYTES),
        name="attn",
    )(qT, qiT, wiT, zbT, k, ki, vT)


def _out_body(x_ref, yb_ref, ma_ref, sgb_ref, wob_ref, wout_ref, o_ref):
    o_b = jnp.dot(yb_ref[...], wob_ref[...], preferred_element_type=F32)
    merged = ma_ref[...].astype(F32) + sgb_ref[...].astype(F32) * o_b
    o_ref[...] = x_ref[...] + jnp.dot(merged.astype(BF16), wout_ref[...], preferred_element_type=F32)


def _out(x2d, yb, ma, sgb, wob, wout):
    m, d = x2d.shape
    tm = TM_OUT
    row = lambda width: pl.BlockSpec((tm, width), lambda i: (i, 0))
    full = lambda shape: pl.BlockSpec(shape, lambda i: (0,) * len(shape))
    return pl.pallas_call(
        _out_body,
        grid=(m // tm,),
        in_specs=[row(d), row(B_WIDTH), row(d), row(d), full(wob.shape), full(wout.shape)],
        out_specs=row(d),
        out_shape=jax.ShapeDtypeStruct((m, d), F32),
        compiler_params=pltpu.CompilerParams(dimension_semantics=("arbitrary",),
                                             vmem_limit_bytes=VMEM_LIMIT_BYTES),
        name="out_proj",
    )(x2d, yb, ma, sgb, wob, wout)


def _rope_tables(s, dim):
    pos = jnp.arange(s, dtype=F32)
    inv = ROPE_THETA ** (-jnp.arange(0, dim, 2, dtype=F32) / dim)
    ang = pos[:, None] * inv[None, :]
    return jnp.cos(ang), jnp.sin(ang)


def kernel(x, norm_g, w_in, gate_b, a_ln_g, a_ln_b, a_ws, a_bs, q_norm_g, k_norm_g, w_oa, w_ob, w_out):
    b, s, d = x.shape
    depth = w_in.shape[0]
    topk = min(TOPK_MAX, s // 4)
    assert s % TM_STD == 0 and s % QB == 0 and (b * s) % TM_OUT == 0 and QB == TK
    assert w_in.shape[2] == _O_END and topk <= TK

    cos_q, sin_q = _rope_tables(s, B_HEAD_DIM)
    cos_i, sin_i = _rope_tables(s, IDX_DIM)
    cos2 = jnp.concatenate([cos_q, cos_q], axis=1)
    sin2 = jnp.concatenate([-sin_q, sin_q], axis=1)
    cosi2 = jnp.concatenate([cos_i, cos_i, cos_i, cos_i], axis=1)
    sini2 = jnp.concatenate([-sin_i, sin_i, -sin_i, sin_i], axis=1)
    cos_t, sin_t, cosi_t, sini_t = cos_q.T, sin_q.T, cos_i.T, sin_i.T

    ik_cols = w_in[:, :, _O_IK:_O_IW]
    w_std = jnp.concatenate([w_in[:, :, _O_AU:_O_BQ], w_in[:, :, _O_BK:_O_BV], ik_cols, ik_cols,
                             w_in[:, :, _O_GA:_O_END]], axis=2).astype(BF16)
    w_tr = jnp.swapaxes(jnp.concatenate([w_in[:, :, _O_BQ:_O_BK], w_in[:, :, _O_BV:_O_IK]], axis=2),
                        1, 2).astype(BF16)
    iw_t = jnp.swapaxes(w_in[:, :, _O_IW:_O_GA], 1, 2)
    w_trw = jnp.concatenate([iw_t, jnp.zeros_like(iw_t)], axis=1).astype(BF16)
    woa, wob, wout = w_oa.astype(BF16), w_ob.astype(BF16), w_out.astype(BF16)
    bias_full = jnp.repeat(jnp.swapaxes(a_bs, 1, 2), A_WIDTH // A_GROUPS, axis=2)
    kng = jnp.reshape(k_norm_g, (depth, 1, B_HEAD_DIM))
    qng = jnp.reshape(q_norm_g, (depth, B_HEAD_DIM, 1))

    x2d = jnp.reshape(x, (b * s, d))
    for l in range(depth):
        ng = norm_g[l][None, :]
        ma, sgb, k, ki = _proj_std(x2d, ng, w_std[l], gate_b[l], a_ln_g[l][None, :], a_ln_b[l][None, :],
                                   a_ws[l], bias_full[l], kng[l], woa[l], cos2, sin2, cosi2, sini2, s)
        qT, vT, zbT, qiT, wiT = _proj_tr(jnp.reshape(x2d, (b, s, d)), ng, w_tr[l], w_trw[l], qng[l],
                                         cos_t, sin_t, cosi_t, sini_t)
        yb = _attn(qT, qiT, wiT, zbT, jnp.reshape(k, (b, s, B_WIDTH)), jnp.reshape(ki, (b, s, IDX_DIM)),
                   vT, topk)
        x2d = _out(x2d, jnp.reshape(yb, (b * s, B_WIDTH)), ma, sgb, wob[l], wout[l])
    return jnp.reshape(x2d, (b, s, d))
```

```python
import functools

import jax
import jax.numpy as jnp
from jax import lax
from jax.experimental import pallas as pl
from jax.experimental.pallas import tpu as pltpu

F32 = jnp.float32
BF16 = jnp.bfloat16

CHUNK = 64
EPS = 1e-6
ROPE_THETA = 10000.0

A_WIDTH = 512
A_GROUPS = 4
A_BLOCK = 128
B_HEADS = 4
B_HEAD_DIM = 128
B_WIDTH = B_HEADS * B_HEAD_DIM
IDX_HEADS = 8
IDX_DIM = 64
TOPK_MAX = 256

_O_AU, _O_AV, _O_AZ = 0, 512, 1024
_O_BQ, _O_BK, _O_BV, _O_BZ = 1536, 2048, 2560, 3072
_O_IQ, _O_IK, _O_IW = 3584, 4096, 4160
_O_GA, _O_GB, _O_END = 4168, 5192, 6216

LANES = 128
SUBLANES = 8
VMEM_LIMIT_BYTES = 52 * 1024 * 1024

QB = 256
TK = 256
TM_STD = 512
TM_OUT = 512
NEG_BIG = -1e30
DIGIT_BITS = 8
DIGIT_BASE = 2 ** DIGIT_BITS
RANK_MAX = float(DIGIT_BASE ** 3 - 1)
COUNT_CHAINS = 4


def _rms_rows(x, g):
    ms = jnp.mean(x * x, axis=-1, keepdims=True)
    return x * lax.rsqrt(ms + EPS) * g


def _proj_std_body(x_ref, ng_ref, w_ref, gb_ref, lng_ref, lnb_ref, ws_ref, bias_ref, kng_ref,
                   woa_ref, cos_ref, sin_ref, cosi_ref, sini_ref,
                   ma_ref, sgb_ref, k_ref, ki_ref):
    tm = x_ref.shape[0]
    h = _rms_rows(x_ref[...], ng_ref[...]).astype(BF16)

    def proj(lo, hi):
        return jnp.dot(h, w_ref[:, lo:hi], preferred_element_type=F32)

    gv = jax.nn.gelu(proj(512, 1024))
    mu = jnp.mean(gv, axis=-1, keepdims=True)
    xc = gv - mu
    var = jnp.mean(xc * xc, axis=-1, keepdims=True)
    vn = (xc * lax.rsqrt(var + EPS) * lng_ref[...] + lnb_ref[...]).astype(BF16)

    ci = lax.broadcasted_iota(jnp.int32, (A_BLOCK, A_BLOCK), 0) // CHUNK
    cj = lax.broadcasted_iota(jnp.int32, (A_BLOCK, A_BLOCK), 1) // CHUNK
    causal = cj <= ci
    wm = [jnp.where(causal, ws_ref[g], 0.0).astype(BF16) for g in range(A_GROUPS)]
    row_blocks = []
    for r in range(tm // A_BLOCK):
        cols = []
        for g in range(A_GROUPS):
            vb = vn[r * A_BLOCK:(r + 1) * A_BLOCK, g * LANES:(g + 1) * LANES]
            cols.append(jnp.dot(wm[g], vb, preferred_element_type=F32))
        row_blocks.append(jnp.concatenate(cols, axis=1) + bias_ref[...])
    mixed = jnp.concatenate(row_blocks, axis=0)

    y_a = jax.nn.gelu(proj(0, 512)) * mixed * jax.nn.silu(proj(1024, 1536))
    o_a = jnp.dot(y_a.astype(BF16), woa_ref[...], preferred_element_type=F32)
    ma_ref[...] = (jax.nn.sigmoid(proj(2176, 3200) + gb_ref[0:1, :]) * o_a).astype(BF16)
    sgb_ref[...] = jax.nn.sigmoid(proj(3200, 4224) + gb_ref[1:2, :]).astype(BF16)

    b_k = proj(1536, 2048)
    for hh in range(B_HEADS):
        kh = _rms_rows(b_k[:, hh * LANES:(hh + 1) * LANES], kng_ref[...])
        kh = kh * cos_ref[...] + pltpu.roll(kh, B_HEAD_DIM // 2, 1) * sin_ref[...]
        k_ref[:, hh * LANES:(hh + 1) * LANES] = kh.astype(BF16)

    ik = proj(2048, 2176)
    ik = ik * cosi_ref[...] + pltpu.roll(ik, IDX_DIM // 2, 1) * sini_ref[...]
    ki_ref[...] = ik[:, :IDX_DIM].astype(BF16)


def _proj_std(x2d, ng, w_std, gb, lng, lnb, ws, bias_full, kng, woa, cos2, sin2, cosi2, sini2, seq):
    m, d = x2d.shape
    tm = TM_STD
    nt = seq // tm
    full = lambda shape: pl.BlockSpec(shape, lambda i: (0,) * len(shape))
    tab = lambda: pl.BlockSpec((tm, LANES), lambda i: (i % nt, 0))
    row = lambda width: pl.BlockSpec((tm, width), lambda i: (i, 0))
    return pl.pallas_call(
        _proj_std_body,
        grid=(m // tm,),
        in_specs=[row(d), full(ng.shape), full(w_std.shape), full(gb.shape), full(lng.shape),
                  full(lnb.shape), full(ws.shape), full(bias_full.shape), full(kng.shape),
                  full(woa.shape), tab(), tab(), tab(), tab()],
        out_specs=[row(d), row(d), row(B_WIDTH), row(IDX_DIM)],
        out_shape=[jax.ShapeDtypeStruct((m, d), BF16), jax.ShapeDtypeStruct((m, d), BF16),
                   jax.ShapeDtypeStruct((m, B_WIDTH), BF16),
                   jax.ShapeDtypeStruct((m, IDX_DIM), BF16)],
        compiler_params=pltpu.CompilerParams(dimension_semantics=("arbitrary",),
                                             vmem_limit_bytes=VMEM_LIMIT_BYTES),
        name="proj_std",
    )(x2d, ng, w_std, gb, lng, lnb, ws, bias_full, kng, woa, cos2, sin2, cosi2, sini2)


def _proj_tr_body(x_ref, ng_ref, w_ref, ww_ref, qng_ref, cos_ref, sin_ref, cosi_ref, sini_ref,
                  q_ref, v_ref, zb_ref, qi_ref, wi_ref):
    h = _rms_rows(x_ref[0], ng_ref[...]).astype(BF16)

    def proj_t(w):
        return lax.dot_general(w, h, (((1,), (1,)), ((), ())), preferred_element_type=F32)

    att_scale = B_HEAD_DIM ** -0.5
    half = B_HEAD_DIM // 2
    qt = proj_t(w_ref[0:512, :])
    for hh in range(B_HEADS):
        qh = qt[hh * B_HEAD_DIM:(hh + 1) * B_HEAD_DIM, :]
        ms = jnp.mean(qh * qh, axis=0, keepdims=True)
        qh = qh * lax.rsqrt(ms + EPS) * qng_ref[...]
        x1, x2 = qh[:half, :], qh[half:, :]
        c, s = cos_ref[...], sin_ref[...]
        base = hh * B_HEAD_DIM
        q_ref[0, 0, base:base + half, :] = ((x1 * c - x2 * s) * att_scale).astype(BF16)
        q_ref[0, 0, base + half:base + B_HEAD_DIM, :] = ((x1 * s + x2 * c) * att_scale).astype(BF16)

    v_ref[0, 0] = proj_t(w_ref[512:1024, :]).astype(BF16)
    zb_ref[0, 0] = jax.nn.silu(proj_t(w_ref[1024:1536, :])).astype(BF16)

    qit = proj_t(w_ref[1536:2048, :])
    ih = IDX_DIM // 2
    for hh in range(IDX_HEADS):
        xh = qit[hh * IDX_DIM:(hh + 1) * IDX_DIM, :]
        x1, x2 = xh[:ih, :], xh[ih:, :]
        c, s = cosi_ref[...], sini_ref[...]
        base = hh * IDX_DIM
        qi_ref[0, 0, base:base + ih, :] = (x1 * c - x2 * s).astype(BF16)
        qi_ref[0, 0, base + ih:base + IDX_DIM, :] = (x1 * s + x2 * c).astype(BF16)

    wi_ref[0, 0] = proj_t(ww_ref[...])


def _proj_tr(x, ng, w_tr, w_trw, qng, cos_t, sin_t, cosi_t, sini_t):
    b, s, d = x.shape
    nq = s // QB
    full = lambda shape: pl.BlockSpec(shape, lambda bi, i: (0,) * len(shape))
    tabt = lambda rows: pl.BlockSpec((rows, QB), lambda bi, i: (0, i))
    outt = lambda rows: pl.BlockSpec((1, 1, rows, QB), lambda bi, i: (bi, i, 0, 0))
    wrows = w_trw.shape[0]
    return pl.pallas_call(
        _proj_tr_body,
        grid=(b, nq),
        in_specs=[pl.BlockSpec((1, QB, d), lambda bi, i: (bi, i, 0)), full(ng.shape),
                  full(w_tr.shape), full(w_trw.shape), full(qng.shape),
                  tabt(B_HEAD_DIM // 2), tabt(B_HEAD_DIM // 2), tabt(IDX_DIM // 2), tabt(IDX_DIM // 2)],
        out_specs=[outt(B_WIDTH), outt(B_WIDTH), outt(B_WIDTH), outt(IDX_HEADS * IDX_DIM), outt(wrows)],
        out_shape=[jax.ShapeDtypeStruct((b, nq, B_WIDTH, QB), BF16),
                   jax.ShapeDtypeStruct((b, nq, B_WIDTH, QB), BF16),
                   jax.ShapeDtypeStruct((b, nq, B_WIDTH, QB), BF16),
                   jax.ShapeDtypeStruct((b, nq, IDX_HEADS * IDX_DIM, QB), BF16),
                   jax.ShapeDtypeStruct((b, nq, wrows, QB), F32)],
        compiler_params=pltpu.CompilerParams(dimension_semantics=("arbitrary", "arbitrary"),
                                             vmem_limit_bytes=VMEM_LIMIT_BYTES),
        name="proj_tr",
    )(x, ng, w_tr, w_trw, qng, cos_t, sin_t, cosi_t, sini_t)


def _slab_reduce(x, op):
    parts = [x[r:r + SUBLANES, :] for r in range(0, x.shape[0], SUBLANES)]
    while len(parts) > 1:
        parts = [op(parts[i], parts[i + 1]) for i in range(0, len(parts), 2)]
    return parts[0]


def _attn_body(q_ref, qi_ref, wi_ref, zb_ref, k_ref, ki_ref, v_ref, y_ref,
               s_scr, r_scr, d_scr, att_scr, acc_scr, *, topk):
    n = pl.program_id(1)
    ntiles = n + 1
    idx_scale = (IDX_DIM ** -0.5) * (IDX_HEADS ** -0.5)
    rows_per_tile = TK // SUBLANES

    def score_tile(t, lo8, hi8, diagonal):
        ki_t = ki_ref[0, pl.ds(pl.multiple_of(t * TK, TK), TK), :]
        acc = jnp.zeros((TK, QB), F32)
        for hh in range(IDX_HEADS):
            logit = jnp.dot(ki_t, qi_ref[0, 0, hh * IDX_DIM:(hh + 1) * IDX_DIM, :],
                            preferred_element_type=F32)
            acc = acc + wi_ref[0, 0, hh:hh + 1, :] * jnp.maximum(logit, 0.0)
        sc = acc * idx_scale
        sc_for_min = sc
        if diagonal:
            kc = lax.broadcasted_iota(jnp.int32, (TK, QB), 0) // CHUNK
            qc = lax.broadcasted_iota(jnp.int32, (TK, QB), 1) // CHUNK
            admissible = kc <= qc
            sc_for_min = jnp.where(admissible, sc, jnp.inf)
            sc = jnp.where(admissible, sc, -jnp.inf)
        s_scr[t] = sc
        return (jnp.minimum(lo8, _slab_reduce(sc_for_min, jnp.minimum)),
                jnp.maximum(hi8, _slab_reduce(sc, jnp.maximum)))

    def tiles_in_pairs(count, tile_fn, carry):
        def pair(p, c):
            return tile_fn(2 * p + 1, tile_fn(2 * p, c))
        carry = lax.fori_loop(0, lax.shift_right_logical(count, 1), pair, carry)
        return lax.cond((count & 1) == 1, lambda c: tile_fn(count - 1, c), lambda c: c, carry)

    lo8, hi8 = tiles_in_pairs(
        n, lambda t, c: score_tile(t, c[0], c[1], False),
        (jnp.full((SUBLANES, QB), jnp.inf, F32), jnp.full((SUBLANES, QB), -jnp.inf, F32)))
    lo8, hi8 = score_tile(n, lo8, hi8, True)
    s_lo = jnp.min(lo8, axis=0, keepdims=True)
    s_hi = jnp.max(hi8, axis=0, keepdims=True)

    span = s_hi - s_lo
    rank_scale = jnp.where(span > 0.0, RANK_MAX / jnp.where(span > 0.0, span, 1.0), 0.0)

    def rank_tile(t, carry):
        r = jnp.minimum(jnp.floor((s_scr[t] - s_lo) * rank_scale), RANK_MAX)
        r = jnp.maximum(r, -1.0)
        r_scr[t] = r
        d1 = jnp.floor(r * (1.0 / DIGIT_BASE ** 2))
        rem = r - d1 * float(DIGIT_BASE ** 2)
        d2 = jnp.floor(rem * (1.0 / DIGIT_BASE))
        d_scr[0, t] = d1.astype(BF16)
        d_scr[1, t] = d2.astype(BF16)
        d_scr[2, t] = (rem - d2 * float(DIGIT_BASE)).astype(BF16)
        return carry

    lax.fori_loop(0, ntiles, rank_tile, 0)

    @pl.when((ntiles & 1) == 1)
    def _():
        for plane in range(3):
            d_scr[plane, ntiles] = jnp.full((TK, QB), -1.0, BF16)

    npairs = lax.shift_right_logical(ntiles + 1, 1)
    packed_rows = 2 * SUBLANES
    slabs_per_tile = TK // packed_rows
    one_bf, zero_bf = jnp.ones((), BF16), jnp.zeros((), BF16)

    def count_ge(plane, cand):
        cand_b = jnp.broadcast_to(cand.astype(BF16), (packed_rows, QB))

        def pair_body(p, accs):
            accs = list(accs)
            for half in range(2):
                for r in range(slabs_per_tile):
                    sl = d_scr[plane, 2 * p + half, r * packed_rows:(r + 1) * packed_rows, :]
                    c = r % COUNT_CHAINS
                    accs[c] = accs[c] + jnp.where(sl >= cand_b, one_bf, zero_bf)
            return tuple(accs)

        accs = lax.fori_loop(0, npairs, pair_body,
                             tuple(jnp.zeros((packed_rows, QB), BF16) for _ in range(COUNT_CHAINS)))
        total = functools.reduce(jnp.add, [a.astype(F32) for a in accs])
        return jnp.sum(total, axis=0, keepdims=True)

    def digit_search(plane, cnt_at):
        def bit_body(i, carry):
            prefix, cnt_at, step = carry
            trial = prefix + step
            cnt = count_ge(plane, trial)
            ok = cnt >= float(topk)
            return jnp.where(ok, trial, prefix), jnp.where(ok, cnt, cnt_at), step * 0.5
        digit, cnt_at, _ = lax.fori_loop(
            0, DIGIT_BITS, bit_body,
            (jnp.zeros((1, QB), F32), cnt_at, jnp.full((1, QB), DIGIT_BASE / 2.0, F32)))
        return digit, cnt_at

    def restrict_plane(plane, digit):
        digit_b = jnp.broadcast_to(digit.astype(BF16), (TK, QB))
        above, below = jnp.full((), float(DIGIT_BASE), BF16), jnp.full((), -1.0, BF16)

        def tile_body(t, carry):
            prev = d_scr[plane, t]
            d_scr[plane + 1, t] = jnp.where(prev > digit_b, above,
                                            jnp.where(prev < digit_b, below, d_scr[plane + 1, t]))
            return carry

        lax.fori_loop(0, ntiles, tile_body, 0)

    q_chunk = (n * QB + lax.broadcasted_iota(jnp.int32, (1, QB), 1)) // CHUNK
    n_admissible = ((q_chunk + 1) * CHUNK).astype(F32)
    digit1, cnt_at = digit_search(0, n_admissible)
    restrict_plane(0, digit1)
    digit2, cnt_at = digit_search(1, cnt_at)
    restrict_plane(1, digit2)
    digit3, cnt_at = digit_search(2, cnt_at)
    thr = (digit1 * float(DIGIT_BASE) + digit2) * float(DIGIT_BASE) + digit3

    excess0 = jnp.maximum(cnt_at - float(topk), 0.0)

    @pl.when(jnp.max(excess0) > 0.0)
    def _():
        thr_b = jnp.broadcast_to(thr, (SUBLANES, QB))
        later_or_same = (lax.broadcasted_iota(jnp.int32, (TK, TK), 1)
                         >= lax.broadcasted_iota(jnp.int32, (TK, TK), 0))
        suffix_ones = jnp.where(later_or_same, 1.0, 0.0).astype(BF16)

        def slab(r):
            return slice(r * SUBLANES, (r + 1) * SUBLANES)

        def drop_group(excess):
            def min_tile(t, ms):
                ms = list(ms)
                for r in range(rows_per_tile):
                    selected = r_scr[t, slab(r), :] >= thr_b
                    c = r % COUNT_CHAINS
                    ms[c] = jnp.minimum(ms[c], jnp.where(selected, s_scr[t, slab(r), :], jnp.inf))
                return tuple(ms)

            ms = lax.fori_loop(0, ntiles, min_tile,
                               tuple(jnp.full((SUBLANES, QB), jnp.inf, F32) for _ in range(COUNT_CHAINS)))
            m8 = functools.reduce(jnp.minimum, ms)
            m_row = jnp.min(m8, axis=0, keepdims=True)

            def drop_tile(i, later):
                t = ntiles - 1 - i
                r = r_scr[t]
                in_group = (r >= thr) & (s_scr[t] == m_row)
                member = jnp.where(in_group, 1.0, 0.0).astype(BF16)
                suffix = jnp.dot(suffix_ones, member, preferred_element_type=F32) + later
                r_scr[t] = jnp.where(in_group & (suffix <= excess), -1.0, r)
                return suffix[0:1, :]

            group_size = lax.fori_loop(0, ntiles, drop_tile, jnp.zeros((1, QB), F32))
            return excess - jnp.minimum(excess, group_size)

        lax.while_loop(lambda e: jnp.max(e) > 0.0, drop_group, excess0)

    slab_reduce = _slab_reduce

    def logit_tile(t, m8):
        sel = r_scr[t] >= thr
        krows = pl.ds(pl.multiple_of(t * TK, TK), TK)
        new_m8 = []
        for hh in range(B_HEADS):
            hs = slice(hh * B_HEAD_DIM, (hh + 1) * B_HEAD_DIM)
            att = jnp.dot(k_ref[0, krows, hs], q_ref[0, 0, hs, :], preferred_element_type=F32)
            att = jnp.where(sel, att, NEG_BIG)
            att_scr[t * B_HEADS + hh] = att
            new_m8.append(jnp.maximum(m8[hh], slab_reduce(att, jnp.maximum)))
        return tuple(new_m8)

    m8 = lax.fori_loop(0, ntiles, logit_tile,
                       tuple(jnp.full((SUBLANES, QB), NEG_BIG, F32) for _ in range(B_HEADS)))
    m_row = [jnp.max(m, axis=0, keepdims=True) for m in m8]

    acc_scr[...] = jnp.zeros_like(acc_scr)

    def value_tile(t, l8):
        new_l8 = []
        for hh in range(B_HEADS):
            hs = slice(hh * B_HEAD_DIM, (hh + 1) * B_HEAD_DIM)
            p = jnp.exp(att_scr[t * B_HEADS + hh] - m_row[hh])
            new_l8.append(l8[hh] + slab_reduce(p, jnp.add))
            acc_scr[hs, :] += jnp.dot(v_ref[0, t, hs, :], p.astype(BF16), preferred_element_type=F32)
        return tuple(new_l8)

    l8 = tiles_in_pairs(ntiles, value_tile,
                        tuple(jnp.zeros((SUBLANES, QB), F32) for _ in range(B_HEADS)))

    parts = []
    for hh in range(B_HEADS):
        hs = slice(hh * B_HEAD_DIM, (hh + 1) * B_HEAD_DIM)
        l_row = jnp.sum(l8[hh], axis=0, keepdims=True)
        parts.append(acc_scr[hs, :] / l_row * zb_ref[0, 0, hs, :].astype(F32))
    y_ref[0] = jnp.concatenate(parts, axis=0).T.astype(BF16)


def _attn(qT, qiT, wiT, zbT, k, ki, vT, topk):
    b, nq = qT.shape[0], qT.shape[1]
    s = k.shape[1]
    blk = lambda rows: pl.BlockSpec((1, 1, rows, QB), lambda bi, i: (bi, i, 0, 0))
    return pl.pallas_call(
        functools.partial(_attn_body, topk=topk),
        grid=(b, nq),
        in_specs=[blk(B_WIDTH), blk(IDX_HEADS * IDX_DIM), blk(wiT.shape[2]), blk(B_WIDTH),
                  pl.BlockSpec((1, s, B_WIDTH), lambda bi, i: (bi, 0, 0), pipeline_mode=pl.Buffered(1)),
                  pl.BlockSpec((1, s, IDX_DIM), lambda bi, i: (bi, 0, 0), pipeline_mode=pl.Buffered(1)),
                  pl.BlockSpec((1, nq, B_WIDTH, QB), lambda bi, i: (bi, 0, 0, 0),
                               pipeline_mode=pl.Buffered(1))],
        out_specs=pl.BlockSpec((1, QB, B_WIDTH), lambda bi, i: (bi, i, 0)),
        out_shape=jax.ShapeDtypeStruct((b, s, B_WIDTH), BF16),
        scratch_shapes=[pltpu.VMEM((s // TK, TK, QB), F32),
                        pltpu.VMEM((s // TK, TK, QB), F32),
                        pltpu.VMEM((3, s // TK, TK, QB), BF16),
                        pltpu.VMEM((s // TK * B_HEADS, TK, QB), F32),
                        pltpu.VMEM((B_WIDTH, QB), F32)],
        compiler_params=pltpu.CompilerParams(dimension_semantics=("arbitrary", "arbitrary"),
                                             vmem_limit_bytes=VMEM_LIMIT_BYTES),
        name="attn",
    )(qT, qiT, wiT, zbT, k, ki, vT)


def _out_body(x_ref, yb_ref, ma_ref, sgb_ref, wob_ref, wout_ref, o_ref):
    o_b = jnp.dot(yb_ref[...], wob_ref[...], preferred_element_type=F32)
    merged = ma_ref[...].astype(F32) + sgb_ref[...].astype(F32) * o_b
    o_ref[...] = x_ref[...] + jnp.dot(merged.astype(BF16), wout_ref[...], preferred_element_type=F32)


def _out(x2d, yb, ma, sgb, wob, wout):
    m, d = x2d.shape
    tm = TM_OUT
    row = lambda width: pl.BlockSpec((tm, width), lambda i: (i, 0))
    full = lambda shape: pl.BlockSpec(shape, lambda i: (0,) * len(shape))
    return pl.pallas_call(
        _out_body,
        grid=(m // tm,),
        in_specs=[row(d), row(B_WIDTH), row(d), row(d), full(wob.shape), full(wout.shape)],
        out_specs=row(d),
        out_shape=jax.ShapeDtypeStruct((m, d), F32),
        compiler_params=pltpu.CompilerParams(dimension_semantics=("arbitrary",),
                                             vmem_limit_bytes=VMEM_LIMIT_BYTES),
        name="out_proj",
    )(x2d, yb, ma, sgb, wob, wout)


def _rope_tables(s, dim):
    pos = jnp.arange(s, dtype=F32)
    inv = ROPE_THETA ** (-jnp.arange(0, dim, 2, dtype=F32) / dim)
    ang = pos[:, None] * inv[None, :]
    return jnp.cos(ang), jnp.sin(ang)


def kernel(x, norm_g, w_in, gate_b, a_ln_g, a_ln_b, a_ws, a_bs, q_norm_g, k_norm_g, w_oa, w_ob, w_out):
    b, s, d = x.shape
    depth = w_in.shape[0]
    topk = min(TOPK_MAX, s // 4)
    assert s % TM_STD == 0 and s % QB == 0 and (b * s) % TM_OUT == 0 and QB == TK
    assert w_in.shape[2] == _O_END and topk <= TK

    cos_q, sin_q = _rope_tables(s, B_HEAD_DIM)
    cos_i, sin_i = _rope_tables(s, IDX_DIM)
    cos2 = jnp.concatenate([cos_q, cos_q], axis=1)
    sin2 = jnp.concatenate([-sin_q, sin_q], axis=1)
    cosi2 = jnp.concatenate([cos_i, cos_i, cos_i, cos_i], axis=1)
    sini2 = jnp.concatenate([-sin_i, sin_i, -sin_i, sin_i], axis=1)
    cos_t, sin_t, cosi_t, sini_t = cos_q.T, sin_q.T, cos_i.T, sin_i.T

    ik_cols = w_in[:, :, _O_IK:_O_IW]
    w_std = jnp.concatenate([w_in[:, :, _O_AU:_O_BQ], w_in[:, :, _O_BK:_O_BV], ik_cols, ik_cols,
                             w_in[:, :, _O_GA:_O_END]], axis=2).astype(BF16)
    w_tr = jnp.swapaxes(jnp.concatenate([w_in[:, :, _O_BQ:_O_BK], w_in[:, :, _O_BV:_O_IK]], axis=2),
                        1, 2).astype(BF16)
    iw_t = jnp.swapaxes(w_in[:, :, _O_IW:_O_GA], 1, 2)
    w_trw = jnp.concatenate([iw_t, jnp.zeros_like(iw_t)], axis=1).astype(BF16)
    woa, wob, wout = w_oa.astype(BF16), w_ob.astype(BF16), w_out.astype(BF16)
    bias_full = jnp.repeat(jnp.swapaxes(a_bs, 1, 2), A_WIDTH // A_GROUPS, axis=2)
    kng = jnp.reshape(k_norm_g, (depth, 1, B_HEAD_DIM))
    qng = jnp.reshape(q_norm_g, (depth, B_HEAD_DIM, 1))

    x2d = jnp.reshape(x, (b * s, d))
    for l in range(depth):
        ng = norm_g[l][None, :]
        ma, sgb, k, ki = _proj_std(x2d, ng, w_std[l], gate_b[l], a_ln_g[l][None, :], a_ln_b[l][None, :],
                                   a_ws[l], bias_full[l], kng[l], woa[l], cos2, sin2, cosi2, sini2, s)
        qT, vT, zbT, qiT, wiT = _proj_tr(jnp.reshape(x2d, (b, s, d)), ng, w_tr[l], w_trw[l], qng[l],
                                         cos_t, sin_t, cosi_t, sini_t)
        yb = _attn(qT, qiT, wiT, zbT, jnp.reshape(k, (b, s, B_WIDTH)), jnp.reshape(ki, (b, s, IDX_DIM)),
                   vT, topk)
        x2d = _out(x2d, jnp.reshape(yb, (b * s, B_WIDTH)), ma, sgb, wob[l], wout[l])
    return jnp.reshape(x2d, (b, s, d))
```

```python
import functools

import jax
import jax.numpy as jnp
from jax import lax
from jax.experimental import pallas as pl
from jax.experimental.pallas import tpu as pltpu

F32 = jnp.float32
BF16 = jnp.bfloat16

CHUNK = 64
EPS = 1e-6
ROPE_THETA = 10000.0

A_WIDTH = 512
A_GROUPS = 4
A_BLOCK = 128
B_HEADS = 4
B_HEAD_DIM = 128
B_WIDTH = B_HEADS * B_HEAD_DIM
IDX_HEADS = 8
IDX_DIM = 64
TOPK_MAX = 256

_O_AU, _O_AV, _O_AZ = 0, 512, 1024
_O_BQ, _O_BK, _O_BV, _O_BZ = 1536, 2048, 2560, 3072
_O_IQ, _O_IK, _O_IW = 3584, 4096, 4160
_O_GA, _O_GB, _O_END = 4168, 5192, 6216

LANES = 128
SUBLANES = 8
VMEM_LIMIT_BYTES = 52 * 1024 * 1024

QB = 256
TK = 256
TM_STD = 512
TM_OUT = 512
NEG_BIG = -1e30
DIGIT_BITS = 8
DIGIT_BASE = 2 ** DIGIT_BITS
RANK_MAX = float(DIGIT_BASE ** 3 - 1)
COUNT_CHAINS = 4


def _rms_rows(x, g):
    ms = jnp.mean(x * x, axis=-1, keepdims=True)
    return x * lax.rsqrt(ms + EPS) * g


def _proj_std_body(x_ref, ng_ref, w_ref, gb_ref, lng_ref, lnb_ref, ws_ref, bias_ref, kng_ref,
                   woa_ref, cos_ref, sin_ref, cosi_ref, sini_ref,
                   ma_ref, sgb_ref, k_ref, ki_ref):
    tm = x_ref.shape[0]
    h = _rms_rows(x_ref[...], ng_ref[...]).astype(BF16)

    def proj(lo, hi):
        return jnp.dot(h, w_ref[:, lo:hi], preferred_element_type=F32)

    gv = jax.nn.gelu(proj(512, 1024))
    mu = jnp.mean(gv, axis=-1, keepdims=True)
    xc = gv - mu
    var = jnp.mean(xc * xc, axis=-1, keepdims=True)
    vn = (xc * lax.rsqrt(var + EPS) * lng_ref[...] + lnb_ref[...]).astype(BF16)

    ci = lax.broadcasted_iota(jnp.int32, (A_BLOCK, A_BLOCK), 0) // CHUNK
    cj = lax.broadcasted_iota(jnp.int32, (A_BLOCK, A_BLOCK), 1) // CHUNK
    causal = cj <= ci
    wm = [jnp.where(causal, ws_ref[g], 0.0).astype(BF16) for g in range(A_GROUPS)]
    row_blocks = []
    for r in range(tm // A_BLOCK):
        cols = []
        for g in range(A_GROUPS):
            vb = vn[r * A_BLOCK:(r + 1) * A_BLOCK, g * LANES:(g + 1) * LANES]
            cols.append(jnp.dot(wm[g], vb, preferred_element_type=F32))
        row_blocks.append(jnp.concatenate(cols, axis=1) + bias_ref[...])
    mixed = jnp.concatenate(row_blocks, axis=0)

    y_a = jax.nn.gelu(proj(0, 512)) * mixed * jax.nn.silu(proj(1024, 1536))
    o_a = jnp.dot(y_a.astype(BF16), woa_ref[...], preferred_element_type=F32)
    ma_ref[...] = (jax.nn.sigmoid(proj(2176, 3200) + gb_ref[0:1, :]) * o_a).astype(BF16)
    sgb_ref[...] = jax.nn.sigmoid(proj(3200, 4224) + gb_ref[1:2, :]).astype(BF16)

    b_k = proj(1536, 2048)
    for hh in range(B_HEADS):
        kh = _rms_rows(b_k[:, hh * LANES:(hh + 1) * LANES], kng_ref[...])
        kh = kh * cos_ref[...] + pltpu.roll(kh, B_HEAD_DIM // 2, 1) * sin_ref[...]
        k_ref[:, hh * LANES:(hh + 1) * LANES] = kh.astype(BF16)

    ik = proj(2048, 2176)
    ik = ik * cosi_ref[...] + pltpu.roll(ik, IDX_DIM // 2, 1) * sini_ref[...]
    ki_ref[...] = ik[:, :IDX_DIM].astype(BF16)


def _layer_spec(arr, layer, grid_rank):
    zeros = (0,) * (arr.ndim - 1)
    if grid_rank == 1:
        return pl.BlockSpec((None,) + arr.shape[1:], lambda i: (layer,) + zeros)
    return pl.BlockSpec((None,) + arr.shape[1:], lambda bi, i: (layer,) + zeros)


def _proj_std(x2d, layer, ng, w_std, gb, lng, lnb, ws, bias_full, kng, woa, cos2, sin2, cosi2, sini2, seq):
    m, d = x2d.shape
    tm = TM_STD
    nt = seq // tm
    full = lambda arr: _layer_spec(arr, layer, 1)
    tab = lambda: pl.BlockSpec((tm, LANES), lambda i: (i % nt, 0))
    row = lambda width: pl.BlockSpec((tm, width), lambda i: (i, 0))
    return pl.pallas_call(
        _proj_std_body,
        grid=(m // tm,),
        in_specs=[row(d), full(ng), full(w_std), full(gb), full(lng),
                  full(lnb), full(ws), full(bias_full), full(kng),
                  full(woa), tab(), tab(), tab(), tab()],
        out_specs=[row(d), row(d), row(B_WIDTH), row(IDX_DIM)],
        out_shape=[jax.ShapeDtypeStruct((m, d), BF16), jax.ShapeDtypeStruct((m, d), BF16),
                   jax.ShapeDtypeStruct((m, B_WIDTH), BF16),
                   jax.ShapeDtypeStruct((m, IDX_DIM), BF16)],
        compiler_params=pltpu.CompilerParams(dimension_semantics=("arbitrary",),
                                             vmem_limit_bytes=VMEM_LIMIT_BYTES),
        name="proj_std",
    )(x2d, ng, w_std, gb, lng, lnb, ws, bias_full, kng, woa, cos2, sin2, cosi2, sini2)


def _proj_tr_body(x_ref, ng_ref, w_ref, ww_ref, qng_ref, cos_ref, sin_ref, cosi_ref, sini_ref,
                  q_ref, v_ref, zb_ref, qi_ref, wi_ref):
    h = _rms_rows(x_ref[0], ng_ref[...]).astype(BF16)

    def proj_t(w):
        return lax.dot_general(w, h, (((1,), (1,)), ((), ())), preferred_element_type=F32)

    att_scale = B_HEAD_DIM ** -0.5
    half = B_HEAD_DIM // 2
    qt = proj_t(w_ref[0:512, :])
    for hh in range(B_HEADS):
        qh = qt[hh * B_HEAD_DIM:(hh + 1) * B_HEAD_DIM, :]
        ms = jnp.mean(qh * qh, axis=0, keepdims=True)
        qh = qh * lax.rsqrt(ms + EPS) * qng_ref[...]
        x1, x2 = qh[:half, :], qh[half:, :]
        c, s = cos_ref[...], sin_ref[...]
        base = hh * B_HEAD_DIM
        q_ref[0, 0, base:base + half, :] = ((x1 * c - x2 * s) * att_scale).astype(BF16)
        q_ref[0, 0, base + half:base + B_HEAD_DIM, :] = ((x1 * s + x2 * c) * att_scale).astype(BF16)

    v_ref[0, 0] = proj_t(w_ref[512:1024, :]).astype(BF16)
    zb_ref[0, 0] = jax.nn.silu(proj_t(w_ref[1024:1536, :])).astype(BF16)

    qit = proj_t(w_ref[1536:2048, :])
    ih = IDX_DIM // 2
    for hh in range(IDX_HEADS):
        xh = qit[hh * IDX_DIM:(hh + 1) * IDX_DIM, :]
        x1, x2 = xh[:ih, :], xh[ih:, :]
        c, s = cosi_ref[...], sini_ref[...]
        base = hh * IDX_DIM
        qi_ref[0, 0, base:base + ih, :] = (x1 * c - x2 * s).astype(BF16)
        qi_ref[0, 0, base + ih:base + IDX_DIM, :] = (x1 * s + x2 * c).astype(BF16)

    wi_ref[0, 0] = proj_t(ww_ref[...])


def _proj_tr(x, layer, ng, w_tr, w_trw, qng, cos_t, sin_t, cosi_t, sini_t):
    b, s, d = x.shape
    nq = s // QB
    full = lambda arr: _layer_spec(arr, layer, 2)
    tabt = lambda rows: pl.BlockSpec((rows, QB), lambda bi, i: (0, i))
    outt = lambda rows: pl.BlockSpec((1, 1, rows, QB), lambda bi, i: (bi, i, 0, 0))
    wrows = w_trw.shape[1]
    return pl.pallas_call(
        _proj_tr_body,
        grid=(b, nq),
        in_specs=[pl.BlockSpec((1, QB, d), lambda bi, i: (bi, i, 0)), full(ng),
                  full(w_tr), full(w_trw), full(qng),
                  tabt(B_HEAD_DIM // 2), tabt(B_HEAD_DIM // 2), tabt(IDX_DIM // 2), tabt(IDX_DIM // 2)],
        out_specs=[outt(B_WIDTH), outt(B_WIDTH), outt(B_WIDTH), outt(IDX_HEADS * IDX_DIM), outt(wrows)],
        out_shape=[jax.ShapeDtypeStruct((b, nq, B_WIDTH, QB), BF16),
                   jax.ShapeDtypeStruct((b, nq, B_WIDTH, QB), BF16),
                   jax.ShapeDtypeStruct((b, nq, B_WIDTH, QB), BF16),
                   jax.ShapeDtypeStruct((b, nq, IDX_HEADS * IDX_DIM, QB), BF16),
                   jax.ShapeDtypeStruct((b, nq, wrows, QB), F32)],
        compiler_params=pltpu.CompilerParams(dimension_semantics=("arbitrary", "arbitrary"),
                                             vmem_limit_bytes=VMEM_LIMIT_BYTES),
        name="proj_tr",
    )(x, ng, w_tr, w_trw, qng, cos_t, sin_t, cosi_t, sini_t)


def _slab_reduce(x, op):
    parts = [x[r:r + SUBLANES, :] for r in range(0, x.shape[0], SUBLANES)]
    while len(parts) > 1:
        parts = [op(parts[i], parts[i + 1]) for i in range(0, len(parts), 2)]
    return parts[0]


def _attn_body(q_ref, qi_ref, wi_ref, zb_ref, k_ref, ki_ref, v_ref, y_ref,
               s_scr, r_scr, d_scr, att_scr, acc_scr, *, topk):
    n = pl.program_id(1)
    ntiles = n + 1
    idx_scale = (IDX_DIM ** -0.5) * (IDX_HEADS ** -0.5)
    rows_per_tile = TK // SUBLANES

    def score_tile(t, lo8, hi8, diagonal):
        ki_t = ki_ref[0, pl.ds(pl.multiple_of(t * TK, TK), TK), :]
        acc = jnp.zeros((TK, QB), F32)
        for hh in range(IDX_HEADS):
            logit = jnp.dot(ki_t, qi_ref[0, 0, hh * IDX_DIM:(hh + 1) * IDX_DIM, :],
                            preferred_element_type=F32)
            acc = acc + wi_ref[0, 0, hh:hh + 1, :] * jnp.maximum(logit, 0.0)
        sc = acc * idx_scale
        sc_for_min = sc
        if diagonal:
            kc = lax.broadcasted_iota(jnp.int32, (TK, QB), 0) // CHUNK
            qc = lax.broadcasted_iota(jnp.int32, (TK, QB), 1) // CHUNK
            admissible = kc <= qc
            sc_for_min = jnp.where(admissible, sc, jnp.inf)
            sc = jnp.where(admissible, sc, -jnp.inf)
        s_scr[t] = sc
        return (jnp.minimum(lo8, _slab_reduce(sc_for_min, jnp.minimum)),
                jnp.maximum(hi8, _slab_reduce(sc, jnp.maximum)))

    def tiles_in_pairs(count, tile_fn, carry):
        def pair(p, c):
            return tile_fn(2 * p + 1, tile_fn(2 * p, c))
        carry = lax.fori_loop(0, lax.shift_right_logical(count, 1), pair, carry)
        return lax.cond((count & 1) == 1, lambda c: tile_fn(count - 1, c), lambda c: c, carry)

    lo8, hi8 = tiles_in_pairs(
        n, lambda t, c: score_tile(t, c[0], c[1], False),
        (jnp.full((SUBLANES, QB), jnp.inf, F32), jnp.full((SUBLANES, QB), -jnp.inf, F32)))
    lo8, hi8 = score_tile(n, lo8, hi8, True)
    s_lo = jnp.min(lo8, axis=0, keepdims=True)
    s_hi = jnp.max(hi8, axis=0, keepdims=True)

    span = s_hi - s_lo
    rank_scale = jnp.where(span > 0.0, RANK_MAX / jnp.where(span > 0.0, span, 1.0), 0.0)

    def rank_tile(t, carry):
        r = jnp.minimum(jnp.floor((s_scr[t] - s_lo) * rank_scale), RANK_MAX)
        r = jnp.maximum(r, -1.0)
        r_scr[t] = r
        d1 = jnp.floor(r * (1.0 / DIGIT_BASE ** 2))
        rem = r - d1 * float(DIGIT_BASE ** 2)
        d2 = jnp.floor(rem * (1.0 / DIGIT_BASE))
        d_scr[0, t] = d1.astype(BF16)
        d_scr[1, t] = d2.astype(BF16)
        d_scr[2, t] = (rem - d2 * float(DIGIT_BASE)).astype(BF16)
        return carry

    lax.fori_loop(0, ntiles, rank_tile, 0)

    @pl.when((ntiles & 1) == 1)
    def _():
        for plane in range(3):
            d_scr[plane, ntiles] = jnp.full((TK, QB), -1.0, BF16)

    npairs = lax.shift_right_logical(ntiles + 1, 1)
    packed_rows = 2 * SUBLANES
    slabs_per_tile = TK // packed_rows
    one_bf, zero_bf = jnp.ones((), BF16), jnp.zeros((), BF16)

    def count_ge(plane, cand):
        cand_b = jnp.broadcast_to(cand.astype(BF16), (packed_rows, QB))

        def pair_body(p, accs):
            accs = list(accs)
            for half in range(2):
                for r in range(slabs_per_tile):
                    sl = d_scr[plane, 2 * p + half, r * packed_rows:(r + 1) * packed_rows, :]
                    c = r % COUNT_CHAINS
                    accs[c] = accs[c] + jnp.where(sl >= cand_b, one_bf, zero_bf)
            return tuple(accs)

        accs = lax.fori_loop(0, npairs, pair_body,
                             tuple(jnp.zeros((packed_rows, QB), BF16) for _ in range(COUNT_CHAINS)))
        total = functools.reduce(jnp.add, [a.astype(F32) for a in accs])
        return jnp.sum(total, axis=0, keepdims=True)

    def digit_search(plane, cnt_at):
        def bit_body(i, carry):
            prefix, cnt_at, step = carry
            trial = prefix + step
            cnt = count_ge(plane, trial)
            ok = cnt >= float(topk)
            return jnp.where(ok, trial, prefix), jnp.where(ok, cnt, cnt_at), step * 0.5
        digit, cnt_at, _ = lax.fori_loop(
            0, DIGIT_BITS, bit_body,
            (jnp.zeros((1, QB), F32), cnt_at, jnp.full((1, QB), DIGIT_BASE / 2.0, F32)))
        return digit, cnt_at

    def restrict_plane(plane, digit):
        digit_b = jnp.broadcast_to(digit.astype(BF16), (TK, QB))
        above, below = jnp.full((), float(DIGIT_BASE), BF16), jnp.full((), -1.0, BF16)

        def tile_body(t, carry):
            prev = d_scr[plane, t]
            d_scr[plane + 1, t] = jnp.where(prev > digit_b, above,
                                            jnp.where(prev < digit_b, below, d_scr[plane + 1, t]))
            return carry

        lax.fori_loop(0, ntiles, tile_body, 0)

    q_chunk = (n * QB + lax.broadcasted_iota(jnp.int32, (1, QB), 1)) // CHUNK
    n_admissible = ((q_chunk + 1) * CHUNK).astype(F32)
    digit1, cnt_at = digit_search(0, n_admissible)
    restrict_plane(0, digit1)
    digit2, cnt_at = digit_search(1, cnt_at)
    restrict_plane(1, digit2)
    digit3, cnt_at = digit_search(2, cnt_at)
    thr = (digit1 * float(DIGIT_BASE) + digit2) * float(DIGIT_BASE) + digit3

    excess0 = jnp.maximum(cnt_at - float(topk), 0.0)

    @pl.when(jnp.max(excess0) > 0.0)
    def _():
        thr_b = jnp.broadcast_to(thr, (SUBLANES, QB))
        later_or_same = (lax.broadcasted_iota(jnp.int32, (TK, TK), 1)
                         >= lax.broadcasted_iota(jnp.int32, (TK, TK), 0))
        suffix_ones = jnp.where(later_or_same, 1.0, 0.0).astype(BF16)

        def slab(r):
            return slice(r * SUBLANES, (r + 1) * SUBLANES)

        def drop_group(excess):
            def min_tile(t, ms):
                ms = list(ms)
                for r in range(rows_per_tile):
                    selected = r_scr[t, slab(r), :] >= thr_b
                    c = r % COUNT_CHAINS
                    ms[c] = jnp.minimum(ms[c], jnp.where(selected, s_scr[t, slab(r), :], jnp.inf))
                return tuple(ms)

            ms = lax.fori_loop(0, ntiles, min_tile,
                               tuple(jnp.full((SUBLANES, QB), jnp.inf, F32) for _ in range(COUNT_CHAINS)))
            m8 = functools.reduce(jnp.minimum, ms)
            m_row = jnp.min(m8, axis=0, keepdims=True)

            def drop_tile(i, later):
                t = ntiles - 1 - i
                r = r_scr[t]
                in_group = (r >= thr) & (s_scr[t] == m_row)
                member = jnp.where(in_group, 1.0, 0.0).astype(BF16)
                suffix = jnp.dot(suffix_ones, member, preferred_element_type=F32) + later
                r_scr[t] = jnp.where(in_group & (suffix <= excess), -1.0, r)
                return suffix[0:1, :]

            group_size = lax.fori_loop(0, ntiles, drop_tile, jnp.zeros((1, QB), F32))
            return excess - jnp.minimum(excess, group_size)

        lax.while_loop(lambda e: jnp.max(e) > 0.0, drop_group, excess0)

    slab_reduce = _slab_reduce

    def logit_tile(t, m8):
        sel = r_scr[t] >= thr
        krows = pl.ds(pl.multiple_of(t * TK, TK), TK)
        new_m8 = []
        for hh in range(B_HEADS):
            hs = slice(hh * B_HEAD_DIM, (hh + 1) * B_HEAD_DIM)
            att = jnp.dot(k_ref[0, krows, hs], q_ref[0, 0, hs, :], preferred_element_type=F32)
            att = jnp.where(sel, att, NEG_BIG)
            att_scr[t * B_HEADS + hh] = att
            new_m8.append(jnp.maximum(m8[hh], slab_reduce(att, jnp.maximum)))
        return tuple(new_m8)

    m8 = lax.fori_loop(0, ntiles, logit_tile,
                       tuple(jnp.full((SUBLANES, QB), NEG_BIG, F32) for _ in range(B_HEADS)))
    m_row = [jnp.max(m, axis=0, keepdims=True) for m in m8]

    acc_scr[...] = jnp.zeros_like(acc_scr)

    def value_tile(t, l8):
        new_l8 = []
        for hh in range(B_HEADS):
            hs = slice(hh * B_HEAD_DIM, (hh + 1) * B_HEAD_DIM)
            p = jnp.exp(att_scr[t * B_HEADS + hh] - m_row[hh])
            new_l8.append(l8[hh] + slab_reduce(p, jnp.add))
            acc_scr[hs, :] += jnp.dot(v_ref[0, t, hs, :], p.astype(BF16), preferred_element_type=F32)
        return tuple(new_l8)

    l8 = tiles_in_pairs(ntiles, value_tile,
                        tuple(jnp.zeros((SUBLANES, QB), F32) for _ in range(B_HEADS)))

    parts = []
    for hh in range(B_HEADS):
        hs = slice(hh * B_HEAD_DIM, (hh + 1) * B_HEAD_DIM)
        l_row = jnp.sum(l8[hh], axis=0, keepdims=True)
        parts.append(acc_scr[hs, :] / l_row * zb_ref[0, 0, hs, :].astype(F32))
    y_ref[0] = jnp.concatenate(parts, axis=0).T.astype(BF16)


def _attn(qT, qiT, wiT, zbT, k, ki, vT, topk):
    b, nq = qT.shape[0], qT.shape[1]
    s = k.shape[1]
    blk = lambda rows: pl.BlockSpec((1, 1, rows, QB), lambda bi, i: (bi, i, 0, 0))
    return pl.pallas_call(
        functools.partial(_attn_body, topk=topk),
        grid=(b, nq),
        in_specs=[blk(B_WIDTH), blk(IDX_HEADS * IDX_DIM), blk(wiT.shape[2]), blk(B_WIDTH),
                  pl.BlockSpec((1, s, B_WIDTH), lambda bi, i: (bi, 0, 0), pipeline_mode=pl.Buffered(1)),
                  pl.BlockSpec((1, s, IDX_DIM), lambda bi, i: (bi, 0, 0), pipeline_mode=pl.Buffered(1)),
                  pl.BlockSpec((1, nq, B_WIDTH, QB), lambda bi, i: (bi, 0, 0, 0),
                               pipeline_mode=pl.Buffered(1))],
        out_specs=pl.BlockSpec((1, QB, B_WIDTH), lambda bi, i: (bi, i, 0)),
        out_shape=jax.ShapeDtypeStruct((b, s, B_WIDTH), BF16),
        scratch_shapes=[pltpu.VMEM((s // TK, TK, QB), F32),
                        pltpu.VMEM((s // TK, TK, QB), F32),
                        pltpu.VMEM((3, s // TK, TK, QB), BF16),
                        pltpu.VMEM((s // TK * B_HEADS, TK, QB), F32),
                        pltpu.VMEM((B_WIDTH, QB), F32)],
        compiler_params=pltpu.CompilerParams(dimension_semantics=("arbitrary", "arbitrary"),
                                             vmem_limit_bytes=VMEM_LIMIT_BYTES),
        name="attn",
    )(qT, qiT, wiT, zbT, k, ki, vT)


def _out_body(x_ref, yb_ref, ma_ref, sgb_ref, wob_ref, wout_ref, o_ref):
    o_b = jnp.dot(yb_ref[...], wob_ref[...], preferred_element_type=F32)
    merged = ma_ref[...].astype(F32) + sgb_ref[...].astype(F32) * o_b
    o_ref[...] = x_ref[...] + jnp.dot(merged.astype(BF16), wout_ref[...], preferred_element_type=F32)


def _out(x2d, layer, yb, ma, sgb, wob, wout):
    m, d = x2d.shape
    tm = TM_OUT
    row = lambda width: pl.BlockSpec((tm, width), lambda i: (i, 0))
    full = lambda arr: _layer_spec(arr, layer, 1)
    return pl.pallas_call(
        _out_body,
        grid=(m // tm,),
        in_specs=[row(d), row(B_WIDTH), row(d), row(d), full(wob), full(wout)],
        out_specs=row(d),
        out_shape=jax.ShapeDtypeStruct((m, d), F32),
        compiler_params=pltpu.CompilerParams(dimension_semantics=("arbitrary",),
                                             vmem_limit_bytes=VMEM_LIMIT_BYTES),
        name="out_proj",
    )(x2d, yb, ma, sgb, wob, wout)


def _rope_tables(s, dim):
    pos = jnp.arange(s, dtype=F32)
    inv = ROPE_THETA ** (-jnp.arange(0, dim, 2, dtype=F32) / dim)
    ang = pos[:, None] * inv[None, :]
    return jnp.cos(ang), jnp.sin(ang)


def kernel(x, norm_g, w_in, gate_b, a_ln_g, a_ln_b, a_ws, a_bs, q_norm_g, k_norm_g, w_oa, w_ob, w_out):
    b, s, d = x.shape
    depth = w_in.shape[0]
    topk = min(TOPK_MAX, s // 4)
    assert s % TM_STD == 0 and s % QB == 0 and (b * s) % TM_OUT == 0 and QB == TK
    assert w_in.shape[2] == _O_END and topk <= TK

    cos_q, sin_q = _rope_tables(s, B_HEAD_DIM)
    cos_i, sin_i = _rope_tables(s, IDX_DIM)
    cos2 = jnp.concatenate([cos_q, cos_q], axis=1)
    sin2 = jnp.concatenate([-sin_q, sin_q], axis=1)
    cosi2 = jnp.concatenate([cos_i, cos_i, cos_i, cos_i], axis=1)
    sini2 = jnp.concatenate([-sin_i, sin_i, -sin_i, sin_i], axis=1)
    cos_t, sin_t, cosi_t, sini_t = cos_q.T, sin_q.T, cos_i.T, sin_i.T

    ik_cols = w_in[:, :, _O_IK:_O_IW]
    w_std = jnp.concatenate([w_in[:, :, _O_AU:_O_BQ], w_in[:, :, _O_BK:_O_BV], ik_cols, ik_cols,
                             w_in[:, :, _O_GA:_O_END]], axis=2).astype(BF16)
    w_tr = jnp.swapaxes(jnp.concatenate([w_in[:, :, _O_BQ:_O_BK], w_in[:, :, _O_BV:_O_IK]], axis=2),
                        1, 2).astype(BF16)
    iw_t = jnp.swapaxes(w_in[:, :, _O_IW:_O_GA], 1, 2)
    w_trw = jnp.concatenate([iw_t, jnp.zeros_like(iw_t)], axis=1).astype(BF16)
    woa, wob, wout = w_oa.astype(BF16), w_ob.astype(BF16), w_out.astype(BF16)
    bias_full = jnp.repeat(jnp.swapaxes(a_bs, 1, 2), A_WIDTH // A_GROUPS, axis=2)
    kng = jnp.reshape(k_norm_g, (depth, 1, B_HEAD_DIM))
    qng = jnp.reshape(q_norm_g, (depth, B_HEAD_DIM, 1))
    ng = jnp.reshape(norm_g, (depth, 1, d))
    lng = jnp.reshape(a_ln_g, (depth, 1, A_WIDTH))
    lnb = jnp.reshape(a_ln_b, (depth, 1, A_WIDTH))

    x2d = jnp.reshape(x, (b * s, d))
    for l in range(depth):
        ma, sgb, k, ki = _proj_std(x2d, l, ng, w_std, gate_b, lng, lnb, a_ws, bias_full, kng, woa,
                                   cos2, sin2, cosi2, sini2, s)
        qT, vT, zbT, qiT, wiT = _proj_tr(jnp.reshape(x2d, (b, s, d)), l, ng, w_tr, w_trw, qng,
                                         cos_t, sin_t, cosi_t, sini_t)
        yb = _attn(qT, qiT, wiT, zbT, jnp.reshape(k, (b, s, B_WIDTH)), jnp.reshape(ki, (b, s, IDX_DIM)),
                   vT, topk)
        x2d = _out(x2d, l, jnp.reshape(yb, (b * s, B_WIDTH)), ma, sgb, wob, wout)
    return jnp.reshape(x2d, (b, s, d))
```

```python
import functools

import jax
import jax.numpy as jnp
from jax import lax
from jax.experimental import pallas as pl
from jax.experimental.pallas import tpu as pltpu

F32 = jnp.float32
BF16 = jnp.bfloat16

CHUNK = 64
EPS = 1e-6
ROPE_THETA = 10000.0

A_WIDTH = 512
A_GROUPS = 4
A_BLOCK = 128
B_HEADS = 4
B_HEAD_DIM = 128
B_WIDTH = B_HEADS * B_HEAD_DIM
IDX_HEADS = 8
IDX_DIM = 64
TOPK_MAX = 256

_O_AU, _O_AV, _O_AZ = 0, 512, 1024
_O_BQ, _O_BK, _O_BV, _O_BZ = 1536, 2048, 2560, 3072
_O_IQ, _O_IK, _O_IW = 3584, 4096, 4160
_O_GA, _O_GB, _O_END = 4168, 5192, 6216

LANES = 128
SUBLANES = 8
VMEM_LIMIT_BYTES = 52 * 1024 * 1024

QB = 256
TK = 256
TM_STD = 512
TM_STD_SUB = 256
TR_SUB = 2
TM_OUT = 512
NEG_BIG = -1e30
DIGIT_BITS = 8
DIGIT_BASE = 2 ** DIGIT_BITS
RANK_MAX = float(DIGIT_BASE ** 3 - 1)
COUNT_CHAINS = 4


def _rms_rows(x, g):
    ms = jnp.mean(x * x, axis=-1, keepdims=True)
    return x * lax.rsqrt(ms + EPS) * g


def _proj_std_body(x_ref, ng_ref, w_ref, gb_ref, lng_ref, lnb_ref, ws_ref, bias_ref, kng_ref,
                   woa_ref, cos_ref, sin_ref, cosi_ref, sini_ref,
                   ma_ref, sgb_ref, k_ref, ki_ref):
    tm = x_ref.shape[0]
    ci = lax.broadcasted_iota(jnp.int32, (A_BLOCK, A_BLOCK), 0) // CHUNK
    cj = lax.broadcasted_iota(jnp.int32, (A_BLOCK, A_BLOCK), 1) // CHUNK
    causal = cj <= ci
    wm = [jnp.where(causal, ws_ref[g], 0.0).astype(BF16) for g in range(A_GROUPS)]

    for r0 in range(0, tm, TM_STD_SUB):
        rows = slice(r0, r0 + TM_STD_SUB)
        h = _rms_rows(x_ref[rows, :], ng_ref[...]).astype(BF16)

        def proj(lo, hi, h=h):
            return jnp.dot(h, w_ref[:, lo:hi], preferred_element_type=F32)

        gv = jax.nn.gelu(proj(512, 1024))
        mu = jnp.mean(gv, axis=-1, keepdims=True)
        xc = gv - mu
        var = jnp.mean(xc * xc, axis=-1, keepdims=True)
        vn = (xc * lax.rsqrt(var + EPS) * lng_ref[...] + lnb_ref[...]).astype(BF16)

        row_blocks = []
        for r in range(TM_STD_SUB // A_BLOCK):
            cols = []
            for g in range(A_GROUPS):
                vb = vn[r * A_BLOCK:(r + 1) * A_BLOCK, g * LANES:(g + 1) * LANES]
                cols.append(jnp.dot(wm[g], vb, preferred_element_type=F32))
            row_blocks.append(jnp.concatenate(cols, axis=1) + bias_ref[...])
        mixed = jnp.concatenate(row_blocks, axis=0)

        y_a = jax.nn.gelu(proj(0, 512)) * mixed * jax.nn.silu(proj(1024, 1536))
        o_a = jnp.dot(y_a.astype(BF16), woa_ref[...], preferred_element_type=F32)
        ma_ref[rows, :] = (jax.nn.sigmoid(proj(2176, 3200) + gb_ref[0:1, :]) * o_a).astype(BF16)
        sgb_ref[rows, :] = jax.nn.sigmoid(proj(3200, 4224) + gb_ref[1:2, :]).astype(BF16)

        b_k = proj(1536, 2048)
        for hh in range(B_HEADS):
            kh = _rms_rows(b_k[:, hh * LANES:(hh + 1) * LANES], kng_ref[...])
            kh = kh * cos_ref[rows, :] + pltpu.roll(kh, B_HEAD_DIM // 2, 1) * sin_ref[rows, :]
            k_ref[rows, hh * LANES:(hh + 1) * LANES] = kh.astype(BF16)

        ik = proj(2048, 2176)
        ik = ik * cosi_ref[rows, :] + pltpu.roll(ik, IDX_DIM // 2, 1) * sini_ref[rows, :]
        ki_ref[rows, :] = ik[:, :IDX_DIM].astype(BF16)


def _layer_spec(arr, layer, grid_rank):
    zeros = (0,) * (arr.ndim - 1)
    if grid_rank == 1:
        return pl.BlockSpec((None,) + arr.shape[1:], lambda i: (layer,) + zeros)
    return pl.BlockSpec((None,) + arr.shape[1:], lambda bi, i: (layer,) + zeros)


def _proj_std(x2d, layer, ng, w_std, gb, lng, lnb, ws, bias_full, kng, woa, cos2, sin2, cosi2, sini2, seq):
    m, d = x2d.shape
    tm = TM_STD
    nt = seq // tm
    full = lambda arr: _layer_spec(arr, layer, 1)
    tab = lambda: pl.BlockSpec((tm, LANES), lambda i: (i % nt, 0))
    row = lambda width: pl.BlockSpec((tm, width), lambda i: (i, 0))
    return pl.pallas_call(
        _proj_std_body,
        grid=(m // tm,),
        in_specs=[row(d), full(ng), full(w_std), full(gb), full(lng),
                  full(lnb), full(ws), full(bias_full), full(kng),
                  full(woa), tab(), tab(), tab(), tab()],
        out_specs=[row(d), row(d), row(B_WIDTH), row(IDX_DIM)],
        out_shape=[jax.ShapeDtypeStruct((m, d), BF16), jax.ShapeDtypeStruct((m, d), BF16),
                   jax.ShapeDtypeStruct((m, B_WIDTH), BF16),
                   jax.ShapeDtypeStruct((m, IDX_DIM), BF16)],
        compiler_params=pltpu.CompilerParams(dimension_semantics=("arbitrary",),
                                             vmem_limit_bytes=VMEM_LIMIT_BYTES),
        name="proj_std",
    )(x2d, ng, w_std, gb, lng, lnb, ws, bias_full, kng, woa, cos2, sin2, cosi2, sini2)


def _proj_tr_body(x_ref, ng_ref, w_ref, ww_ref, qng_ref, cos_ref, sin_ref, cosi_ref, sini_ref,
                  q_ref, v_ref, zb_ref, qi_ref, wi_ref):
    att_scale = B_HEAD_DIM ** -0.5
    half = B_HEAD_DIM // 2
    ih = IDX_DIM // 2

    for j in range(TR_SUB):
        cols = slice(j * QB, (j + 1) * QB)
        h = _rms_rows(x_ref[0, cols, :], ng_ref[...]).astype(BF16)

        def proj_t(w, h=h):
            return lax.dot_general(w, h, (((1,), (1,)), ((), ())), preferred_element_type=F32)

        qt = proj_t(w_ref[0:512, :])
        for hh in range(B_HEADS):
            qh = qt[hh * B_HEAD_DIM:(hh + 1) * B_HEAD_DIM, :]
            ms = jnp.mean(qh * qh, axis=0, keepdims=True)
            qh = qh * lax.rsqrt(ms + EPS) * qng_ref[...]
            x1, x2 = qh[:half, :], qh[half:, :]
            c, s = cos_ref[:, cols], sin_ref[:, cols]
            base = hh * B_HEAD_DIM
            q_ref[0, j, base:base + half, :] = ((x1 * c - x2 * s) * att_scale).astype(BF16)
            q_ref[0, j, base + half:base + B_HEAD_DIM, :] = ((x1 * s + x2 * c) * att_scale).astype(BF16)

        v_ref[0, j] = proj_t(w_ref[512:1024, :]).astype(BF16)
        zb_ref[0, j] = jax.nn.silu(proj_t(w_ref[1024:1536, :])).astype(BF16)

        qit = proj_t(w_ref[1536:2048, :])
        for hh in range(IDX_HEADS):
            xh = qit[hh * IDX_DIM:(hh + 1) * IDX_DIM, :]
            x1, x2 = xh[:ih, :], xh[ih:, :]
            c, s = cosi_ref[:, cols], sini_ref[:, cols]
            base = hh * IDX_DIM
            qi_ref[0, j, base:base + ih, :] = (x1 * c - x2 * s).astype(BF16)
            qi_ref[0, j, base + ih:base + IDX_DIM, :] = (x1 * s + x2 * c).astype(BF16)

        wi_ref[0, j] = proj_t(ww_ref[...])


def _proj_tr(x, layer, ng, w_tr, w_trw, qng, cos_t, sin_t, cosi_t, sini_t):
    b, s, d = x.shape
    nq = s // QB
    tw = TR_SUB * QB
    full = lambda arr: _layer_spec(arr, layer, 2)
    tabt = lambda rows: pl.BlockSpec((rows, tw), lambda bi, i: (0, i))
    outt = lambda rows: pl.BlockSpec((1, TR_SUB, rows, QB), lambda bi, i: (bi, i, 0, 0))
    wrows = w_trw.shape[1]
    return pl.pallas_call(
        _proj_tr_body,
        grid=(b, nq // TR_SUB),
        in_specs=[pl.BlockSpec((1, tw, d), lambda bi, i: (bi, i, 0)), full(ng),
                  full(w_tr), full(w_trw), full(qng),
                  tabt(B_HEAD_DIM // 2), tabt(B_HEAD_DIM // 2), tabt(IDX_DIM // 2), tabt(IDX_DIM // 2)],
        out_specs=[outt(B_WIDTH), outt(B_WIDTH), outt(B_WIDTH), outt(IDX_HEADS * IDX_DIM), outt(wrows)],
        out_shape=[jax.ShapeDtypeStruct((b, nq, B_WIDTH, QB), BF16),
                   jax.ShapeDtypeStruct((b, nq, B_WIDTH, QB), BF16),
                   jax.ShapeDtypeStruct((b, nq, B_WIDTH, QB), BF16),
                   jax.ShapeDtypeStruct((b, nq, IDX_HEADS * IDX_DIM, QB), BF16),
                   jax.ShapeDtypeStruct((b, nq, wrows, QB), F32)],
        compiler_params=pltpu.CompilerParams(dimension_semantics=("arbitrary", "arbitrary"),
                                             vmem_limit_bytes=VMEM_LIMIT_BYTES),
        name="proj_tr",
    )(x, ng, w_tr, w_trw, qng, cos_t, sin_t, cosi_t, sini_t)


def _slab_reduce(x, op):
    parts = [x[r:r + SUBLANES, :] for r in range(0, x.shape[0], SUBLANES)]
    while len(parts) > 1:
        parts = [op(parts[i], parts[i + 1]) for i in range(0, len(parts), 2)]
    return parts[0]


def _attn_body(q_ref, qi_ref, wi_ref, zb_ref, k_ref, ki_ref, v_ref, y_ref,
               s_scr, r_scr, d_scr, att_scr, acc_scr, *, topk):
    n = pl.program_id(1)
    ntiles = n + 1
    idx_scale = (IDX_DIM ** -0.5) * (IDX_HEADS ** -0.5)
    rows_per_tile = TK // SUBLANES

    def score_tile(t, lo8, hi8, diagonal):
        ki_t = ki_ref[0, pl.ds(pl.multiple_of(t * TK, TK), TK), :]
        acc = jnp.zeros((TK, QB), F32)
        for hh in range(IDX_HEADS):
            logit = jnp.dot(ki_t, qi_ref[0, 0, hh * IDX_DIM:(hh + 1) * IDX_DIM, :],
                            preferred_element_type=F32)
            acc = acc + wi_ref[0, 0, hh:hh + 1, :] * jnp.maximum(logit, 0.0)
        sc = acc * idx_scale
        sc_for_min = sc
        if diagonal:
            kc = lax.broadcasted_iota(jnp.int32, (TK, QB), 0) // CHUNK
            qc = lax.broadcasted_iota(jnp.int32, (TK, QB), 1) // CHUNK
            admissible = kc <= qc
            sc_for_min = jnp.where(admissible, sc, jnp.inf)
            sc = jnp.where(admissible, sc, -jnp.inf)
        s_scr[t] = sc
        return (jnp.minimum(lo8, _slab_reduce(sc_for_min, jnp.minimum)),
                jnp.maximum(hi8, _slab_reduce(sc, jnp.maximum)))

    def tiles_in_pairs(count, tile_fn, carry):
        def pair(p, c):
            return tile_fn(2 * p + 1, tile_fn(2 * p, c))
        carry = lax.fori_loop(0, lax.shift_right_logical(count, 1), pair, carry)
        return lax.cond((count & 1) == 1, lambda c: tile_fn(count - 1, c), lambda c: c, carry)

    lo8, hi8 = tiles_in_pairs(
        n, lambda t, c: score_tile(t, c[0], c[1], False),
        (jnp.full((SUBLANES, QB), jnp.inf, F32), jnp.full((SUBLANES, QB), -jnp.inf, F32)))
    lo8, hi8 = score_tile(n, lo8, hi8, True)
    s_lo = jnp.min(lo8, axis=0, keepdims=True)
    s_hi = jnp.max(hi8, axis=0, keepdims=True)

    span = s_hi - s_lo
    rank_scale = jnp.where(span > 0.0, RANK_MAX / jnp.where(span > 0.0, span, 1.0), 0.0)

    def rank_tile(t, carry):
        r = jnp.minimum(jnp.floor((s_scr[t] - s_lo) * rank_scale), RANK_MAX)
        r = jnp.maximum(r, -1.0)
        r_scr[t] = r
        d1 = jnp.floor(r * (1.0 / DIGIT_BASE ** 2))
        rem = r - d1 * float(DIGIT_BASE ** 2)
        d2 = jnp.floor(rem * (1.0 / DIGIT_BASE))
        d_scr[0, t] = d1.astype(BF16)
        d_scr[1, t] = d2.astype(BF16)
        d_scr[2, t] = (rem - d2 * float(DIGIT_BASE)).astype(BF16)
        return carry

    lax.fori_loop(0, ntiles, rank_tile, 0)

    @pl.when((ntiles & 1) == 1)
    def _():
        for plane in range(3):
            d_scr[plane, ntiles] = jnp.full((TK, QB), -1.0, BF16)

    npairs = lax.shift_right_logical(ntiles + 1, 1)
    packed_rows = 2 * SUBLANES
    slabs_per_tile = TK // packed_rows
    one_bf, zero_bf = jnp.ones((), BF16), jnp.zeros((), BF16)

    def count_ge(plane, cand):
        cand_b = jnp.broadcast_to(cand.astype(BF16), (packed_rows, QB))

        def pair_body(p, accs):
            accs = list(accs)
            for half in range(2):
                for r in range(slabs_per_tile):
                    sl = d_scr[plane, 2 * p + half, r * packed_rows:(r + 1) * packed_rows, :]
                    c = r % COUNT_CHAINS
                    accs[c] = accs[c] + jnp.where(sl >= cand_b, one_bf, zero_bf)
            return tuple(accs)

        accs = lax.fori_loop(0, npairs, pair_body,
                             tuple(jnp.zeros((packed_rows, QB), BF16) for _ in range(COUNT_CHAINS)))
        total = functools.reduce(jnp.add, [a.astype(F32) for a in accs])
        return jnp.sum(total, axis=0, keepdims=True)

    def digit_search(plane, cnt_at):
        def bit_body(i, carry):
            prefix, cnt_at, step = carry
            trial = prefix + step
            cnt = count_ge(plane, trial)
            ok = cnt >= float(topk)
            return jnp.where(ok, trial, prefix), jnp.where(ok, cnt, cnt_at), step * 0.5
        digit, cnt_at, _ = lax.fori_loop(
            0, DIGIT_BITS, bit_body,
            (jnp.zeros((1, QB), F32), cnt_at, jnp.full((1, QB), DIGIT_BASE / 2.0, F32)))
        return digit, cnt_at

    def restrict_plane(plane, digit):
        digit_b = jnp.broadcast_to(digit.astype(BF16), (TK, QB))
        above, below = jnp.full((), float(DIGIT_BASE), BF16), jnp.full((), -1.0, BF16)

        def tile_body(t, carry):
            prev = d_scr[plane, t]
            d_scr[plane + 1, t] = jnp.where(prev > digit_b, above,
                                            jnp.where(prev < digit_b, below, d_scr[plane + 1, t]))
            return carry

        lax.fori_loop(0, ntiles, tile_body, 0)

    q_chunk = (n * QB + lax.broadcasted_iota(jnp.int32, (1, QB), 1)) // CHUNK
    n_admissible = ((q_chunk + 1) * CHUNK).astype(F32)
    digit1, cnt_at = digit_search(0, n_admissible)
    restrict_plane(0, digit1)
    digit2, cnt_at = digit_search(1, cnt_at)
    restrict_plane(1, digit2)
    digit3, cnt_at = digit_search(2, cnt_at)
    thr = (digit1 * float(DIGIT_BASE) + digit2) * float(DIGIT_BASE) + digit3

    excess0 = jnp.maximum(cnt_at - float(topk), 0.0)

    @pl.when(jnp.max(excess0) > 0.0)
    def _():
        thr_b = jnp.broadcast_to(thr, (SUBLANES, QB))
        later_or_same = (lax.broadcasted_iota(jnp.int32, (TK, TK), 1)
                         >= lax.broadcasted_iota(jnp.int32, (TK, TK), 0))
        suffix_ones = jnp.where(later_or_same, 1.0, 0.0).astype(BF16)

        def slab(r):
            return slice(r * SUBLANES, (r + 1) * SUBLANES)

        def drop_group(excess):
            def min_tile(t, ms):
                ms = list(ms)
                for r in range(rows_per_tile):
                    selected = r_scr[t, slab(r), :] >= thr_b
                    c = r % COUNT_CHAINS
                    ms[c] = jnp.minimum(ms[c], jnp.where(selected, s_scr[t, slab(r), :], jnp.inf))
                return tuple(ms)

            ms = lax.fori_loop(0, ntiles, min_tile,
                               tuple(jnp.full((SUBLANES, QB), jnp.inf, F32) for _ in range(COUNT_CHAINS)))
            m8 = functools.reduce(jnp.minimum, ms)
            m_row = jnp.min(m8, axis=0, keepdims=True)

            def drop_tile(i, later):
                t = ntiles - 1 - i
                r = r_scr[t]
                in_group = (r >= thr) & (s_scr[t] == m_row)
                member = jnp.where(in_group, 1.0, 0.0).astype(BF16)
                suffix = jnp.dot(suffix_ones, member, preferred_element_type=F32) + later
                r_scr[t] = jnp.where(in_group & (suffix <= excess), -1.0, r)
                return suffix[0:1, :]

            group_size = lax.fori_loop(0, ntiles, drop_tile, jnp.zeros((1, QB), F32))
            return excess - jnp.minimum(excess, group_size)

        lax.while_loop(lambda e: jnp.max(e) > 0.0, drop_group, excess0)

    slab_reduce = _slab_reduce

    def logit_tile(t, m8):
        sel = r_scr[t] >= thr
        krows = pl.ds(pl.multiple_of(t * TK, TK), TK)
        new_m8 = []
        for hh in range(B_HEADS):
            hs = slice(hh * B_HEAD_DIM, (hh + 1) * B_HEAD_DIM)
            att = jnp.dot(k_ref[0, krows, hs], q_ref[0, 0, hs, :], preferred_element_type=F32)
            att = jnp.where(sel, att, NEG_BIG)
            att_scr[t * B_HEADS + hh] = att
            new_m8.append(jnp.maximum(m8[hh], slab_reduce(att, jnp.maximum)))
        return tuple(new_m8)

    m8 = tiles_in_pairs(ntiles, logit_tile,
                        tuple(jnp.full((SUBLANES, QB), NEG_BIG, F32) for _ in range(B_HEADS)))
    m_row = [jnp.max(m, axis=0, keepdims=True) for m in m8]

    acc_scr[...] = jnp.zeros_like(acc_scr)

    def value_tile(t, l8):
        new_l8 = []
        for hh in range(B_HEADS):
            hs = slice(hh * B_HEAD_DIM, (hh + 1) * B_HEAD_DIM)
            p = jnp.exp(att_scr[t * B_HEADS + hh] - m_row[hh])
            new_l8.append(l8[hh] + slab_reduce(p, jnp.add))
            acc_scr[hs, :] += jnp.dot(v_ref[0, t, hs, :], p.astype(BF16), preferred_element_type=F32)
        return tuple(new_l8)

    l8 = tiles_in_pairs(ntiles, value_tile,
                        tuple(jnp.zeros((SUBLANES, QB), F32) for _ in range(B_HEADS)))

    parts = []
    for hh in range(B_HEADS):
        hs = slice(hh * B_HEAD_DIM, (hh + 1) * B_HEAD_DIM)
        l_row = jnp.sum(l8[hh], axis=0, keepdims=True)
        parts.append(acc_scr[hs, :] / l_row * zb_ref[0, 0, hs, :].astype(F32))
    y_ref[0] = jnp.concatenate(parts, axis=0).T.astype(BF16)


def _attn(qT, qiT, wiT, zbT, k, ki, vT, topk):
    b, nq = qT.shape[0], qT.shape[1]
    s = k.shape[1]
    blk = lambda rows: pl.BlockSpec((1, 1, rows, QB), lambda bi, i: (bi, i, 0, 0))
    return pl.pallas_call(
        functools.partial(_attn_body, topk=topk),
        grid=(b, nq),
        in_specs=[blk(B_WIDTH), blk(IDX_HEADS * IDX_DIM), blk(wiT.shape[2]), blk(B_WIDTH),
                  pl.BlockSpec((1, s, B_WIDTH), lambda bi, i: (bi, 0, 0), pipeline_mode=pl.Buffered(1)),
                  pl.BlockSpec((1, s, IDX_DIM), lambda bi, i: (bi, 0, 0), pipeline_mode=pl.Buffered(1)),
                  pl.BlockSpec((1, nq, B_WIDTH, QB), lambda bi, i: (bi, 0, 0, 0),
                               pipeline_mode=pl.Buffered(1))],
        out_specs=pl.BlockSpec((1, QB, B_WIDTH), lambda bi, i: (bi, i, 0)),
        out_shape=jax.ShapeDtypeStruct((b, s, B_WIDTH), BF16),
        scratch_shapes=[pltpu.VMEM((s // TK, TK, QB), F32),
                        pltpu.VMEM((s // TK, TK, QB), F32),
                        pltpu.VMEM((3, s // TK, TK, QB), BF16),
                        pltpu.VMEM((s // TK * B_HEADS, TK, QB), F32),
                        pltpu.VMEM((B_WIDTH, QB), F32)],
        compiler_params=pltpu.CompilerParams(dimension_semantics=("arbitrary", "arbitrary"),
                                             vmem_limit_bytes=VMEM_LIMIT_BYTES),
        name="attn",
    )(qT, qiT, wiT, zbT, k, ki, vT)


def _out_body(x_ref, yb_ref, ma_ref, sgb_ref, wob_ref, wout_ref, o_ref):
    o_b = jnp.dot(yb_ref[...], wob_ref[...], preferred_element_type=F32)
    merged = ma_ref[...].astype(F32) + sgb_ref[...].astype(F32) * o_b
    o_ref[...] = x_ref[...] + jnp.dot(merged.astype(BF16), wout_ref[...], preferred_element_type=F32)


def _out(x2d, layer, yb, ma, sgb, wob, wout):
    m, d = x2d.shape
    tm = TM_OUT
    row = lambda width: pl.BlockSpec((tm, width), lambda i: (i, 0))
    full = lambda arr: _layer_spec(arr, layer, 1)
    return pl.pallas_call(
        _out_body,
        grid=(m // tm,),
        in_specs=[row(d), row(B_WIDTH), row(d), row(d), full(wob), full(wout)],
        out_specs=row(d),
        out_shape=jax.ShapeDtypeStruct((m, d), F32),
        compiler_params=pltpu.CompilerParams(dimension_semantics=("arbitrary",),
                                             vmem_limit_bytes=VMEM_LIMIT_BYTES),
        name="out_proj",
    )(x2d, yb, ma, sgb, wob, wout)


def _rope_tables(s, dim):
    pos = jnp.arange(s, dtype=F32)
    inv = ROPE_THETA ** (-jnp.arange(0, dim, 2, dtype=F32) / dim)
    ang = pos[:, None] * inv[None, :]
    return jnp.cos(ang), jnp.sin(ang)


def kernel(x, norm_g, w_in, gate_b, a_ln_g, a_ln_b, a_ws, a_bs, q_norm_g, k_norm_g, w_oa, w_ob, w_out):
    b, s, d = x.shape
    depth = w_in.shape[0]
    topk = min(TOPK_MAX, s // 4)
    assert s % TM_STD == 0 and s % QB == 0 and (b * s) % TM_OUT == 0 and QB == TK
    assert w_in.shape[2] == _O_END and topk <= TK

    cos_q, sin_q = _rope_tables(s, B_HEAD_DIM)
    cos_i, sin_i = _rope_tables(s, IDX_DIM)
    cos2 = jnp.concatenate([cos_q, cos_q], axis=1)
    sin2 = jnp.concatenate([-sin_q, sin_q], axis=1)
    cosi2 = jnp.concatenate([cos_i, cos_i, cos_i, cos_i], axis=1)
    sini2 = jnp.concatenate([-sin_i, sin_i, -sin_i, sin_i], axis=1)
    cos_t, sin_t, cosi_t, sini_t = cos_q.T, sin_q.T, cos_i.T, sin_i.T

    ik_cols = w_in[:, :, _O_IK:_O_IW]
    w_std = jnp.concatenate([w_in[:, :, _O_AU:_O_BQ], w_in[:, :, _O_BK:_O_BV], ik_cols, ik_cols,
                             w_in[:, :, _O_GA:_O_END]], axis=2).astype(BF16)
    w_tr = jnp.swapaxes(jnp.concatenate([w_in[:, :, _O_BQ:_O_BK], w_in[:, :, _O_BV:_O_IK]], axis=2),
                        1, 2).astype(BF16)
    iw_t = jnp.swapaxes(w_in[:, :, _O_IW:_O_GA], 1, 2)
    w_trw = jnp.concatenate([iw_t, jnp.zeros_like(iw_t)], axis=1).astype(BF16)
    woa, wob, wout = w_oa.astype(BF16), w_ob.astype(BF16), w_out.astype(BF16)
    bias_full = jnp.repeat(jnp.swapaxes(a_bs, 1, 2), A_WIDTH // A_GROUPS, axis=2)
    kng = jnp.reshape(k_norm_g, (depth, 1, B_HEAD_DIM))
    qng = jnp.reshape(q_norm_g, (depth, B_HEAD_DIM, 1))
    ng = jnp.reshape(norm_g, (depth, 1, d))
    lng = jnp.reshape(a_ln_g, (depth, 1, A_WIDTH))
    lnb = jnp.reshape(a_ln_b, (depth, 1, A_WIDTH))

    x2d = jnp.reshape(x, (b * s, d))
    for l in range(depth):
        ma, sgb, k, ki = _proj_std(x2d, l, ng, w_std, gate_b, lng, lnb, a_ws, bias_full, kng, woa,
                                   cos2, sin2, cosi2, sini2, s)
        qT, vT, zbT, qiT, wiT = _proj_tr(jnp.reshape(x2d, (b, s, d)), l, ng, w_tr, w_trw, qng,
                                         cos_t, sin_t, cosi_t, sini_t)
        yb = _attn(qT, qiT, wiT, zbT, jnp.reshape(k, (b, s, B_WIDTH)), jnp.reshape(ki, (b, s, IDX_DIM)),
                   vT, topk)
        x2d = _out(x2d, l, jnp.reshape(yb, (b * s, B_WIDTH)), ma, sgb, wob, wout)
    return jnp.reshape(x2d, (b, s, d))
```

```python
import functools

import jax
import jax.numpy as jnp
from jax import lax
from jax.experimental import pallas as pl
from jax.experimental.pallas import tpu as pltpu

F32 = jnp.float32
BF16 = jnp.bfloat16

CHUNK = 64
EPS = 1e-6
ROPE_THETA = 10000.0

A_WIDTH = 512
A_GROUPS = 4
A_BLOCK = 128
B_HEADS = 4
B_HEAD_DIM = 128
B_WIDTH = B_HEADS * B_HEAD_DIM
IDX_HEADS = 8
IDX_DIM = 64
TOPK_MAX = 256

_O_AU, _O_AV, _O_AZ = 0, 512, 1024
_O_BQ, _O_BK, _O_BV, _O_BZ = 1536, 2048, 2560, 3072
_O_IQ, _O_IK, _O_IW = 3584, 4096, 4160
_O_GA, _O_GB, _O_END = 4168, 5192, 6216

LANES = 128
SUBLANES = 8
VMEM_LIMIT_BYTES = 52 * 1024 * 1024

QB = 256
TK = 256
TM_STD = 512
TM_STD_SUB = 256
TR_SUB = 2
TM_OUT = 512
NEG_BIG = -1e30
DIGIT_BITS = 8
DIGIT_BASE = 2 ** DIGIT_BITS
RANK_MAX = float(DIGIT_BASE ** 3 - 1)
COUNT_CHAINS = 4


def _rms_rows(x, g):
    ms = jnp.mean(x * x, axis=-1, keepdims=True)
    return x * lax.rsqrt(ms + EPS) * g


def _proj_std_body(x_ref, ng_ref, w_ref, gb_ref, lng_ref, lnb_ref, ws_ref, bias_ref, kng_ref,
                   woa_ref, cos_ref, sin_ref, cosi_ref, sini_ref,
                   ma_ref, sgb_ref, k_ref, ki_ref):
    tm = x_ref.shape[0]
    ci = lax.broadcasted_iota(jnp.int32, (A_BLOCK, A_BLOCK), 0) // CHUNK
    cj = lax.broadcasted_iota(jnp.int32, (A_BLOCK, A_BLOCK), 1) // CHUNK
    causal = cj <= ci
    wm = [jnp.where(causal, ws_ref[g], 0.0).astype(BF16) for g in range(A_GROUPS)]

    for r0 in range(0, tm, TM_STD_SUB):
        rows = slice(r0, r0 + TM_STD_SUB)
        h = _rms_rows(x_ref[rows, :], ng_ref[...]).astype(BF16)

        def proj(lo, hi, h=h):
            return lax.dot_general(h, w_ref[lo:hi, :], (((1,), (1,)), ((), ())), preferred_element_type=F32)

        gv = jax.nn.gelu(proj(512, 1024))
        mu = jnp.mean(gv, axis=-1, keepdims=True)
        xc = gv - mu
        var = jnp.mean(xc * xc, axis=-1, keepdims=True)
        vn = (xc * lax.rsqrt(var + EPS) * lng_ref[...] + lnb_ref[...]).astype(BF16)

        row_blocks = []
        for r in range(TM_STD_SUB // A_BLOCK):
            cols = []
            for g in range(A_GROUPS):
                vb = vn[r * A_BLOCK:(r + 1) * A_BLOCK, g * LANES:(g + 1) * LANES]
                cols.append(jnp.dot(wm[g], vb, preferred_element_type=F32))
            row_blocks.append(jnp.concatenate(cols, axis=1) + bias_ref[...])
        mixed = jnp.concatenate(row_blocks, axis=0)

        y_a = jax.nn.gelu(proj(0, 512)) * mixed * jax.nn.silu(proj(1024, 1536))
        o_a = jnp.dot(y_a.astype(BF16), woa_ref[...], preferred_element_type=F32)
        ma_ref[rows, :] = (jax.nn.sigmoid(proj(2176, 3200) + gb_ref[0:1, :]) * o_a).astype(BF16)
        sgb_ref[rows, :] = jax.nn.sigmoid(proj(3200, 4224) + gb_ref[1:2, :]).astype(BF16)

        b_k = proj(1536, 2048)
        for hh in range(B_HEADS):
            kh = _rms_rows(b_k[:, hh * LANES:(hh + 1) * LANES], kng_ref[...])
            kh = kh * cos_ref[rows, :] + pltpu.roll(kh, B_HEAD_DIM // 2, 1) * sin_ref[rows, :]
            k_ref[rows, hh * LANES:(hh + 1) * LANES] = kh.astype(BF16)

        ik = proj(2048, 2176)
        ik = ik * cosi_ref[rows, :] + pltpu.roll(ik, IDX_DIM // 2, 1) * sini_ref[rows, :]
        ki_ref[rows, :] = ik[:, :IDX_DIM].astype(BF16)


def _layer_spec(arr, layer, grid_rank):
    zeros = (0,) * (arr.ndim - 1)
    if grid_rank == 1:
        return pl.BlockSpec((None,) + arr.shape[1:], lambda i: (layer,) + zeros)
    return pl.BlockSpec((None,) + arr.shape[1:], lambda bi, i: (layer,) + zeros)


def _proj_std(x2d, layer, ng, w_std, gb, lng, lnb, ws, bias_full, kng, woa, cos2, sin2, cosi2, sini2, seq):
    m, d = x2d.shape
    tm = TM_STD
    nt = seq // tm
    full = lambda arr: _layer_spec(arr, layer, 1)
    tab = lambda: pl.BlockSpec((tm, LANES), lambda i: (i % nt, 0))
    row = lambda width: pl.BlockSpec((tm, width), lambda i: (i, 0))
    return pl.pallas_call(
        _proj_std_body,
        grid=(m // tm,),
        in_specs=[row(d), full(ng), full(w_std), full(gb), full(lng),
                  full(lnb), full(ws), full(bias_full), full(kng),
                  full(woa), tab(), tab(), tab(), tab()],
        out_specs=[row(d), row(d), row(B_WIDTH), row(IDX_DIM)],
        out_shape=[jax.ShapeDtypeStruct((m, d), BF16), jax.ShapeDtypeStruct((m, d), BF16),
                   jax.ShapeDtypeStruct((m, B_WIDTH), BF16),
                   jax.ShapeDtypeStruct((m, IDX_DIM), BF16)],
        compiler_params=pltpu.CompilerParams(dimension_semantics=("arbitrary",),
                                             vmem_limit_bytes=VMEM_LIMIT_BYTES),
        name="proj_std",
    )(x2d, ng, w_std, gb, lng, lnb, ws, bias_full, kng, woa, cos2, sin2, cosi2, sini2)


def _proj_tr_body(x_ref, ng_ref, w_ref, ww_ref, qng_ref, cos_ref, sin_ref, cosi_ref, sini_ref,
                  q_ref, v_ref, zb_ref, qi_ref, wi_ref):
    att_scale = B_HEAD_DIM ** -0.5
    half = B_HEAD_DIM // 2
    ih = IDX_DIM // 2

    for j in range(TR_SUB):
        cols = slice(j * QB, (j + 1) * QB)
        h = _rms_rows(x_ref[0, cols, :], ng_ref[...]).astype(BF16)

        def proj_t(w, h=h):
            return lax.dot_general(w, h, (((1,), (1,)), ((), ())), preferred_element_type=F32)

        qt = proj_t(w_ref[0:512, :])
        for hh in range(B_HEADS):
            qh = qt[hh * B_HEAD_DIM:(hh + 1) * B_HEAD_DIM, :]
            ms = jnp.mean(qh * qh, axis=0, keepdims=True)
            qh = qh * lax.rsqrt(ms + EPS) * qng_ref[...]
            x1, x2 = qh[:half, :], qh[half:, :]
            c, s = cos_ref[:, cols], sin_ref[:, cols]
            base = hh * B_HEAD_DIM
            q_ref[0, j, base:base + half, :] = ((x1 * c - x2 * s) * att_scale).astype(BF16)
            q_ref[0, j, base + half:base + B_HEAD_DIM, :] = ((x1 * s + x2 * c) * att_scale).astype(BF16)

        v_ref[0, j] = proj_t(w_ref[512:1024, :]).astype(BF16)
        zb_ref[0, j] = jax.nn.silu(proj_t(w_ref[1024:1536, :])).astype(BF16)

        qit = proj_t(w_ref[1536:2048, :])
        for hh in range(IDX_HEADS):
            xh = qit[hh * IDX_DIM:(hh + 1) * IDX_DIM, :]
            x1, x2 = xh[:ih, :], xh[ih:, :]
            c, s = cosi_ref[:, cols], sini_ref[:, cols]
            base = hh * IDX_DIM
            qi_ref[0, j, base:base + ih, :] = (x1 * c - x2 * s).astype(BF16)
            qi_ref[0, j, base + ih:base + IDX_DIM, :] = (x1 * s + x2 * c).astype(BF16)

        wi_ref[0, j] = proj_t(ww_ref[...])


def _proj_tr(x, layer, ng, w_tr, w_trw, qng, cos_t, sin_t, cosi_t, sini_t):
    b, s, d = x.shape
    nq = s // QB
    tw = TR_SUB * QB
    full = lambda arr: _layer_spec(arr, layer, 2)
    tabt = lambda rows: pl.BlockSpec((rows, tw), lambda bi, i: (0, i))
    outt = lambda rows: pl.BlockSpec((1, TR_SUB, rows, QB), lambda bi, i: (bi, i, 0, 0))
    wrows = w_trw.shape[1]
    return pl.pallas_call(
        _proj_tr_body,
        grid=(b, nq // TR_SUB),
        in_specs=[pl.BlockSpec((1, tw, d), lambda bi, i: (bi, i, 0)), full(ng),
                  full(w_tr), full(w_trw), full(qng),
                  tabt(B_HEAD_DIM // 2), tabt(B_HEAD_DIM // 2), tabt(IDX_DIM // 2), tabt(IDX_DIM // 2)],
        out_specs=[outt(B_WIDTH), outt(B_WIDTH), outt(B_WIDTH), outt(IDX_HEADS * IDX_DIM), outt(wrows)],
        out_shape=[jax.ShapeDtypeStruct((b, nq, B_WIDTH, QB), BF16),
                   jax.ShapeDtypeStruct((b, nq, B_WIDTH, QB), BF16),
                   jax.ShapeDtypeStruct((b, nq, B_WIDTH, QB), BF16),
                   jax.ShapeDtypeStruct((b, nq, IDX_HEADS * IDX_DIM, QB), BF16),
                   jax.ShapeDtypeStruct((b, nq, wrows, QB), F32)],
        compiler_params=pltpu.CompilerParams(dimension_semantics=("arbitrary", "arbitrary"),
                                             vmem_limit_bytes=VMEM_LIMIT_BYTES),
        name="proj_tr",
    )(x, ng, w_tr, w_trw, qng, cos_t, sin_t, cosi_t, sini_t)


def _slab_reduce(x, op):
    parts = [x[r:r + SUBLANES, :] for r in range(0, x.shape[0], SUBLANES)]
    while len(parts) > 1:
        parts = [op(parts[i], parts[i + 1]) for i in range(0, len(parts), 2)]
    return parts[0]


def _attn_body(q_ref, qi_ref, wi_ref, zb_ref, k_ref, ki_ref, v_ref, y_ref,
               s_scr, r_scr, d_scr, att_scr, acc_scr, *, topk):
    n = pl.program_id(1)
    ntiles = n + 1
    idx_scale = (IDX_DIM ** -0.5) * (IDX_HEADS ** -0.5)
    rows_per_tile = TK // SUBLANES

    def score_tile(t, lo8, hi8, diagonal):
        ki_t = ki_ref[0, pl.ds(pl.multiple_of(t * TK, TK), TK), :]
        acc = jnp.zeros((TK, QB), F32)
        for hh in range(IDX_HEADS):
            logit = jnp.dot(ki_t, qi_ref[0, 0, hh * IDX_DIM:(hh + 1) * IDX_DIM, :],
                            preferred_element_type=F32)
            acc = acc + wi_ref[0, 0, hh:hh + 1, :] * jnp.maximum(logit, 0.0)
        sc = acc * idx_scale
        sc_for_min = sc
        if diagonal:
            kc = lax.broadcasted_iota(jnp.int32, (TK, QB), 0) // CHUNK
            qc = lax.broadcasted_iota(jnp.int32, (TK, QB), 1) // CHUNK
            admissible = kc <= qc
            sc_for_min = jnp.where(admissible, sc, jnp.inf)
            sc = jnp.where(admissible, sc, -jnp.inf)
        s_scr[t] = sc
        return (jnp.minimum(lo8, _slab_reduce(sc_for_min, jnp.minimum)),
                jnp.maximum(hi8, _slab_reduce(sc, jnp.maximum)))

    def tiles_in_pairs(count, tile_fn, carry):
        def pair(p, c):
            return tile_fn(2 * p + 1, tile_fn(2 * p, c))
        carry = lax.fori_loop(0, lax.shift_right_logical(count, 1), pair, carry)
        return lax.cond((count & 1) == 1, lambda c: tile_fn(count - 1, c), lambda c: c, carry)

    lo8, hi8 = tiles_in_pairs(
        n, lambda t, c: score_tile(t, c[0], c[1], False),
        (jnp.full((SUBLANES, QB), jnp.inf, F32), jnp.full((SUBLANES, QB), -jnp.inf, F32)))
    lo8, hi8 = score_tile(n, lo8, hi8, True)
    s_lo = jnp.min(lo8, axis=0, keepdims=True)
    s_hi = jnp.max(hi8, axis=0, keepdims=True)

    span = s_hi - s_lo
    rank_scale = jnp.where(span > 0.0, RANK_MAX / jnp.where(span > 0.0, span, 1.0), 0.0)

    def rank_tile(t, carry):
        r = jnp.minimum(jnp.floor((s_scr[t] - s_lo) * rank_scale), RANK_MAX)
        r = jnp.maximum(r, -1.0)
        r_scr[t] = r
        d1 = jnp.floor(r * (1.0 / DIGIT_BASE ** 2))
        rem = r - d1 * float(DIGIT_BASE ** 2)
        d2 = jnp.floor(rem * (1.0 / DIGIT_BASE))
        d_scr[0, t] = d1.astype(BF16)
        d_scr[1, t] = d2.astype(BF16)
        d_scr[2, t] = (rem - d2 * float(DIGIT_BASE)).astype(BF16)
        return carry

    lax.fori_loop(0, ntiles, rank_tile, 0)

    @pl.when((ntiles & 1) == 1)
    def _():
        for plane in range(3):
            d_scr[plane, ntiles] = jnp.full((TK, QB), -1.0, BF16)

    npairs = lax.shift_right_logical(ntiles + 1, 1)
    packed_rows = 2 * SUBLANES
    slabs_per_tile = TK // packed_rows
    one_bf, zero_bf = jnp.ones((), BF16), jnp.zeros((), BF16)

    def count_ge(plane, cand):
        cand_b = jnp.broadcast_to(cand.astype(BF16), (packed_rows, QB))

        def pair_body(p, accs):
            accs = list(accs)
            for half in range(2):
                for r in range(slabs_per_tile):
                    sl = d_scr[plane, 2 * p + half, r * packed_rows:(r + 1) * packed_rows, :]
                    c = r % COUNT_CHAINS
                    accs[c] = accs[c] + jnp.where(sl >= cand_b, one_bf, zero_bf)
            return tuple(accs)

        accs = lax.fori_loop(0, npairs, pair_body,
                             tuple(jnp.zeros((packed_rows, QB), BF16) for _ in range(COUNT_CHAINS)))
        total = functools.reduce(jnp.add, [a.astype(F32) for a in accs])
        return jnp.sum(total, axis=0, keepdims=True)

    def digit_search(plane, cnt_at):
        def bit_body(i, carry):
            prefix, cnt_at, step = carry
            trial = prefix + step
            cnt = count_ge(plane, trial)
            ok = cnt >= float(topk)
            return jnp.where(ok, trial, prefix), jnp.where(ok, cnt, cnt_at), step * 0.5
        digit, cnt_at, _ = lax.fori_loop(
            0, DIGIT_BITS, bit_body,
            (jnp.zeros((1, QB), F32), cnt_at, jnp.full((1, QB), DIGIT_BASE / 2.0, F32)))
        return digit, cnt_at

    def restrict_plane(plane, digit):
        digit_b = jnp.broadcast_to(digit.astype(BF16), (TK, QB))
        above, below = jnp.full((), float(DIGIT_BASE), BF16), jnp.full((), -1.0, BF16)

        def tile_body(t, carry):
            prev = d_scr[plane, t]
            d_scr[plane + 1, t] = jnp.where(prev > digit_b, above,
                                            jnp.where(prev < digit_b, below, d_scr[plane + 1, t]))
            return carry

        lax.fori_loop(0, ntiles, tile_body, 0)

    q_chunk = (n * QB + lax.broadcasted_iota(jnp.int32, (1, QB), 1)) // CHUNK
    n_admissible = ((q_chunk + 1) * CHUNK).astype(F32)
    digit1, cnt_at = digit_search(0, n_admissible)
    restrict_plane(0, digit1)
    digit2, cnt_at = digit_search(1, cnt_at)
    restrict_plane(1, digit2)
    digit3, cnt_at = digit_search(2, cnt_at)
    thr = (digit1 * float(DIGIT_BASE) + digit2) * float(DIGIT_BASE) + digit3

    excess0 = jnp.maximum(cnt_at - float(topk), 0.0)

    @pl.when(jnp.max(excess0) > 0.0)
    def _():
        thr_b = jnp.broadcast_to(thr, (SUBLANES, QB))
        later_or_same = (lax.broadcasted_iota(jnp.int32, (TK, TK), 1)
                         >= lax.broadcasted_iota(jnp.int32, (TK, TK), 0))
        suffix_ones = jnp.where(later_or_same, 1.0, 0.0).astype(BF16)

        def slab(r):
            return slice(r * SUBLANES, (r + 1) * SUBLANES)

        def drop_group(excess):
            def min_tile(t, ms):
                ms = list(ms)
                for r in range(rows_per_tile):
                    selected = r_scr[t, slab(r), :] >= thr_b
                    c = r % COUNT_CHAINS
                    ms[c] = jnp.minimum(ms[c], jnp.where(selected, s_scr[t, slab(r), :], jnp.inf))
                return tuple(ms)

            ms = lax.fori_loop(0, ntiles, min_tile,
                               tuple(jnp.full((SUBLANES, QB), jnp.inf, F32) for _ in range(COUNT_CHAINS)))
            m8 = functools.reduce(jnp.minimum, ms)
            m_row = jnp.min(m8, axis=0, keepdims=True)

            def drop_tile(i, later):
                t = ntiles - 1 - i
                r = r_scr[t]
                in_group = (r >= thr) & (s_scr[t] == m_row)
                member = jnp.where(in_group, 1.0, 0.0).astype(BF16)
                suffix = jnp.dot(suffix_ones, member, preferred_element_type=F32) + later
                r_scr[t] = jnp.where(in_group & (suffix <= excess), -1.0, r)
                return suffix[0:1, :]

            group_size = lax.fori_loop(0, ntiles, drop_tile, jnp.zeros((1, QB), F32))
            return excess - jnp.minimum(excess, group_size)

        lax.while_loop(lambda e: jnp.max(e) > 0.0, drop_group, excess0)

    slab_reduce = _slab_reduce

    def logit_tile(t, m8):
        sel = r_scr[t] >= thr
        krows = pl.ds(pl.multiple_of(t * TK, TK), TK)
        new_m8 = []
        for hh in range(B_HEADS):
            hs = slice(hh * B_HEAD_DIM, (hh + 1) * B_HEAD_DIM)
            att = jnp.dot(k_ref[0, krows, hs], q_ref[0, 0, hs, :], preferred_element_type=F32)
            att = jnp.where(sel, att, NEG_BIG)
            att_scr[t * B_HEADS + hh] = att
            new_m8.append(jnp.maximum(m8[hh], slab_reduce(att, jnp.maximum)))
        return tuple(new_m8)

    m8 = tiles_in_pairs(ntiles, logit_tile,
                        tuple(jnp.full((SUBLANES, QB), NEG_BIG, F32) for _ in range(B_HEADS)))
    m_row = [jnp.max(m, axis=0, keepdims=True) for m in m8]

    acc_scr[...] = jnp.zeros_like(acc_scr)

    def value_tile(t, l8):
        new_l8 = []
        for hh in range(B_HEADS):
            hs = slice(hh * B_HEAD_DIM, (hh + 1) * B_HEAD_DIM)
            p = jnp.exp(att_scr[t * B_HEADS + hh] - m_row[hh])
            new_l8.append(l8[hh] + slab_reduce(p, jnp.add))
            acc_scr[hs, :] += jnp.dot(v_ref[0, t, hs, :], p.astype(BF16), preferred_element_type=F32)
        return tuple(new_l8)

    l8 = tiles_in_pairs(ntiles, value_tile,
                        tuple(jnp.zeros((SUBLANES, QB), F32) for _ in range(B_HEADS)))

    parts = []
    for hh in range(B_HEADS):
        hs = slice(hh * B_HEAD_DIM, (hh + 1) * B_HEAD_DIM)
        l_row = jnp.sum(l8[hh], axis=0, keepdims=True)
        parts.append(acc_scr[hs, :] / l_row * zb_ref[0, 0, hs, :].astype(F32))
    y_ref[0] = jnp.concatenate(parts, axis=0).T.astype(BF16)


def _attn(qT, qiT, wiT, zbT, k, ki, vT, topk):
    b, nq = qT.shape[0], qT.shape[1]
    s = k.shape[1]
    blk = lambda rows: pl.BlockSpec((1, 1, rows, QB), lambda bi, i: (bi, i, 0, 0))
    return pl.pallas_call(
        functools.partial(_attn_body, topk=topk),
        grid=(b, nq),
        in_specs=[blk(B_WIDTH), blk(IDX_HEADS * IDX_DIM), blk(wiT.shape[2]), blk(B_WIDTH),
                  pl.BlockSpec((1, s, B_WIDTH), lambda bi, i: (bi, 0, 0), pipeline_mode=pl.Buffered(1)),
                  pl.BlockSpec((1, s, IDX_DIM), lambda bi, i: (bi, 0, 0), pipeline_mode=pl.Buffered(1)),
                  pl.BlockSpec((1, nq, B_WIDTH, QB), lambda bi, i: (bi, 0, 0, 0),
                               pipeline_mode=pl.Buffered(1))],
        out_specs=pl.BlockSpec((1, QB, B_WIDTH), lambda bi, i: (bi, i, 0)),
        out_shape=jax.ShapeDtypeStruct((b, s, B_WIDTH), BF16),
        scratch_shapes=[pltpu.VMEM((s // TK, TK, QB), F32),
                        pltpu.VMEM((s // TK, TK, QB), F32),
                        pltpu.VMEM((3, s // TK, TK, QB), BF16),
                        pltpu.VMEM((s // TK * B_HEADS, TK, QB), F32),
                        pltpu.VMEM((B_WIDTH, QB), F32)],
        compiler_params=pltpu.CompilerParams(dimension_semantics=("arbitrary", "arbitrary"),
                                             vmem_limit_bytes=VMEM_LIMIT_BYTES),
        name="attn",
    )(qT, qiT, wiT, zbT, k, ki, vT)


def _out_body(x_ref, yb_ref, ma_ref, sgb_ref, wob_ref, wout_ref, o_ref):
    o_b = jnp.dot(yb_ref[...], wob_ref[...], preferred_element_type=F32)
    merged = ma_ref[...].astype(F32) + sgb_ref[...].astype(F32) * o_b
    o_ref[...] = x_ref[...] + jnp.dot(merged.astype(BF16), wout_ref[...], preferred_element_type=F32)


def _out(x2d, layer, yb, ma, sgb, wob, wout):
    m, d = x2d.shape
    tm = TM_OUT
    row = lambda width: pl.BlockSpec((tm, width), lambda i: (i, 0))
    full = lambda arr: _layer_spec(arr, layer, 1)
    return pl.pallas_call(
        _out_body,
        grid=(m // tm,),
        in_specs=[row(d), row(B_WIDTH), row(d), row(d), full(wob), full(wout)],
        out_specs=row(d),
        out_shape=jax.ShapeDtypeStruct((m, d), F32),
        compiler_params=pltpu.CompilerParams(dimension_semantics=("arbitrary",),
                                             vmem_limit_bytes=VMEM_LIMIT_BYTES),
        name="out_proj",
    )(x2d, yb, ma, sgb, wob, wout)


def _rope_tables(s, dim):
    pos = jnp.arange(s, dtype=F32)
    inv = ROPE_THETA ** (-jnp.arange(0, dim, 2, dtype=F32) / dim)
    ang = pos[:, None] * inv[None, :]
    return jnp.cos(ang), jnp.sin(ang)


def kernel(x, norm_g, w_in, gate_b, a_ln_g, a_ln_b, a_ws, a_bs, q_norm_g, k_norm_g, w_oa, w_ob, w_out):
    b, s, d = x.shape
    depth = w_in.shape[0]
    topk = min(TOPK_MAX, s // 4)
    assert s % TM_STD == 0 and s % QB == 0 and (b * s) % TM_OUT == 0 and QB == TK
    assert w_in.shape[2] == _O_END and topk <= TK

    cos_q, sin_q = _rope_tables(s, B_HEAD_DIM)
    cos_i, sin_i = _rope_tables(s, IDX_DIM)
    cos2 = jnp.concatenate([cos_q, cos_q], axis=1)
    sin2 = jnp.concatenate([-sin_q, sin_q], axis=1)
    cosi2 = jnp.concatenate([cos_i, cos_i, cos_i, cos_i], axis=1)
    sini2 = jnp.concatenate([-sin_i, sin_i, -sin_i, sin_i], axis=1)
    cos_t, sin_t, cosi_t, sini_t = cos_q.T, sin_q.T, cos_i.T, sin_i.T

    w_t = jnp.swapaxes(w_in, 1, 2)
    ik_rows = w_t[:, _O_IK:_O_IW]
    w_std = jnp.concatenate([w_t[:, _O_AU:_O_BQ], w_t[:, _O_BK:_O_BV], ik_rows, ik_rows,
                             w_t[:, _O_GA:_O_END]], axis=1).astype(BF16)
    w_tr = jnp.concatenate([w_t[:, _O_BQ:_O_BK], w_t[:, _O_BV:_O_IK]], axis=1).astype(BF16)
    iw_t = w_t[:, _O_IW:_O_GA]
    w_trw = jnp.concatenate([iw_t, jnp.zeros_like(iw_t)], axis=1).astype(BF16)
    woa, wob, wout = w_oa.astype(BF16), w_ob.astype(BF16), w_out.astype(BF16)
    bias_full = jnp.repeat(jnp.swapaxes(a_bs, 1, 2), A_WIDTH // A_GROUPS, axis=2)
    kng = jnp.reshape(k_norm_g, (depth, 1, B_HEAD_DIM))
    qng = jnp.reshape(q_norm_g, (depth, B_HEAD_DIM, 1))
    ng = jnp.reshape(norm_g, (depth, 1, d))
    lng = jnp.reshape(a_ln_g, (depth, 1, A_WIDTH))
    lnb = jnp.reshape(a_ln_b, (depth, 1, A_WIDTH))

    x2d = jnp.reshape(x, (b * s, d))
    for l in range(depth):
        ma, sgb, k, ki = _proj_std(x2d, l, ng, w_std, gate_b, lng, lnb, a_ws, bias_full, kng, woa,
                                   cos2, sin2, cosi2, sini2, s)
        qT, vT, zbT, qiT, wiT = _proj_tr(jnp.reshape(x2d, (b, s, d)), l, ng, w_tr, w_trw, qng,
                                         cos_t, sin_t, cosi_t, sini_t)
        yb = _attn(qT, qiT, wiT, zbT, jnp.reshape(k, (b, s, B_WIDTH)), jnp.reshape(ki, (b, s, IDX_DIM)),
                   vT, topk)
        x2d = _out(x2d, l, jnp.reshape(yb, (b * s, B_WIDTH)), ma, sgb, wob, wout)
    return jnp.reshape(x2d, (b, s, d))
```

```python
import functools

import jax
import jax.numpy as jnp
from jax import lax
from jax.experimental import pallas as pl
from jax.experimental.pallas import tpu as pltpu

F32 = jnp.float32
BF16 = jnp.bfloat16

CHUNK = 64
EPS = 1e-6
ROPE_THETA = 10000.0

A_WIDTH = 512
A_GROUPS = 4
A_BLOCK = 128
B_HEADS = 4
B_HEAD_DIM = 128
B_WIDTH = B_HEADS * B_HEAD_DIM
IDX_HEADS = 8
IDX_DIM = 64
TOPK_MAX = 256

_O_AU, _O_AV, _O_AZ = 0, 512, 1024
_O_BQ, _O_BK, _O_BV, _O_BZ = 1536, 2048, 2560, 3072
_O_IQ, _O_IK, _O_IW = 3584, 4096, 4160
_O_GA, _O_GB, _O_END = 4168, 5192, 6216

LANES = 128
SUBLANES = 8
VMEM_LIMIT_BYTES = 52 * 1024 * 1024

QB = 256
TK = 256
TM_STD = 512
TM_STD_SUB = 256
TM_OUT = 512
NEG_BIG = -1e30
DIGIT_BITS = 8
DIGIT_BASE = 2 ** DIGIT_BITS
RANK_MAX = float(DIGIT_BASE ** 3 - 1)
COUNT_CHAINS = 4


def _rms_rows(x, g):
    ms = jnp.mean(x * x, axis=-1, keepdims=True)
    return x * lax.rsqrt(ms + EPS) * g


def _proj_body(x_ref, ng_ref, w_ref, gb_ref, lng_ref, lnb_ref, ws_ref, bias_ref, kng_ref,
               woa_ref, cos_ref, sin_ref, cosi_ref, sini_ref,
               wt_ref, ww_ref, qng_ref, cost_ref, sint_ref, cosit_ref, sinit_ref,
               ma_ref, sgb_ref, k_ref, ki_ref, q_ref, v_ref, zb_ref, qi_ref, wi_ref):
    tm = x_ref.shape[0]
    att_scale = B_HEAD_DIM ** -0.5
    half = B_HEAD_DIM // 2
    ihalf = IDX_DIM // 2
    ci = lax.broadcasted_iota(jnp.int32, (A_BLOCK, A_BLOCK), 0) // CHUNK
    cj = lax.broadcasted_iota(jnp.int32, (A_BLOCK, A_BLOCK), 1) // CHUNK
    causal = cj <= ci
    wm = [jnp.where(causal, ws_ref[g], 0.0).astype(BF16) for g in range(A_GROUPS)]

    for r0 in range(0, tm, TM_STD_SUB):
        rows = slice(r0, r0 + TM_STD_SUB)
        h = _rms_rows(x_ref[rows, :], ng_ref[...]).astype(BF16)

        def proj(lo, hi, h=h):
            return lax.dot_general(h, w_ref[lo:hi, :], (((1,), (1,)), ((), ())), preferred_element_type=F32)

        gv = jax.nn.gelu(proj(512, 1024))
        mu = jnp.mean(gv, axis=-1, keepdims=True)
        xc = gv - mu
        var = jnp.mean(xc * xc, axis=-1, keepdims=True)
        vn = (xc * lax.rsqrt(var + EPS) * lng_ref[...] + lnb_ref[...]).astype(BF16)

        row_blocks = []
        for r in range(TM_STD_SUB // A_BLOCK):
            cols = []
            for g in range(A_GROUPS):
                vb = vn[r * A_BLOCK:(r + 1) * A_BLOCK, g * LANES:(g + 1) * LANES]
                cols.append(jnp.dot(wm[g], vb, preferred_element_type=F32))
            row_blocks.append(jnp.concatenate(cols, axis=1) + bias_ref[...])
        mixed = jnp.concatenate(row_blocks, axis=0)

        y_a = jax.nn.gelu(proj(0, 512)) * mixed * jax.nn.silu(proj(1024, 1536))
        o_a = jnp.dot(y_a.astype(BF16), woa_ref[...], preferred_element_type=F32)
        ma_ref[rows, :] = (jax.nn.sigmoid(proj(2176, 3200) + gb_ref[0:1, :]) * o_a).astype(BF16)
        sgb_ref[rows, :] = jax.nn.sigmoid(proj(3200, 4224) + gb_ref[1:2, :]).astype(BF16)

        b_k = proj(1536, 2048)
        for hh in range(B_HEADS):
            kh = _rms_rows(b_k[:, hh * LANES:(hh + 1) * LANES], kng_ref[...])
            kh = kh * cos_ref[rows, :] + pltpu.roll(kh, B_HEAD_DIM // 2, 1) * sin_ref[rows, :]
            k_ref[rows, hh * LANES:(hh + 1) * LANES] = kh.astype(BF16)

        ik = proj(2048, 2176)
        ik = ik * cosi_ref[rows, :] + pltpu.roll(ik, IDX_DIM // 2, 1) * sini_ref[rows, :]
        ki_ref[rows, :] = ik[:, :IDX_DIM].astype(BF16)

        j = r0 // QB

        def proj_t(w, h=h):
            return lax.dot_general(w, h, (((1,), (1,)), ((), ())), preferred_element_type=F32)

        qt = proj_t(wt_ref[0:512, :])
        for hh in range(B_HEADS):
            qh = qt[hh * B_HEAD_DIM:(hh + 1) * B_HEAD_DIM, :]
            ms = jnp.mean(qh * qh, axis=0, keepdims=True)
            qh = qh * lax.rsqrt(ms + EPS) * qng_ref[...]
            x1, x2 = qh[:half, :], qh[half:, :]
            c, s = cost_ref[:, rows], sint_ref[:, rows]
            base = hh * B_HEAD_DIM
            q_ref[0, j, base:base + half, :] = ((x1 * c - x2 * s) * att_scale).astype(BF16)
            q_ref[0, j, base + half:base + B_HEAD_DIM, :] = ((x1 * s + x2 * c) * att_scale).astype(BF16)

        v_ref[0, j] = proj_t(wt_ref[512:1024, :]).astype(BF16)
        zb_ref[0, j] = jax.nn.silu(proj_t(wt_ref[1024:1536, :])).astype(BF16)

        qit = proj_t(wt_ref[1536:2048, :])
        for hh in range(IDX_HEADS):
            xh = qit[hh * IDX_DIM:(hh + 1) * IDX_DIM, :]
            x1, x2 = xh[:ihalf, :], xh[ihalf:, :]
            c, s = cosit_ref[:, rows], sinit_ref[:, rows]
            base = hh * IDX_DIM
            qi_ref[0, j, base:base + ihalf, :] = (x1 * c - x2 * s).astype(BF16)
            qi_ref[0, j, base + ihalf:base + IDX_DIM, :] = (x1 * s + x2 * c).astype(BF16)

        wi_ref[0, j] = proj_t(ww_ref[...])


def _layer_spec(arr, layer):
    zeros = (0,) * (arr.ndim - 1)
    return pl.BlockSpec((None,) + arr.shape[1:], lambda i: (layer,) + zeros)


def _proj(x2d, layer, ng, w_std, gb, lng, lnb, ws, bias_full, kng, woa, cos2, sin2, cosi2, sini2,
          w_tr, w_trw, qng, cos_t, sin_t, cosi_t, sini_t, seq):
    m, d = x2d.shape
    tm = TM_STD
    nt = seq // tm
    b, nq, per = m // seq, seq // QB, tm // QB
    full = lambda arr: _layer_spec(arr, layer)
    tab = lambda: pl.BlockSpec((tm, LANES), lambda i: (i % nt, 0))
    tabt = lambda rows: pl.BlockSpec((rows, tm), lambda i: (0, i % nt))
    row = lambda width: pl.BlockSpec((tm, width), lambda i: (i, 0))
    outt = lambda rows: pl.BlockSpec((1, per, rows, QB), lambda i: (i // nt, i % nt, 0, 0))
    wrows = w_trw.shape[1]
    feat = lambda rows, dtype: jax.ShapeDtypeStruct((b, nq, rows, QB), dtype)
    return pl.pallas_call(
        _proj_body,
        grid=(m // tm,),
        in_specs=[row(d), full(ng), full(w_std), full(gb), full(lng),
                  full(lnb), full(ws), full(bias_full), full(kng),
                  full(woa), tab(), tab(), tab(), tab(),
                  full(w_tr), full(w_trw), full(qng),
                  tabt(B_HEAD_DIM // 2), tabt(B_HEAD_DIM // 2), tabt(IDX_DIM // 2), tabt(IDX_DIM // 2)],
        out_specs=[row(d), row(d), row(B_WIDTH), row(IDX_DIM),
                   outt(B_WIDTH), outt(B_WIDTH), outt(B_WIDTH), outt(IDX_HEADS * IDX_DIM), outt(wrows)],
        out_shape=[jax.ShapeDtypeStruct((m, d), BF16), jax.ShapeDtypeStruct((m, d), BF16),
                   jax.ShapeDtypeStruct((m, B_WIDTH), BF16), jax.ShapeDtypeStruct((m, IDX_DIM), BF16),
                   feat(B_WIDTH, BF16), feat(B_WIDTH, BF16), feat(B_WIDTH, BF16),
                   feat(IDX_HEADS * IDX_DIM, BF16), feat(wrows, F32)],
        compiler_params=pltpu.CompilerParams(dimension_semantics=("arbitrary",),
                                             vmem_limit_bytes=VMEM_LIMIT_BYTES),
        name="proj",
    )(x2d, ng, w_std, gb, lng, lnb, ws, bias_full, kng, woa, cos2, sin2, cosi2, sini2,
      w_tr, w_trw, qng, cos_t, sin_t, cosi_t, sini_t)


def _slab_reduce(x, op):
    parts = [x[r:r + SUBLANES, :] for r in range(0, x.shape[0], SUBLANES)]
    while len(parts) > 1:
        parts = [op(parts[i], parts[i + 1]) for i in range(0, len(parts), 2)]
    return parts[0]


def _attn_body(q_ref, qi_ref, wi_ref, zb_ref, k_ref, ki_ref, v_ref, y_ref,
               s_scr, r_scr, d_scr, att_scr, acc_scr, *, topk):
    n = pl.program_id(1)
    ntiles = n + 1
    idx_scale = (IDX_DIM ** -0.5) * (IDX_HEADS ** -0.5)
    rows_per_tile = TK // SUBLANES

    def score_tile(t, lo8, hi8, diagonal):
        ki_t = ki_ref[0, pl.ds(pl.multiple_of(t * TK, TK), TK), :]
        acc = jnp.zeros((TK, QB), F32)
        for hh in range(IDX_HEADS):
            logit = jnp.dot(ki_t, qi_ref[0, 0, hh * IDX_DIM:(hh + 1) * IDX_DIM, :],
                            preferred_element_type=F32)
            acc = acc + wi_ref[0, 0, hh:hh + 1, :] * jnp.maximum(logit, 0.0)
        sc = acc * idx_scale
        sc_for_min = sc
        if diagonal:
            kc = lax.broadcasted_iota(jnp.int32, (TK, QB), 0) // CHUNK
            qc = lax.broadcasted_iota(jnp.int32, (TK, QB), 1) // CHUNK
            admissible = kc <= qc
            sc_for_min = jnp.where(admissible, sc, jnp.inf)
            sc = jnp.where(admissible, sc, -jnp.inf)
        s_scr[t] = sc
        return (jnp.minimum(lo8, _slab_reduce(sc_for_min, jnp.minimum)),
                jnp.maximum(hi8, _slab_reduce(sc, jnp.maximum)))

    def tiles_in_pairs(count, tile_fn, carry):
        def pair(p, c):
            return tile_fn(2 * p + 1, tile_fn(2 * p, c))
        carry = lax.fori_loop(0, lax.shift_right_logical(count, 1), pair, carry)
        return lax.cond((count & 1) == 1, lambda c: tile_fn(count - 1, c), lambda c: c, carry)

    lo8, hi8 = tiles_in_pairs(
        n, lambda t, c: score_tile(t, c[0], c[1], False),
        (jnp.full((SUBLANES, QB), jnp.inf, F32), jnp.full((SUBLANES, QB), -jnp.inf, F32)))
    lo8, hi8 = score_tile(n, lo8, hi8, True)
    s_lo = jnp.min(lo8, axis=0, keepdims=True)
    s_hi = jnp.max(hi8, axis=0, keepdims=True)

    span = s_hi - s_lo
    rank_scale = jnp.where(span > 0.0, RANK_MAX / jnp.where(span > 0.0, span, 1.0), 0.0)

    def rank_tile(t, carry):
        r = jnp.minimum(jnp.floor((s_scr[t] - s_lo) * rank_scale), RANK_MAX)
        r = jnp.maximum(r, -1.0)
        r_scr[t] = r
        d1 = jnp.floor(r * (1.0 / DIGIT_BASE ** 2))
        rem = r - d1 * float(DIGIT_BASE ** 2)
        d2 = jnp.floor(rem * (1.0 / DIGIT_BASE))
        d_scr[0, t] = d1.astype(BF16)
        d_scr[1, t] = d2.astype(BF16)
        d_scr[2, t] = (rem - d2 * float(DIGIT_BASE)).astype(BF16)
        return carry

    lax.fori_loop(0, ntiles, rank_tile, 0)

    @pl.when((ntiles & 1) == 1)
    def _():
        for plane in range(3):
            d_scr[plane, ntiles] = jnp.full((TK, QB), -1.0, BF16)

    npairs = lax.shift_right_logical(ntiles + 1, 1)
    packed_rows = 2 * SUBLANES
    slabs_per_tile = TK // packed_rows
    one_bf, zero_bf = jnp.ones((), BF16), jnp.zeros((), BF16)

    def count_ge(plane, cand):
        cand_b = jnp.broadcast_to(cand.astype(BF16), (packed_rows, QB))

        def pair_body(p, accs):
            accs = list(accs)
            for half in range(2):
                for r in range(slabs_per_tile):
                    sl = d_scr[plane, 2 * p + half, r * packed_rows:(r + 1) * packed_rows, :]
                    c = r % COUNT_CHAINS
                    accs[c] = accs[c] + jnp.where(sl >= cand_b, one_bf, zero_bf)
            return tuple(accs)

        accs = lax.fori_loop(0, npairs, pair_body,
                             tuple(jnp.zeros((packed_rows, QB), BF16) for _ in range(COUNT_CHAINS)))
        total = functools.reduce(jnp.add, [a.astype(F32) for a in accs])
        return jnp.sum(total, axis=0, keepdims=True)

    def digit_search(plane, cnt_at):
        def bit_body(i, carry):
            prefix, cnt_at, step = carry
            trial = prefix + step
            cnt = count_ge(plane, trial)
            ok = cnt >= float(topk)
            return jnp.where(ok, trial, prefix), jnp.where(ok, cnt, cnt_at), step * 0.5
        digit, cnt_at, _ = lax.fori_loop(
            0, DIGIT_BITS, bit_body,
            (jnp.zeros((1, QB), F32), cnt_at, jnp.full((1, QB), DIGIT_BASE / 2.0, F32)))
        return digit, cnt_at

    def restrict_plane(plane, digit):
        digit_b = jnp.broadcast_to(digit.astype(BF16), (TK, QB))
        above, below = jnp.full((), float(DIGIT_BASE), BF16), jnp.full((), -1.0, BF16)

        def tile_body(t, carry):
            prev = d_scr[plane, t]
            d_scr[plane + 1, t] = jnp.where(prev > digit_b, above,
                                            jnp.where(prev < digit_b, below, d_scr[plane + 1, t]))
            return carry

        lax.fori_loop(0, ntiles, tile_body, 0)

    q_chunk = (n * QB + lax.broadcasted_iota(jnp.int32, (1, QB), 1)) // CHUNK
    n_admissible = ((q_chunk + 1) * CHUNK).astype(F32)
    digit1, cnt_at = digit_search(0, n_admissible)
    restrict_plane(0, digit1)
    digit2, cnt_at = digit_search(1, cnt_at)
    restrict_plane(1, digit2)
    digit3, cnt_at = digit_search(2, cnt_at)
    thr = (digit1 * float(DIGIT_BASE) + digit2) * float(DIGIT_BASE) + digit3

    excess0 = jnp.maximum(cnt_at - float(topk), 0.0)

    @pl.when(jnp.max(excess0) > 0.0)
    def _():
        thr_b = jnp.broadcast_to(thr, (SUBLANES, QB))
        later_or_same = (lax.broadcasted_iota(jnp.int32, (TK, TK), 1)
                         >= lax.broadcasted_iota(jnp.int32, (TK, TK), 0))
        suffix_ones = jnp.where(later_or_same, 1.0, 0.0).astype(BF16)

        def slab(r):
            return slice(r * SUBLANES, (r + 1) * SUBLANES)

        def drop_group(excess):
            def min_tile(t, ms):
                ms = list(ms)
                for r in range(rows_per_tile):
                    selected = r_scr[t, slab(r), :] >= thr_b
                    c = r % COUNT_CHAINS
                    ms[c] = jnp.minimum(ms[c], jnp.where(selected, s_scr[t, slab(r), :], jnp.inf))
                return tuple(ms)

            ms = lax.fori_loop(0, ntiles, min_tile,
                               tuple(jnp.full((SUBLANES, QB), jnp.inf, F32) for _ in range(COUNT_CHAINS)))
            m8 = functools.reduce(jnp.minimum, ms)
            m_row = jnp.min(m8, axis=0, keepdims=True)

            def drop_tile(i, later):
                t = ntiles - 1 - i
                r = r_scr[t]
                in_group = (r >= thr) & (s_scr[t] == m_row)
                member = jnp.where(in_group, 1.0, 0.0).astype(BF16)
                suffix = jnp.dot(suffix_ones, member, preferred_element_type=F32) + later
                r_scr[t] = jnp.where(in_group & (suffix <= excess), -1.0, r)
                return suffix[0:1, :]

            group_size = lax.fori_loop(0, ntiles, drop_tile, jnp.zeros((1, QB), F32))
            return excess - jnp.minimum(excess, group_size)

        lax.while_loop(lambda e: jnp.max(e) > 0.0, drop_group, excess0)

    slab_reduce = _slab_reduce

    def logit_tile(t, m8):
        sel = r_scr[t] >= thr
        krows = pl.ds(pl.multiple_of(t * TK, TK), TK)
        new_m8 = []
        for hh in range(B_HEADS):
            hs = slice(hh * B_HEAD_DIM, (hh + 1) * B_HEAD_DIM)
            att = jnp.dot(k_ref[0, krows, hs], q_ref[0, 0, hs, :], preferred_element_type=F32)
            att = jnp.where(sel, att, NEG_BIG)
            att_scr[t * B_HEADS + hh] = att
            new_m8.append(jnp.maximum(m8[hh], slab_reduce(att, jnp.maximum)))
        return tuple(new_m8)

    m8 = tiles_in_pairs(ntiles, logit_tile,
                        tuple(jnp.full((SUBLANES, QB), NEG_BIG, F32) for _ in range(B_HEADS)))
    m_row = [jnp.max(m, axis=0, keepdims=True) for m in m8]

    acc_scr[...] = jnp.zeros_like(acc_scr)

    def value_tile(t, l8):
        new_l8 = []
        for hh in range(B_HEADS):
            hs = slice(hh * B_HEAD_DIM, (hh + 1) * B_HEAD_DIM)
            p = jnp.exp(att_scr[t * B_HEADS + hh] - m_row[hh])
            new_l8.append(l8[hh] + slab_reduce(p, jnp.add))
            acc_scr[hs, :] += jnp.dot(v_ref[0, t, hs, :], p.astype(BF16), preferred_element_type=F32)
        return tuple(new_l8)

    l8 = tiles_in_pairs(ntiles, value_tile,
                        tuple(jnp.zeros((SUBLANES, QB), F32) for _ in range(B_HEADS)))

    parts = []
    for hh in range(B_HEADS):
        hs = slice(hh * B_HEAD_DIM, (hh + 1) * B_HEAD_DIM)
        l_row = jnp.sum(l8[hh], axis=0, keepdims=True)
        parts.append(acc_scr[hs, :] / l_row * zb_ref[0, 0, hs, :].astype(F32))
    y_ref[0] = jnp.concatenate(parts, axis=0).T.astype(BF16)


def _attn(qT, qiT, wiT, zbT, k, ki, vT, topk):
    b, nq = qT.shape[0], qT.shape[1]
    s = k.shape[1]
    blk = lambda rows: pl.BlockSpec((1, 1, rows, QB), lambda bi, i: (bi, i, 0, 0))
    return pl.pallas_call(
        functools.partial(_attn_body, topk=topk),
        grid=(b, nq),
        in_specs=[blk(B_WIDTH), blk(IDX_HEADS * IDX_DIM), blk(wiT.shape[2]), blk(B_WIDTH),
                  pl.BlockSpec((1, s, B_WIDTH), lambda bi, i: (bi, 0, 0), pipeline_mode=pl.Buffered(1)),
                  pl.BlockSpec((1, s, IDX_DIM), lambda bi, i: (bi, 0, 0), pipeline_mode=pl.Buffered(1)),
                  pl.BlockSpec((1, nq, B_WIDTH, QB), lambda bi, i: (bi, 0, 0, 0),
                               pipeline_mode=pl.Buffered(1))],
        out_specs=pl.BlockSpec((1, QB, B_WIDTH), lambda bi, i: (bi, i, 0)),
        out_shape=jax.ShapeDtypeStruct((b, s, B_WIDTH), BF16),
        scratch_shapes=[pltpu.VMEM((s // TK, TK, QB), F32),
                        pltpu.VMEM((s // TK, TK, QB), F32),
                        pltpu.VMEM((3, s // TK, TK, QB), BF16),
                        pltpu.VMEM((s // TK * B_HEADS, TK, QB), F32),
                        pltpu.VMEM((B_WIDTH, QB), F32)],
        compiler_params=pltpu.CompilerParams(dimension_semantics=("arbitrary", "arbitrary"),
                                             vmem_limit_bytes=VMEM_LIMIT_BYTES),
        name="attn",
    )(qT, qiT, wiT, zbT, k, ki, vT)


def _out_body(x_ref, yb_ref, ma_ref, sgb_ref, wob_ref, wout_ref, o_ref):
    o_b = jnp.dot(yb_ref[...], wob_ref[...], preferred_element_type=F32)
    merged = ma_ref[...].astype(F32) + sgb_ref[...].astype(F32) * o_b
    o_ref[...] = x_ref[...] + jnp.dot(merged.astype(BF16), wout_ref[...], preferred_element_type=F32)


def _out(x2d, layer, yb, ma, sgb, wob, wout):
    m, d = x2d.shape
    tm = TM_OUT
    row = lambda width: pl.BlockSpec((tm, width), lambda i: (i, 0))
    full = lambda arr: _layer_spec(arr, layer)
    return pl.pallas_call(
        _out_body,
        grid=(m // tm,),
        in_specs=[row(d), row(B_WIDTH), row(d), row(d), full(wob), full(wout)],
        out_specs=row(d),
        out_shape=jax.ShapeDtypeStruct((m, d), F32),
        compiler_params=pltpu.CompilerParams(dimension_semantics=("arbitrary",),
                                             vmem_limit_bytes=VMEM_LIMIT_BYTES),
        name="out_proj",
    )(x2d, yb, ma, sgb, wob, wout)


def _rope_tables(s, dim):
    pos = jnp.arange(s, dtype=F32)
    inv = ROPE_THETA ** (-jnp.arange(0, dim, 2, dtype=F32) / dim)
    ang = pos[:, None] * inv[None, :]
    return jnp.cos(ang), jnp.sin(ang)


def kernel(x, norm_g, w_in, gate_b, a_ln_g, a_ln_b, a_ws, a_bs, q_norm_g, k_norm_g, w_oa, w_ob, w_out):
    b, s, d = x.shape
    depth = w_in.shape[0]
    topk = min(TOPK_MAX, s // 4)
    assert s % TM_STD == 0 and s % QB == 0 and (b * s) % TM_OUT == 0 and QB == TK and TM_STD_SUB == QB
    assert w_in.shape[2] == _O_END and topk <= TK

    cos_q, sin_q = _rope_tables(s, B_HEAD_DIM)
    cos_i, sin_i = _rope_tables(s, IDX_DIM)
    cos2 = jnp.concatenate([cos_q, cos_q], axis=1)
    sin2 = jnp.concatenate([-sin_q, sin_q], axis=1)
    cosi2 = jnp.concatenate([cos_i, cos_i, cos_i, cos_i], axis=1)
    sini2 = jnp.concatenate([-sin_i, sin_i, -sin_i, sin_i], axis=1)
    cos_t, sin_t, cosi_t, sini_t = cos_q.T, sin_q.T, cos_i.T, sin_i.T

    w_t = jnp.swapaxes(w_in, 1, 2)
    ik_rows = w_t[:, _O_IK:_O_IW]
    w_std = jnp.concatenate([w_t[:, _O_AU:_O_BQ], w_t[:, _O_BK:_O_BV], ik_rows, ik_rows,
                             w_t[:, _O_GA:_O_END]], axis=1).astype(BF16)
    w_tr = jnp.concatenate([w_t[:, _O_BQ:_O_BK], w_t[:, _O_BV:_O_IK]], axis=1).astype(BF16)
    iw_t = w_t[:, _O_IW:_O_GA]
    w_trw = jnp.concatenate([iw_t, jnp.zeros_like(iw_t)], axis=1).astype(BF16)
    woa, wob, wout = w_oa.astype(BF16), w_ob.astype(BF16), w_out.astype(BF16)
    bias_full = jnp.repeat(jnp.swapaxes(a_bs, 1, 2), A_WIDTH // A_GROUPS, axis=2)
    kng = jnp.reshape(k_norm_g, (depth, 1, B_HEAD_DIM))
    qng = jnp.reshape(q_norm_g, (depth, B_HEAD_DIM, 1))
    ng = jnp.reshape(norm_g, (depth, 1, d))
    lng = jnp.reshape(a_ln_g, (depth, 1, A_WIDTH))
    lnb = jnp.reshape(a_ln_b, (depth, 1, A_WIDTH))

    x2d = jnp.reshape(x, (b * s, d))
    for l in range(depth):
        ma, sgb, k, ki, qT, vT, zbT, qiT, wiT = _proj(
            x2d, l, ng, w_std, gate_b, lng, lnb, a_ws, bias_full, kng, woa, cos2, sin2, cosi2, sini2,
            w_tr, w_trw, qng, cos_t, sin_t, cosi_t, sini_t, s)
        yb = _attn(qT, qiT, wiT, zbT, jnp.reshape(k, (b, s, B_WIDTH)), jnp.reshape(ki, (b, s, IDX_DIM)),
                   vT, topk)
        x2d = _out(x2d, l, jnp.reshape(yb, (b * s, B_WIDTH)), ma, sgb, wob, wout)
    return jnp.reshape(x2d, (b, s, d))
```

```python
import functools

import jax
import jax.numpy as jnp
from jax import lax
from jax.experimental import pallas as pl
from jax.experimental.pallas import tpu as pltpu

F32 = jnp.float32
BF16 = jnp.bfloat16

CHUNK = 64
EPS = 1e-6
ROPE_THETA = 10000.0

A_WIDTH = 512
A_GROUPS = 4
A_BLOCK = 128
B_HEADS = 4
B_HEAD_DIM = 128
B_WIDTH = B_HEADS * B_HEAD_DIM
IDX_HEADS = 8
IDX_DIM = 64
TOPK_MAX = 256

_O_AU, _O_AV, _O_AZ = 0, 512, 1024
_O_BQ, _O_BK, _O_BV, _O_BZ = 1536, 2048, 2560, 3072
_O_IQ, _O_IK, _O_IW = 3584, 4096, 4160
_O_GA, _O_GB, _O_END = 4168, 5192, 6216

_P_AU, _P_AV, _P_AZ, _P_BK, _P_IK, _P_GA, _P_GB, _P_END = 0, 512, 1024, 1536, 2048, 2176, 3200, 4224
_F_BQ, _F_BV, _F_BZ, _F_IQ, _F_END = 0, 512, 1024, 1536, 2048

LANES = 128
SUBLANES = 8
VMEM_LIMIT_BYTES = 52 * 1024 * 1024

QB = 256
TK = 256
TM_STD = 512
TM_STD_SUB = 256
TM_OUT = 512
NEG_BIG = -1e30
DIGIT_BITS = 8
DIGIT_BASE = 2 ** DIGIT_BITS
RANK_MAX = float(DIGIT_BASE ** 3 - 1)
COUNT_CHAINS = 4


def _rms_rows(x, g):
    ms = jnp.mean(x * x, axis=-1, keepdims=True)
    return x * lax.rsqrt(ms + EPS) * g


def _proj_body(x_ref, ng_ref, w_ref, gb_ref, lng_ref, lnb_ref, ws_ref, bias_ref, kng_ref,
               woa_ref, cos_ref, sin_ref, cosi_ref, sini_ref,
               wt_ref, ww_ref, qng_ref, cost_ref, sint_ref, cosit_ref, sinit_ref,
               ma_ref, sgb_ref, k_ref, ki_ref, q_ref, v_ref, zb_ref, qi_ref, wi_ref):
    tm = x_ref.shape[0]
    att_scale = B_HEAD_DIM ** -0.5
    half = B_HEAD_DIM // 2
    ihalf = IDX_DIM // 2
    ci = lax.broadcasted_iota(jnp.int32, (A_BLOCK, A_BLOCK), 0) // CHUNK
    cj = lax.broadcasted_iota(jnp.int32, (A_BLOCK, A_BLOCK), 1) // CHUNK
    causal = cj <= ci
    wm = [jnp.where(causal, ws_ref[g], 0.0).astype(BF16) for g in range(A_GROUPS)]

    for r0 in range(0, tm, TM_STD_SUB):
        rows = slice(r0, r0 + TM_STD_SUB)
        h = _rms_rows(x_ref[rows, :], ng_ref[...]).astype(BF16)

        def proj(lo, hi, h=h):
            return lax.dot_general(h, w_ref[lo:hi, :], (((1,), (1,)), ((), ())), preferred_element_type=F32)

        gv = jax.nn.gelu(proj(_P_AV, _P_AZ))
        mu = jnp.mean(gv, axis=-1, keepdims=True)
        xc = gv - mu
        var = jnp.mean(xc * xc, axis=-1, keepdims=True)
        vn = (xc * lax.rsqrt(var + EPS) * lng_ref[...] + lnb_ref[...]).astype(BF16)

        row_blocks = []
        for r in range(TM_STD_SUB // A_BLOCK):
            cols = []
            for g in range(A_GROUPS):
                vb = vn[r * A_BLOCK:(r + 1) * A_BLOCK, g * LANES:(g + 1) * LANES]
                cols.append(jnp.dot(wm[g], vb, preferred_element_type=F32))
            row_blocks.append(jnp.concatenate(cols, axis=1) + bias_ref[...])
        mixed = jnp.concatenate(row_blocks, axis=0)

        y_a = jax.nn.gelu(proj(_P_AU, _P_AV)) * mixed * jax.nn.silu(proj(_P_AZ, _P_BK))
        o_a = jnp.dot(y_a.astype(BF16), woa_ref[...], preferred_element_type=F32)
        ma_ref[rows, :] = (jax.nn.sigmoid(proj(_P_GA, _P_GB) + gb_ref[0:1, :]) * o_a).astype(BF16)
        sgb_ref[rows, :] = jax.nn.sigmoid(proj(_P_GB, _P_END) + gb_ref[1:2, :]).astype(BF16)

        b_k = proj(_P_BK, _P_IK)
        for hh in range(B_HEADS):
            kh = _rms_rows(b_k[:, hh * LANES:(hh + 1) * LANES], kng_ref[...])
            kh = kh * cos_ref[rows, :] + pltpu.roll(kh, B_HEAD_DIM // 2, 1) * sin_ref[rows, :]
            k_ref[rows, hh * LANES:(hh + 1) * LANES] = kh.astype(BF16)

        ik = proj(_P_IK, _P_GA)
        ik = ik * cosi_ref[rows, :] + pltpu.roll(ik, IDX_DIM // 2, 1) * sini_ref[rows, :]
        ki_ref[rows, :] = ik[:, :IDX_DIM].astype(BF16)

        j = r0 // QB

        def proj_t(w, h=h):
            return lax.dot_general(w, h, (((1,), (1,)), ((), ())), preferred_element_type=F32)

        qt = proj_t(wt_ref[_F_BQ:_F_BV, :])
        for hh in range(B_HEADS):
            qh = qt[hh * B_HEAD_DIM:(hh + 1) * B_HEAD_DIM, :]
            ms = jnp.mean(qh * qh, axis=0, keepdims=True)
            qh = qh * lax.rsqrt(ms + EPS) * qng_ref[...]
            x1, x2 = qh[:half, :], qh[half:, :]
            c, s = cost_ref[:, rows], sint_ref[:, rows]
            base = hh * B_HEAD_DIM
            q_ref[0, j, base:base + half, :] = ((x1 * c - x2 * s) * att_scale).astype(BF16)
            q_ref[0, j, base + half:base + B_HEAD_DIM, :] = ((x1 * s + x2 * c) * att_scale).astype(BF16)

        v_ref[0, j] = proj_t(wt_ref[_F_BV:_F_BZ, :]).astype(BF16)
        zb_ref[0, j] = jax.nn.silu(proj_t(wt_ref[_F_BZ:_F_IQ, :])).astype(BF16)

        qit = proj_t(wt_ref[_F_IQ:_F_END, :])
        for hh in range(IDX_HEADS):
            xh = qit[hh * IDX_DIM:(hh + 1) * IDX_DIM, :]
            x1, x2 = xh[:ihalf, :], xh[ihalf:, :]
            c, s = cosit_ref[:, rows], sinit_ref[:, rows]
            base = hh * IDX_DIM
            qi_ref[0, j, base:base + ihalf, :] = (x1 * c - x2 * s).astype(BF16)
            qi_ref[0, j, base + ihalf:base + IDX_DIM, :] = (x1 * s + x2 * c).astype(BF16)

        wi_ref[0, j] = proj_t(ww_ref[...])


def _layer_spec(arr, layer):
    zeros = (0,) * (arr.ndim - 1)
    return pl.BlockSpec((None,) + arr.shape[1:], lambda i: (layer,) + zeros)


def _proj(x2d, layer, ng, w_std, gb, lng, lnb, ws, bias_full, kng, woa, cos2, sin2, cosi2, sini2,
          w_tr, w_trw, qng, cos_t, sin_t, cosi_t, sini_t, seq):
    m, d = x2d.shape
    tm = TM_STD
    nt = seq // tm
    b, nq, per = m // seq, seq // QB, tm // QB
    full = lambda arr: _layer_spec(arr, layer)
    tab = lambda: pl.BlockSpec((tm, LANES), lambda i: (i % nt, 0))
    tabt = lambda rows: pl.BlockSpec((rows, tm), lambda i: (0, i % nt))
    row = lambda width: pl.BlockSpec((tm, width), lambda i: (i, 0))
    outt = lambda rows: pl.BlockSpec((1, per, rows, QB), lambda i: (i // nt, i % nt, 0, 0))
    wrows = w_trw.shape[1]
    feat = lambda rows, dtype: jax.ShapeDtypeStruct((b, nq, rows, QB), dtype)
    return pl.pallas_call(
        _proj_body,
        grid=(m // tm,),
        in_specs=[row(d), full(ng), full(w_std), full(gb), full(lng),
                  full(lnb), full(ws), full(bias_full), full(kng),
                  full(woa), tab(), tab(), tab(), tab(),
                  full(w_tr), full(w_trw), full(qng),
                  tabt(B_HEAD_DIM // 2), tabt(B_HEAD_DIM // 2), tabt(IDX_DIM // 2), tabt(IDX_DIM // 2)],
        out_specs=[row(d), row(d), row(B_WIDTH), row(IDX_DIM),
                   outt(B_WIDTH), outt(B_WIDTH), outt(B_WIDTH), outt(IDX_HEADS * IDX_DIM), outt(wrows)],
        out_shape=[jax.ShapeDtypeStruct((m, d), BF16), jax.ShapeDtypeStruct((m, d), BF16),
                   jax.ShapeDtypeStruct((m, B_WIDTH), BF16), jax.ShapeDtypeStruct((m, IDX_DIM), BF16),
                   feat(B_WIDTH, BF16), feat(B_WIDTH, BF16), feat(B_WIDTH, BF16),
                   feat(IDX_HEADS * IDX_DIM, BF16), feat(wrows, F32)],
        compiler_params=pltpu.CompilerParams(dimension_semantics=("arbitrary",),
                                             vmem_limit_bytes=VMEM_LIMIT_BYTES),
        name="proj",
    )(x2d, ng, w_std, gb, lng, lnb, ws, bias_full, kng, woa, cos2, sin2, cosi2, sini2,
      w_tr, w_trw, qng, cos_t, sin_t, cosi_t, sini_t)


def _slab_reduce(x, op):
    parts = [x[r:r + SUBLANES, :] for r in range(0, x.shape[0], SUBLANES)]
    while len(parts) > 1:
        parts = [op(parts[i], parts[i + 1]) for i in range(0, len(parts), 2)]
    return parts[0]


def _attn_body(q_ref, qi_ref, wi_ref, zb_ref, k_ref, ki_ref, v_ref, y_ref,
               s_scr, r_scr, d_scr, att_scr, acc_scr, *, topk):
    n = pl.program_id(1)
    ntiles = n + 1
    idx_scale = (IDX_DIM ** -0.5) * (IDX_HEADS ** -0.5)
    rows_per_tile = TK // SUBLANES

    def score_tile(t, lo8, hi8, diagonal):
        ki_t = ki_ref[0, pl.ds(pl.multiple_of(t * TK, TK), TK), :]
        acc = jnp.zeros((TK, QB), F32)
        for hh in range(IDX_HEADS):
            logit = jnp.dot(ki_t, qi_ref[0, 0, hh * IDX_DIM:(hh + 1) * IDX_DIM, :],
                            preferred_element_type=F32)
            acc = acc + wi_ref[0, 0, hh:hh + 1, :] * jnp.maximum(logit, 0.0)
        sc = acc * idx_scale
        sc_for_min = sc
        if diagonal:
            kc = lax.broadcasted_iota(jnp.int32, (TK, QB), 0) // CHUNK
            qc = lax.broadcasted_iota(jnp.int32, (TK, QB), 1) // CHUNK
            admissible = kc <= qc
            sc_for_min = jnp.where(admissible, sc, jnp.inf)
            sc = jnp.where(admissible, sc, -jnp.inf)
        s_scr[t] = sc
        return (jnp.minimum(lo8, _slab_reduce(sc_for_min, jnp.minimum)),
                jnp.maximum(hi8, _slab_reduce(sc, jnp.maximum)))

    def tiles_in_pairs(count, tile_fn, carry):
        def pair(p, c):
            return tile_fn(2 * p + 1, tile_fn(2 * p, c))
        carry = lax.fori_loop(0, lax.shift_right_logical(count, 1), pair, carry)
        return lax.cond((count & 1) == 1, lambda c: tile_fn(count - 1, c), lambda c: c, carry)

    lo8, hi8 = tiles_in_pairs(
        n, lambda t, c: score_tile(t, c[0], c[1], False),
        (jnp.full((SUBLANES, QB), jnp.inf, F32), jnp.full((SUBLANES, QB), -jnp.inf, F32)))
    lo8, hi8 = score_tile(n, lo8, hi8, True)
    s_lo = jnp.min(lo8, axis=0, keepdims=True)
    s_hi = jnp.max(hi8, axis=0, keepdims=True)

    span = s_hi - s_lo
    rank_scale = jnp.where(span > 0.0, RANK_MAX / jnp.where(span > 0.0, span, 1.0), 0.0)

    def rank_tile(t, carry):
        r = jnp.minimum(jnp.floor((s_scr[t] - s_lo) * rank_scale), RANK_MAX)
        r = jnp.maximum(r, -1.0)
        r_scr[t] = r
        d1 = jnp.floor(r * (1.0 / DIGIT_BASE ** 2))
        rem = r - d1 * float(DIGIT_BASE ** 2)
        d2 = jnp.floor(rem * (1.0 / DIGIT_BASE))
        d_scr[0, t] = d1.astype(BF16)
        d_scr[1, t] = d2.astype(BF16)
        d_scr[2, t] = (rem - d2 * float(DIGIT_BASE)).astype(BF16)
        return carry

    lax.fori_loop(0, ntiles, rank_tile, 0)

    @pl.when((ntiles & 1) == 1)
    def _():
        for plane in range(3):
            d_scr[plane, ntiles] = jnp.full((TK, QB), -1.0, BF16)

    npairs = lax.shift_right_logical(ntiles + 1, 1)
    packed_rows = 2 * SUBLANES
    slabs_per_tile = TK // packed_rows
    one_bf, zero_bf = jnp.ones((), BF16), jnp.zeros((), BF16)

    def count_ge(plane, cand):
        cand_b = jnp.broadcast_to(cand.astype(BF16), (packed_rows, QB))

        def pair_body(p, accs):
            accs = list(accs)
            for half in range(2):
                for r in range(slabs_per_tile):
                    sl = d_scr[plane, 2 * p + half, r * packed_rows:(r + 1) * packed_rows, :]
                    c = r % COUNT_CHAINS
                    accs[c] = accs[c] + jnp.where(sl >= cand_b, one_bf, zero_bf)
            return tuple(accs)

        accs = lax.fori_loop(0, npairs, pair_body,
                             tuple(jnp.zeros((packed_rows, QB), BF16) for _ in range(COUNT_CHAINS)))
        total = functools.reduce(jnp.add, [a.astype(F32) for a in accs])
        return jnp.sum(total, axis=0, keepdims=True)

    def digit_search(plane, cnt_at):
        def bit_body(i, carry):
            prefix, cnt_at, step = carry
            trial = prefix + step
            cnt = count_ge(plane, trial)
            ok = cnt >= float(topk)
            return jnp.where(ok, trial, prefix), jnp.where(ok, cnt, cnt_at), step * 0.5
        digit, cnt_at, _ = lax.fori_loop(
            0, DIGIT_BITS, bit_body,
            (jnp.zeros((1, QB), F32), cnt_at, jnp.full((1, QB), DIGIT_BASE / 2.0, F32)))
        return digit, cnt_at

    def restrict_plane(plane, digit):
        digit_b = jnp.broadcast_to(digit.astype(BF16), (TK, QB))
        above, below = jnp.full((), float(DIGIT_BASE), BF16), jnp.full((), -1.0, BF16)

        def tile_body(t, carry):
            prev = d_scr[plane, t]
            d_scr[plane + 1, t] = jnp.where(prev > digit_b, above,
                                            jnp.where(prev < digit_b, below, d_scr[plane + 1, t]))
            return carry

        lax.fori_loop(0, ntiles, tile_body, 0)

    q_chunk = (n * QB + lax.broadcasted_iota(jnp.int32, (1, QB), 1)) // CHUNK
    n_admissible = ((q_chunk + 1) * CHUNK).astype(F32)
    digit1, cnt_at = digit_search(0, n_admissible)
    restrict_plane(0, digit1)
    digit2, cnt_at = digit_search(1, cnt_at)
    restrict_plane(1, digit2)
    digit3, cnt_at = digit_search(2, cnt_at)
    thr = (digit1 * float(DIGIT_BASE) + digit2) * float(DIGIT_BASE) + digit3

    excess0 = jnp.maximum(cnt_at - float(topk), 0.0)

    @pl.when(jnp.max(excess0) > 0.0)
    def _():
        thr_b = jnp.broadcast_to(thr, (SUBLANES, QB))
        later_or_same = (lax.broadcasted_iota(jnp.int32, (TK, TK), 1)
                         >= lax.broadcasted_iota(jnp.int32, (TK, TK), 0))
        suffix_ones = jnp.where(later_or_same, 1.0, 0.0).astype(BF16)

        def slab(r):
            return slice(r * SUBLANES, (r + 1) * SUBLANES)

        def drop_group(excess):
            def min_tile(t, ms):
                ms = list(ms)
                for r in range(rows_per_tile):
                    selected = r_scr[t, slab(r), :] >= thr_b
                    c = r % COUNT_CHAINS
                    ms[c] = jnp.minimum(ms[c], jnp.where(selected, s_scr[t, slab(r), :], jnp.inf))
                return tuple(ms)

            ms = lax.fori_loop(0, ntiles, min_tile,
                               tuple(jnp.full((SUBLANES, QB), jnp.inf, F32) for _ in range(COUNT_CHAINS)))
            m8 = functools.reduce(jnp.minimum, ms)
            m_row = jnp.min(m8, axis=0, keepdims=True)

            def drop_tile(i, later):
                t = ntiles - 1 - i
                r = r_scr[t]
                in_group = (r >= thr) & (s_scr[t] == m_row)
                member = jnp.where(in_group, 1.0, 0.0).astype(BF16)
                suffix = jnp.dot(suffix_ones, member, preferred_element_type=F32) + later
                r_scr[t] = jnp.where(in_group & (suffix <= excess), -1.0, r)
                return suffix[0:1, :]

            group_size = lax.fori_loop(0, ntiles, drop_tile, jnp.zeros((1, QB), F32))
            return excess - jnp.minimum(excess, group_size)

        lax.while_loop(lambda e: jnp.max(e) > 0.0, drop_group, excess0)

    slab_reduce = _slab_reduce

    def logit_tile(t, m8):
        sel = r_scr[t] >= thr
        krows = pl.ds(pl.multiple_of(t * TK, TK), TK)
        new_m8 = []
        for hh in range(B_HEADS):
            hs = slice(hh * B_HEAD_DIM, (hh + 1) * B_HEAD_DIM)
            att = jnp.dot(k_ref[0, krows, hs], q_ref[0, 0, hs, :], preferred_element_type=F32)
            att = jnp.where(sel, att, NEG_BIG)
            att_scr[t * B_HEADS + hh] = att
            new_m8.append(jnp.maximum(m8[hh], slab_reduce(att, jnp.maximum)))
        return tuple(new_m8)

    m8 = tiles_in_pairs(ntiles, logit_tile,
                        tuple(jnp.full((SUBLANES, QB), NEG_BIG, F32) for _ in range(B_HEADS)))
    m_row = [jnp.max(m, axis=0, keepdims=True) for m in m8]

    acc_scr[...] = jnp.zeros_like(acc_scr)

    def value_tile(t, l8):
        new_l8 = []
        for hh in range(B_HEADS):
            hs = slice(hh * B_HEAD_DIM, (hh + 1) * B_HEAD_DIM)
            p = jnp.exp(att_scr[t * B_HEADS + hh] - m_row[hh])
            new_l8.append(l8[hh] + slab_reduce(p, jnp.add))
            acc_scr[hs, :] += jnp.dot(v_ref[0, t, hs, :], p.astype(BF16), preferred_element_type=F32)
        return tuple(new_l8)

    l8 = tiles_in_pairs(ntiles, value_tile,
                        tuple(jnp.zeros((SUBLANES, QB), F32) for _ in range(B_HEADS)))

    parts = []
    for hh in range(B_HEADS):
        hs = slice(hh * B_HEAD_DIM, (hh + 1) * B_HEAD_DIM)
        l_row = jnp.sum(l8[hh], axis=0, keepdims=True)
        parts.append(acc_scr[hs, :] / l_row * zb_ref[0, 0, hs, :].astype(F32))
    y_ref[0] = jnp.concatenate(parts, axis=0).T.astype(BF16)


def _attn(qT, qiT, wiT, zbT, k, ki, vT, topk):
    b, nq = qT.shape[0], qT.shape[1]
    s = k.shape[1]
    blk = lambda rows: pl.BlockSpec((1, 1, rows, QB), lambda bi, i: (bi, i, 0, 0))
    return pl.pallas_call(
        functools.partial(_attn_body, topk=topk),
        grid=(b, nq),
        in_specs=[blk(B_WIDTH), blk(IDX_HEADS * IDX_DIM), blk(wiT.shape[2]), blk(B_WIDTH),
                  pl.BlockSpec((1, s, B_WIDTH), lambda bi, i: (bi, 0, 0), pipeline_mode=pl.Buffered(1)),
                  pl.BlockSpec((1, s, IDX_DIM), lambda bi, i: (bi, 0, 0), pipeline_mode=pl.Buffered(1)),
                  pl.BlockSpec((1, nq, B_WIDTH, QB), lambda bi, i: (bi, 0, 0, 0),
                               pipeline_mode=pl.Buffered(1))],
        out_specs=pl.BlockSpec((1, QB, B_WIDTH), lambda bi, i: (bi, i, 0)),
        out_shape=jax.ShapeDtypeStruct((b, s, B_WIDTH), BF16),
        scratch_shapes=[pltpu.VMEM((s // TK, TK, QB), F32),
                        pltpu.VMEM((s // TK, TK, QB), F32),
                        pltpu.VMEM((3, s // TK, TK, QB), BF16),
                        pltpu.VMEM((s // TK * B_HEADS, TK, QB), F32),
                        pltpu.VMEM((B_WIDTH, QB), F32)],
        compiler_params=pltpu.CompilerParams(dimension_semantics=("arbitrary", "arbitrary"),
                                             vmem_limit_bytes=VMEM_LIMIT_BYTES),
        name="attn",
    )(qT, qiT, wiT, zbT, k, ki, vT)


def _out_body(x_ref, yb_ref, ma_ref, sgb_ref, wob_ref, wout_ref, o_ref):
    o_b = jnp.dot(yb_ref[...], wob_ref[...], preferred_element_type=F32)
    merged = ma_ref[...].astype(F32) + sgb_ref[...].astype(F32) * o_b
    o_ref[...] = x_ref[...] + jnp.dot(merged.astype(BF16), wout_ref[...], preferred_element_type=F32)


def _out(x2d, layer, yb, ma, sgb, wob, wout):
    m, d = x2d.shape
    tm = TM_OUT
    row = lambda width: pl.BlockSpec((tm, width), lambda i: (i, 0))
    full = lambda arr: _layer_spec(arr, layer)
    return pl.pallas_call(
        _out_body,
        grid=(m // tm,),
        in_specs=[row(d), row(B_WIDTH), row(d), row(d), full(wob), full(wout)],
        out_specs=row(d),
        out_shape=jax.ShapeDtypeStruct((m, d), F32),
        compiler_params=pltpu.CompilerParams(dimension_semantics=("arbitrary",),
                                             vmem_limit_bytes=VMEM_LIMIT_BYTES),
        name="out_proj",
    )(x2d, yb, ma, sgb, wob, wout)


def _rope_tables(s, dim):
    pos = jnp.arange(s, dtype=F32)
    inv = ROPE_THETA ** (-jnp.arange(0, dim, 2, dtype=F32) / dim)
    ang = pos[:, None] * inv[None, :]
    return jnp.cos(ang), jnp.sin(ang)


def kernel(x, norm_g, w_in, gate_b, a_ln_g, a_ln_b, a_ws, a_bs, q_norm_g, k_norm_g, w_oa, w_ob, w_out):
    b, s, d = x.shape
    depth = w_in.shape[0]
    topk = min(TOPK_MAX, s // 4)
    assert s % TM_STD == 0 and s % QB == 0 and (b * s) % TM_OUT == 0 and QB == TK and TM_STD_SUB == QB
    assert w_in.shape[2] == _O_END and topk <= TK

    cos_q, sin_q = _rope_tables(s, B_HEAD_DIM)
    cos_i, sin_i = _rope_tables(s, IDX_DIM)
    cos2 = jnp.concatenate([cos_q, cos_q], axis=1)
    sin2 = jnp.concatenate([-sin_q, sin_q], axis=1)
    cosi2 = jnp.concatenate([cos_i, cos_i, cos_i, cos_i], axis=1)
    sini2 = jnp.concatenate([-sin_i, sin_i, -sin_i, sin_i], axis=1)
    cos_t, sin_t, cosi_t, sini_t = cos_q.T, sin_q.T, cos_i.T, sin_i.T

    w_t = jnp.swapaxes(w_in, 1, 2)
    ik_rows = w_t[:, _O_IK:_O_IW]
    w_std = jnp.concatenate([w_t[:, _O_AU:_O_BQ], w_t[:, _O_BK:_O_BV], ik_rows, ik_rows,
                             w_t[:, _O_GA:_O_END]], axis=1).astype(BF16)
    w_tr = jnp.concatenate([w_t[:, _O_BQ:_O_BK], w_t[:, _O_BV:_O_IK]], axis=1).astype(BF16)
    iw_t = w_t[:, _O_IW:_O_GA]
    w_trw = jnp.concatenate([iw_t, jnp.zeros_like(iw_t)], axis=1).astype(BF16)
    woa, wob, wout = w_oa.astype(BF16), w_ob.astype(BF16), w_out.astype(BF16)
    bias_full = jnp.repeat(jnp.swapaxes(a_bs, 1, 2), A_WIDTH // A_GROUPS, axis=2)
    kng = jnp.reshape(k_norm_g, (depth, 1, B_HEAD_DIM))
    qng = jnp.reshape(q_norm_g, (depth, B_HEAD_DIM, 1))
    ng = jnp.reshape(norm_g, (depth, 1, d))
    lng = jnp.reshape(a_ln_g, (depth, 1, A_WIDTH))
    lnb = jnp.reshape(a_ln_b, (depth, 1, A_WIDTH))

    x2d = jnp.reshape(x, (b * s, d))
    for l in range(depth):
        ma, sgb, k, ki, qT, vT, zbT, qiT, wiT = _proj(
            x2d, l, ng, w_std, gate_b, lng, lnb, a_ws, bias_full, kng, woa, cos2, sin2, cosi2, sini2,
            w_tr, w_trw, qng, cos_t, sin_t, cosi_t, sini_t, s)
        yb = _attn(qT, qiT, wiT, zbT, jnp.reshape(k, (b, s, B_WIDTH)), jnp.reshape(ki, (b, s, IDX_DIM)),
                   vT, topk)
        x2d = _out(x2d, l, jnp.reshape(yb, (b * s, B_WIDTH)), ma, sgb, wob, wout)
    return jnp.reshape(x2d, (b, s, d))
```

```python
import functools

import jax
import jax.numpy as jnp
from jax import lax
from jax.experimental import pallas as pl
from jax.experimental.pallas import tpu as pltpu

F32 = jnp.float32
BF16 = jnp.bfloat16

CHUNK = 64
EPS = 1e-6
ROPE_THETA = 10000.0

A_WIDTH = 512
A_GROUPS = 4
A_BLOCK = 128
B_HEADS = 4
B_HEAD_DIM = 128
B_WIDTH = B_HEADS * B_HEAD_DIM
IDX_HEADS = 8
IDX_DIM = 64
TOPK_MAX = 256

_O_AU, _O_AV, _O_AZ = 0, 512, 1024
_O_BQ, _O_BK, _O_BV, _O_BZ = 1536, 2048, 2560, 3072
_O_IQ, _O_IK, _O_IW = 3584, 4096, 4160
_O_GA, _O_GB, _O_END = 4168, 5192, 6216

_P_AU, _P_AV, _P_AZ, _P_BK, _P_IK, _P_GA, _P_GB, _P_END = 0, 512, 1024, 1536, 2048, 2176, 3200, 4224
_F_BQ, _F_BV, _F_BZ, _F_IQ, _F_END = 0, 512, 1024, 1536, 2048

LANES = 128
SUBLANES = 8
VMEM_LIMIT_BYTES = 52 * 1024 * 1024

QB = 256
TK = 256
TM_STD = 512
TM_STD_SUB = 256
TM_OUT = 512
NEG_BIG = -1e30
DIGIT_BITS = 8
DIGIT_BASE = 2 ** DIGIT_BITS
RANK_MAX = float(DIGIT_BASE ** 3 - 1)
COUNT_CHAINS = 4


def _rms_rows(x, g):
    ms = jnp.mean(x * x, axis=-1, keepdims=True)
    return x * lax.rsqrt(ms + EPS) * g


def _proj_body(x_ref, ng_ref, w_ref, gb_ref, lng_ref, lnb_ref, ws_ref, bias_ref, kng_ref,
               woa_ref, cos_ref, sin_ref, cosi_ref, sini_ref,
               wt_ref, ww_ref, qng_ref, cost_ref, sint_ref, cosit_ref, sinit_ref,
               ma_ref, sgb_ref, k_ref, ki_ref, q_ref, v_ref, zb_ref, qi_ref, wi_ref):
    tm = x_ref.shape[0]
    att_scale = B_HEAD_DIM ** -0.5
    half = B_HEAD_DIM // 2
    ihalf = IDX_DIM // 2
    ci = lax.broadcasted_iota(jnp.int32, (A_BLOCK, A_BLOCK), 0) // CHUNK
    cj = lax.broadcasted_iota(jnp.int32, (A_BLOCK, A_BLOCK), 1) // CHUNK
    causal = cj <= ci
    wm = [jnp.where(causal, ws_ref[g], 0.0).astype(BF16) for g in range(A_GROUPS)]

    for r0 in range(0, tm, TM_STD_SUB):
        rows = slice(r0, r0 + TM_STD_SUB)
        h = _rms_rows(x_ref[rows, :], ng_ref[...]).astype(BF16)

        def proj(lo, hi, h=h):
            return lax.dot_general(h, w_ref[lo:hi, :], (((1,), (1,)), ((), ())), preferred_element_type=F32)

        gv = jax.nn.gelu(proj(_P_AV, _P_AZ))
        mu = jnp.mean(gv, axis=-1, keepdims=True)
        xc = gv - mu
        var = jnp.mean(xc * xc, axis=-1, keepdims=True)
        vn = (xc * lax.rsqrt(var + EPS) * lng_ref[...] + lnb_ref[...]).astype(BF16)

        row_blocks = []
        for r in range(TM_STD_SUB // A_BLOCK):
            cols = []
            for g in range(A_GROUPS):
                vb = vn[r * A_BLOCK:(r + 1) * A_BLOCK, g * LANES:(g + 1) * LANES]
                cols.append(jnp.dot(wm[g], vb, preferred_element_type=F32))
            row_blocks.append(jnp.concatenate(cols, axis=1) + bias_ref[...])
        mixed = jnp.concatenate(row_blocks, axis=0)

        y_a = jax.nn.gelu(proj(_P_AU, _P_AV)) * mixed * jax.nn.silu(proj(_P_AZ, _P_BK))
        o_a = jnp.dot(y_a.astype(BF16), woa_ref[...], preferred_element_type=F32)
        ma_ref[rows, :] = (jax.nn.sigmoid(proj(_P_GA, _P_GB) + gb_ref[0:1, :]) * o_a).astype(BF16)
        sgb_ref[rows, :] = jax.nn.sigmoid(proj(_P_GB, _P_END) + gb_ref[1:2, :]).astype(BF16)

        b_k = proj(_P_BK, _P_IK)
        for hh in range(B_HEADS):
            kh = _rms_rows(b_k[:, hh * LANES:(hh + 1) * LANES], kng_ref[...])
            kh = kh * cos_ref[rows, :] + pltpu.roll(kh, B_HEAD_DIM // 2, 1) * sin_ref[rows, :]
            k_ref[rows, hh * LANES:(hh + 1) * LANES] = kh.astype(BF16)

        ik = proj(_P_IK, _P_GA)
        ik = ik * cosi_ref[rows, :] + pltpu.roll(ik, IDX_DIM // 2, 1) * sini_ref[rows, :]
        ki_ref[rows, :] = ik[:, :IDX_DIM].astype(BF16)

        j = r0 // QB

        def proj_t(w, h=h):
            return lax.dot_general(w, h, (((1,), (1,)), ((), ())), preferred_element_type=F32)

        qt = proj_t(wt_ref[_F_BQ:_F_BV, :])
        for hh in range(B_HEADS):
            qh = qt[hh * B_HEAD_DIM:(hh + 1) * B_HEAD_DIM, :]
            ms = jnp.mean(qh * qh, axis=0, keepdims=True)
            qh = qh * lax.rsqrt(ms + EPS) * qng_ref[...]
            x1, x2 = qh[:half, :], qh[half:, :]
            c, s = cost_ref[:, rows], sint_ref[:, rows]
            base = hh * B_HEAD_DIM
            q_ref[0, j, base:base + half, :] = ((x1 * c - x2 * s) * att_scale).astype(BF16)
            q_ref[0, j, base + half:base + B_HEAD_DIM, :] = ((x1 * s + x2 * c) * att_scale).astype(BF16)

        v_ref[0, j] = proj_t(wt_ref[_F_BV:_F_BZ, :]).astype(BF16)
        zb_ref[0, j] = jax.nn.silu(proj_t(wt_ref[_F_BZ:_F_IQ, :])).astype(BF16)

        qit = proj_t(wt_ref[_F_IQ:_F_END, :])
        for hh in range(IDX_HEADS):
            xh = qit[hh * IDX_DIM:(hh + 1) * IDX_DIM, :]
            x1, x2 = xh[:ihalf, :], xh[ihalf:, :]
            c, s = cosit_ref[:, rows], sinit_ref[:, rows]
            base = hh * IDX_DIM
            qi_ref[0, j, base:base + ihalf, :] = (x1 * c - x2 * s).astype(BF16)
            qi_ref[0, j, base + ihalf:base + IDX_DIM, :] = (x1 * s + x2 * c).astype(BF16)

        wi_ref[0, j] = proj_t(ww_ref[...])


def _layer_spec(arr, layer):
    zeros = (0,) * (arr.ndim - 1)
    return pl.BlockSpec((None,) + arr.shape[1:], lambda i: (layer,) + zeros)


def _proj(x2d, layer, ng, w_std, gb, lng, lnb, ws, bias_full, kng, woa, cos2, sin2, cosi2, sini2,
          w_tr, w_trw, qng, cos_t, sin_t, cosi_t, sini_t, seq):
    m, d = x2d.shape
    tm = TM_STD
    nt = seq // tm
    b, nq, per = m // seq, seq // QB, tm // QB
    full = lambda arr: _layer_spec(arr, layer)
    tab = lambda: pl.BlockSpec((tm, LANES), lambda i: (i % nt, 0))
    tabt = lambda rows: pl.BlockSpec((rows, tm), lambda i: (0, i % nt))
    row = lambda width: pl.BlockSpec((tm, width), lambda i: (i, 0))
    outt = lambda rows: pl.BlockSpec((1, per, rows, QB), lambda i: (i // nt, i % nt, 0, 0))
    wrows = w_trw.shape[1]
    feat = lambda rows, dtype: jax.ShapeDtypeStruct((b, nq, rows, QB), dtype)
    return pl.pallas_call(
        _proj_body,
        grid=(m // tm,),
        in_specs=[row(d), full(ng), full(w_std), full(gb), full(lng),
                  full(lnb), full(ws), full(bias_full), full(kng),
                  full(woa), tab(), tab(), tab(), tab(),
                  full(w_tr), full(w_trw), full(qng),
                  tabt(B_HEAD_DIM // 2), tabt(B_HEAD_DIM // 2), tabt(IDX_DIM // 2), tabt(IDX_DIM // 2)],
        out_specs=[row(d), row(d), row(B_WIDTH), row(IDX_DIM),
                   outt(B_WIDTH), outt(B_WIDTH), outt(B_WIDTH), outt(IDX_HEADS * IDX_DIM), outt(wrows)],
        out_shape=[jax.ShapeDtypeStruct((m, d), BF16), jax.ShapeDtypeStruct((m, d), BF16),
                   jax.ShapeDtypeStruct((m, B_WIDTH), BF16), jax.ShapeDtypeStruct((m, IDX_DIM), BF16),
                   feat(B_WIDTH, BF16), feat(B_WIDTH, BF16), feat(B_WIDTH, BF16),
                   feat(IDX_HEADS * IDX_DIM, BF16), feat(wrows, F32)],
        compiler_params=pltpu.CompilerParams(dimension_semantics=("arbitrary",),
                                             vmem_limit_bytes=VMEM_LIMIT_BYTES),
        name="proj",
    )(x2d, ng, w_std, gb, lng, lnb, ws, bias_full, kng, woa, cos2, sin2, cosi2, sini2,
      w_tr, w_trw, qng, cos_t, sin_t, cosi_t, sini_t)


def _slab_reduce(x, op):
    parts = [x[r:r + SUBLANES, :] for r in range(0, x.shape[0], SUBLANES)]
    while len(parts) > 1:
        parts = [op(parts[i], parts[i + 1]) for i in range(0, len(parts), 2)]
    return parts[0]


def _attn_body(q_ref, qi_ref, wi_ref, zb_ref, k_ref, ki_ref, v_ref, y_ref,
               s_scr, r_scr, d_scr, att_scr, acc_scr, *, topk):
    n = pl.program_id(1)
    ntiles = n + 1
    idx_scale = (IDX_DIM ** -0.5) * (IDX_HEADS ** -0.5)
    rows_per_tile = TK // SUBLANES

    def score_tile(t, lo8, hi8, diagonal):
        ki_t = ki_ref[0, pl.ds(pl.multiple_of(t * TK, TK), TK), :]
        acc = jnp.zeros((TK, QB), F32)
        for hh in range(IDX_HEADS):
            logit = jnp.dot(ki_t, qi_ref[0, 0, hh * IDX_DIM:(hh + 1) * IDX_DIM, :],
                            preferred_element_type=F32)
            acc = acc + wi_ref[0, 0, hh:hh + 1, :] * jnp.maximum(logit, 0.0)
        sc = acc * idx_scale
        sc_for_min = sc
        if diagonal:
            kc = lax.broadcasted_iota(jnp.int32, (TK, QB), 0) // CHUNK
            qc = lax.broadcasted_iota(jnp.int32, (TK, QB), 1) // CHUNK
            admissible = kc <= qc
            sc_for_min = jnp.where(admissible, sc, jnp.inf)
            sc = jnp.where(admissible, sc, -jnp.inf)
        s_scr[t] = sc
        return (jnp.minimum(lo8, _slab_reduce(sc_for_min, jnp.minimum)),
                jnp.maximum(hi8, _slab_reduce(sc, jnp.maximum)))

    def tiles_grouped(count, tile_fn, carry):
        def quad(p, c):
            for j in range(4):
                c = tile_fn(4 * p + j, c)
            return c
        carry = lax.fori_loop(0, lax.shift_right_logical(count, 2), quad, carry)
        base = count & ~3
        carry = lax.cond((count & 2) == 2, lambda c: tile_fn(base + 1, tile_fn(base, c)), lambda c: c, carry)
        return lax.cond((count & 1) == 1, lambda c: tile_fn(count - 1, c), lambda c: c, carry)

    lo8, hi8 = tiles_grouped(
        n, lambda t, c: score_tile(t, c[0], c[1], False),
        (jnp.full((SUBLANES, QB), jnp.inf, F32), jnp.full((SUBLANES, QB), -jnp.inf, F32)))
    lo8, hi8 = score_tile(n, lo8, hi8, True)
    s_lo = jnp.min(lo8, axis=0, keepdims=True)
    s_hi = jnp.max(hi8, axis=0, keepdims=True)

    span = s_hi - s_lo
    rank_scale = jnp.where(span > 0.0, RANK_MAX / jnp.where(span > 0.0, span, 1.0), 0.0)

    def rank_tile(t, carry):
        r = jnp.minimum(jnp.floor((s_scr[t] - s_lo) * rank_scale), RANK_MAX)
        r = jnp.maximum(r, -1.0)
        r_scr[t] = r
        d1 = jnp.floor(r * (1.0 / DIGIT_BASE ** 2))
        rem = r - d1 * float(DIGIT_BASE ** 2)
        d2 = jnp.floor(rem * (1.0 / DIGIT_BASE))
        d_scr[0, t] = d1.astype(BF16)
        d_scr[1, t] = d2.astype(BF16)
        d_scr[2, t] = (rem - d2 * float(DIGIT_BASE)).astype(BF16)
        return carry

    lax.fori_loop(0, ntiles, rank_tile, 0)

    @pl.when((ntiles & 1) == 1)
    def _():
        for plane in range(3):
            d_scr[plane, ntiles] = jnp.full((TK, QB), -1.0, BF16)

    npairs = lax.shift_right_logical(ntiles + 1, 1)
    packed_rows = 2 * SUBLANES
    slabs_per_tile = TK // packed_rows
    one_bf, zero_bf = jnp.ones((), BF16), jnp.zeros((), BF16)

    def count_ge(plane, cand):
        cand_b = jnp.broadcast_to(cand.astype(BF16), (packed_rows, QB))

        def pair_body(p, accs):
            accs = list(accs)
            for half in range(2):
                for r in range(slabs_per_tile):
                    sl = d_scr[plane, 2 * p + half, r * packed_rows:(r + 1) * packed_rows, :]
                    c = r % COUNT_CHAINS
                    accs[c] = accs[c] + jnp.where(sl >= cand_b, one_bf, zero_bf)
            return tuple(accs)

        accs = lax.fori_loop(0, npairs, pair_body,
                             tuple(jnp.zeros((packed_rows, QB), BF16) for _ in range(COUNT_CHAINS)))
        total = functools.reduce(jnp.add, [a.astype(F32) for a in accs])
        return jnp.sum(total, axis=0, keepdims=True)

    def digit_search(plane, cnt_at):
        def bit_body(i, carry):
            prefix, cnt_at, step = carry
            trial = prefix + step
            cnt = count_ge(plane, trial)
            ok = cnt >= float(topk)
            return jnp.where(ok, trial, prefix), jnp.where(ok, cnt, cnt_at), step * 0.5
        digit, cnt_at, _ = lax.fori_loop(
            0, DIGIT_BITS, bit_body,
            (jnp.zeros((1, QB), F32), cnt_at, jnp.full((1, QB), DIGIT_BASE / 2.0, F32)))
        return digit, cnt_at

    def restrict_plane(plane, digit):
        digit_b = jnp.broadcast_to(digit.astype(BF16), (TK, QB))
        above, below = jnp.full((), float(DIGIT_BASE), BF16), jnp.full((), -1.0, BF16)

        def tile_body(t, carry):
            prev = d_scr[plane, t]
            d_scr[plane + 1, t] = jnp.where(prev > digit_b, above,
                                            jnp.where(prev < digit_b, below, d_scr[plane + 1, t]))
            return carry

        lax.fori_loop(0, ntiles, tile_body, 0)

    q_chunk = (n * QB + lax.broadcasted_iota(jnp.int32, (1, QB), 1)) // CHUNK
    n_admissible = ((q_chunk + 1) * CHUNK).astype(F32)
    digit1, cnt_at = digit_search(0, n_admissible)
    restrict_plane(0, digit1)
    digit2, cnt_at = digit_search(1, cnt_at)
    restrict_plane(1, digit2)
    digit3, cnt_at = digit_search(2, cnt_at)
    thr = (digit1 * float(DIGIT_BASE) + digit2) * float(DIGIT_BASE) + digit3

    excess0 = jnp.maximum(cnt_at - float(topk), 0.0)

    @pl.when(jnp.max(excess0) > 0.0)
    def _():
        thr_b = jnp.broadcast_to(thr, (SUBLANES, QB))
        later_or_same = (lax.broadcasted_iota(jnp.int32, (TK, TK), 1)
                         >= lax.broadcasted_iota(jnp.int32, (TK, TK), 0))
        suffix_ones = jnp.where(later_or_same, 1.0, 0.0).astype(BF16)

        def slab(r):
            return slice(r * SUBLANES, (r + 1) * SUBLANES)

        def drop_group(excess):
            def min_tile(t, ms):
                ms = list(ms)
                for r in range(rows_per_tile):
                    selected = r_scr[t, slab(r), :] >= thr_b
                    c = r % COUNT_CHAINS
                    ms[c] = jnp.minimum(ms[c], jnp.where(selected, s_scr[t, slab(r), :], jnp.inf))
                return tuple(ms)

            ms = lax.fori_loop(0, ntiles, min_tile,
                               tuple(jnp.full((SUBLANES, QB), jnp.inf, F32) for _ in range(COUNT_CHAINS)))
            m8 = functools.reduce(jnp.minimum, ms)
            m_row = jnp.min(m8, axis=0, keepdims=True)

            def drop_tile(i, later):
                t = ntiles - 1 - i
                r = r_scr[t]
                in_group = (r >= thr) & (s_scr[t] == m_row)
                member = jnp.where(in_group, 1.0, 0.0).astype(BF16)
                suffix = jnp.dot(suffix_ones, member, preferred_element_type=F32) + later
                r_scr[t] = jnp.where(in_group & (suffix <= excess), -1.0, r)
                return suffix[0:1, :]

            group_size = lax.fori_loop(0, ntiles, drop_tile, jnp.zeros((1, QB), F32))
            return excess - jnp.minimum(excess, group_size)

        lax.while_loop(lambda e: jnp.max(e) > 0.0, drop_group, excess0)

    slab_reduce = _slab_reduce

    def logit_tile(t, m8):
        sel = r_scr[t] >= thr
        krows = pl.ds(pl.multiple_of(t * TK, TK), TK)
        new_m8 = []
        for hh in range(B_HEADS):
            hs = slice(hh * B_HEAD_DIM, (hh + 1) * B_HEAD_DIM)
            att = jnp.dot(k_ref[0, krows, hs], q_ref[0, 0, hs, :], preferred_element_type=F32)
            att = jnp.where(sel, att, NEG_BIG)
            att_scr[t * B_HEADS + hh] = att
            new_m8.append(jnp.maximum(m8[hh], slab_reduce(att, jnp.maximum)))
        return tuple(new_m8)

    m8 = tiles_grouped(ntiles, logit_tile,
                        tuple(jnp.full((SUBLANES, QB), NEG_BIG, F32) for _ in range(B_HEADS)))
    m_row = [jnp.max(m, axis=0, keepdims=True) for m in m8]

    acc_scr[...] = jnp.zeros_like(acc_scr)

    def value_tile(t, l8):
        new_l8 = []
        for hh in range(B_HEADS):
            hs = slice(hh * B_HEAD_DIM, (hh + 1) * B_HEAD_DIM)
            p = jnp.exp(att_scr[t * B_HEADS + hh] - m_row[hh])
            new_l8.append(l8[hh] + slab_reduce(p, jnp.add))
            acc_scr[hs, :] += jnp.dot(v_ref[0, t, hs, :], p.astype(BF16), preferred_element_type=F32)
        return tuple(new_l8)

    l8 = tiles_grouped(ntiles, value_tile,
                        tuple(jnp.zeros((SUBLANES, QB), F32) for _ in range(B_HEADS)))

    parts = []
    for hh in range(B_HEADS):
        hs = slice(hh * B_HEAD_DIM, (hh + 1) * B_HEAD_DIM)
        l_row = jnp.sum(l8[hh], axis=0, keepdims=True)
        parts.append(acc_scr[hs, :] / l_row * zb_ref[0, 0, hs, :].astype(F32))
    y_ref[0] = jnp.concatenate(parts, axis=0).T.astype(BF16)


def _attn(qT, qiT, wiT, zbT, k, ki, vT, topk):
    b, nq = qT.shape[0], qT.shape[1]
    s = k.shape[1]
    blk = lambda rows: pl.BlockSpec((1, 1, rows, QB), lambda bi, i: (bi, i, 0, 0))
    return pl.pallas_call(
        functools.partial(_attn_body, topk=topk),
        grid=(b, nq),
        in_specs=[blk(B_WIDTH), blk(IDX_HEADS * IDX_DIM), blk(wiT.shape[2]), blk(B_WIDTH),
                  pl.BlockSpec((1, s, B_WIDTH), lambda bi, i: (bi, 0, 0), pipeline_mode=pl.Buffered(1)),
                  pl.BlockSpec((1, s, IDX_DIM), lambda bi, i: (bi, 0, 0), pipeline_mode=pl.Buffered(1)),
                  pl.BlockSpec((1, nq, B_WIDTH, QB), lambda bi, i: (bi, 0, 0, 0),
                               pipeline_mode=pl.Buffered(1))],
        out_specs=pl.BlockSpec((1, QB, B_WIDTH), lambda bi, i: (bi, i, 0)),
        out_shape=jax.ShapeDtypeStruct((b, s, B_WIDTH), BF16),
        scratch_shapes=[pltpu.VMEM((s // TK, TK, QB), F32),
                        pltpu.VMEM((s // TK, TK, QB), F32),
                        pltpu.VMEM((3, s // TK, TK, QB), BF16),
                        pltpu.VMEM((s // TK * B_HEADS, TK, QB), F32),
                        pltpu.VMEM((B_WIDTH, QB), F32)],
        compiler_params=pltpu.CompilerParams(dimension_semantics=("arbitrary", "arbitrary"),
                                             vmem_limit_bytes=VMEM_LIMIT_BYTES),
        name="attn",
    )(qT, qiT, wiT, zbT, k, ki, vT)


def _out_body(x_ref, yb_ref, ma_ref, sgb_ref, wob_ref, wout_ref, o_ref):
    o_b = jnp.dot(yb_ref[...], wob_ref[...], preferred_element_type=F32)
    merged = ma_ref[...].astype(F32) + sgb_ref[...].astype(F32) * o_b
    o_ref[...] = x_ref[...] + jnp.dot(merged.astype(BF16), wout_ref[...], preferred_element_type=F32)


def _out(x2d, layer, yb, ma, sgb, wob, wout):
    m, d = x2d.shape
    tm = TM_OUT
    row = lambda width: pl.BlockSpec((tm, width), lambda i: (i, 0))
    full = lambda arr: _layer_spec(arr, layer)
    return pl.pallas_call(
        _out_body,
        grid=(m // tm,),
        in_specs=[row(d), row(B_WIDTH), row(d), row(d), full(wob), full(wout)],
        out_specs=row(d),
        out_shape=jax.ShapeDtypeStruct((m, d), F32),
        compiler_params=pltpu.CompilerParams(dimension_semantics=("arbitrary",),
                                             vmem_limit_bytes=VMEM_LIMIT_BYTES),
        name="out_proj",
    )(x2d, yb, ma, sgb, wob, wout)


def _rope_tables(s, dim):
    pos = jnp.arange(s, dtype=F32)
    inv = ROPE_THETA ** (-jnp.arange(0, dim, 2, dtype=F32) / dim)
    ang = pos[:, None] * inv[None, :]
    return jnp.cos(ang), jnp.sin(ang)


def kernel(x, norm_g, w_in, gate_b, a_ln_g, a_ln_b, a_ws, a_bs, q_norm_g, k_norm_g, w_oa, w_ob, w_out):
    b, s, d = x.shape
    depth = w_in.shape[0]
    topk = min(TOPK_MAX, s // 4)
    assert s % TM_STD == 0 and s % QB == 0 and (b * s) % TM_OUT == 0 and QB == TK and TM_STD_SUB == QB
    assert w_in.shape[2] == _O_END and topk <= TK

    cos_q, sin_q = _rope_tables(s, B_HEAD_DIM)
    cos_i, sin_i = _rope_tables(s, IDX_DIM)
    cos2 = jnp.concatenate([cos_q, cos_q], axis=1)
    sin2 = jnp.concatenate([-sin_q, sin_q], axis=1)
    cosi2 = jnp.concatenate([cos_i, cos_i, cos_i, cos_i], axis=1)
    sini2 = jnp.concatenate([-sin_i, sin_i, -sin_i, sin_i], axis=1)
    cos_t, sin_t, cosi_t, sini_t = cos_q.T, sin_q.T, cos_i.T, sin_i.T

    w_t = jnp.swapaxes(w_in, 1, 2)
    ik_rows = w_t[:, _O_IK:_O_IW]
    w_std = jnp.concatenate([w_t[:, _O_AU:_O_BQ], w_t[:, _O_BK:_O_BV], ik_rows, ik_rows,
                             w_t[:, _O_GA:_O_END]], axis=1).astype(BF16)
    w_tr = jnp.concatenate([w_t[:, _O_BQ:_O_BK], w_t[:, _O_BV:_O_IK]], axis=1).astype(BF16)
    iw_t = w_t[:, _O_IW:_O_GA]
    w_trw = jnp.concatenate([iw_t, jnp.zeros_like(iw_t)], axis=1).astype(BF16)
    woa, wob, wout = w_oa.astype(BF16), w_ob.astype(BF16), w_out.astype(BF16)
    bias_full = jnp.repeat(jnp.swapaxes(a_bs, 1, 2), A_WIDTH // A_GROUPS, axis=2)
    kng = jnp.reshape(k_norm_g, (depth, 1, B_HEAD_DIM))
    qng = jnp.reshape(q_norm_g, (depth, B_HEAD_DIM, 1))
    ng = jnp.reshape(norm_g, (depth, 1, d))
    lng = jnp.reshape(a_ln_g, (depth, 1, A_WIDTH))
    lnb = jnp.reshape(a_ln_b, (depth, 1, A_WIDTH))

    x2d = jnp.reshape(x, (b * s, d))
    for l in range(depth):
        ma, sgb, k, ki, qT, vT, zbT, qiT, wiT = _proj(
            x2d, l, ng, w_std, gate_b, lng, lnb, a_ws, bias_full, kng, woa, cos2, sin2, cosi2, sini2,
            w_tr, w_trw, qng, cos_t, sin_t, cosi_t, sini_t, s)
        yb = _attn(qT, qiT, wiT, zbT, jnp.reshape(k, (b, s, B_WIDTH)), jnp.reshape(ki, (b, s, IDX_DIM)),
                   vT, topk)
        x2d = _out(x2d, l, jnp.reshape(yb, (b * s, B_WIDTH)), ma, sgb, wob, wout)
    return jnp.reshape(x2d, (b, s, d))
```

```python
import functools

import jax
import jax.numpy as jnp
from jax import lax
from jax.experimental import pallas as pl
from jax.experimental.pallas import tpu as pltpu

F32 = jnp.float32
BF16 = jnp.bfloat16

CHUNK = 64
EPS = 1e-6
ROPE_THETA = 10000.0

A_WIDTH = 512
A_GROUPS = 4
A_BLOCK = 128
B_HEADS = 4
B_HEAD_DIM = 128
B_WIDTH = B_HEADS * B_HEAD_DIM
IDX_HEADS = 8
IDX_DIM = 64
TOPK_MAX = 256

_O_AU, _O_AV, _O_AZ = 0, 512, 1024
_O_BQ, _O_BK, _O_BV, _O_BZ = 1536, 2048, 2560, 3072
_O_IQ, _O_IK, _O_IW = 3584, 4096, 4160
_O_GA, _O_GB, _O_END = 4168, 5192, 6216

_P_AU, _P_AV, _P_AZ, _P_BK, _P_IK, _P_GA, _P_GB, _P_END = 0, 512, 1024, 1536, 2048, 2176, 3200, 4224
_F_BQ, _F_BV, _F_BZ, _F_IQ, _F_END = 0, 512, 1024, 1536, 2048

LANES = 128
SUBLANES = 8
VMEM_LIMIT_BYTES = 52 * 1024 * 1024

QB = 256
TK = 256
TM_STD = 512
TM_STD_SUB = 256
TM_OUT = 512
NEG_BIG = -1e30
DIGIT_BITS = 8
DIGIT_BASE = 2 ** DIGIT_BITS
RANK_MAX = float(DIGIT_BASE ** 3 - 1)
COUNT_CHAINS = 4
TILE_GROUPS = (8, 4, 2, 1)


def _rms_rows(x, g):
    ms = jnp.mean(x * x, axis=-1, keepdims=True)
    return x * lax.rsqrt(ms + EPS) * g


def _proj_body(x_ref, ng_ref, w_ref, gb_ref, lng_ref, lnb_ref, ws_ref, bias_ref, kng_ref,
               woa_ref, cos_ref, sin_ref, cosi_ref, sini_ref,
               wt_ref, ww_ref, qng_ref, cost_ref, sint_ref, cosit_ref, sinit_ref,
               ma_ref, sgb_ref, k_ref, ki_ref, q_ref, v_ref, zb_ref, qi_ref, wi_ref):
    tm = x_ref.shape[0]
    att_scale = B_HEAD_DIM ** -0.5
    half = B_HEAD_DIM // 2
    ihalf = IDX_DIM // 2
    ci = lax.broadcasted_iota(jnp.int32, (A_BLOCK, A_BLOCK), 0) // CHUNK
    cj = lax.broadcasted_iota(jnp.int32, (A_BLOCK, A_BLOCK), 1) // CHUNK
    causal = cj <= ci
    wm = [jnp.where(causal, ws_ref[g], 0.0).astype(BF16) for g in range(A_GROUPS)]

    for r0 in range(0, tm, TM_STD_SUB):
        rows = slice(r0, r0 + TM_STD_SUB)
        h = _rms_rows(x_ref[rows, :], ng_ref[...]).astype(BF16)

        def proj(lo, hi, h=h):
            return lax.dot_general(h, w_ref[lo:hi, :], (((1,), (1,)), ((), ())), preferred_element_type=F32)

        gv = jax.nn.gelu(proj(_P_AV, _P_AZ))
        mu = jnp.mean(gv, axis=-1, keepdims=True)
        xc = gv - mu
        var = jnp.mean(xc * xc, axis=-1, keepdims=True)
        vn = (xc * lax.rsqrt(var + EPS) * lng_ref[...] + lnb_ref[...]).astype(BF16)

        row_blocks = []
        for r in range(TM_STD_SUB // A_BLOCK):
            cols = []
            for g in range(A_GROUPS):
                vb = vn[r * A_BLOCK:(r + 1) * A_BLOCK, g * LANES:(g + 1) * LANES]
                cols.append(jnp.dot(wm[g], vb, preferred_element_type=F32))
            row_blocks.append(jnp.concatenate(cols, axis=1) + bias_ref[...])
        mixed = jnp.concatenate(row_blocks, axis=0)

        y_a = jax.nn.gelu(proj(_P_AU, _P_AV)) * mixed * jax.nn.silu(proj(_P_AZ, _P_BK))
        o_a = jnp.dot(y_a.astype(BF16), woa_ref[...], preferred_element_type=F32)
        ma_ref[rows, :] = (jax.nn.sigmoid(proj(_P_GA, _P_GB) + gb_ref[0:1, :]) * o_a).astype(BF16)
        sgb_ref[rows, :] = jax.nn.sigmoid(proj(_P_GB, _P_END) + gb_ref[1:2, :]).astype(BF16)

        b_k = proj(_P_BK, _P_IK)
        for hh in range(B_HEADS):
            kh = _rms_rows(b_k[:, hh * LANES:(hh + 1) * LANES], kng_ref[...])
            kh = kh * cos_ref[rows, :] + pltpu.roll(kh, B_HEAD_DIM // 2, 1) * sin_ref[rows, :]
            k_ref[rows, hh * LANES:(hh + 1) * LANES] = kh.astype(BF16)

        ik = proj(_P_IK, _P_GA)
        ik = ik * cosi_ref[rows, :] + pltpu.roll(ik, IDX_DIM // 2, 1) * sini_ref[rows, :]
        ki_ref[rows, :] = ik[:, :IDX_DIM].astype(BF16)

        j = r0 // QB

        def proj_t(w, h=h):
            return lax.dot_general(w, h, (((1,), (1,)), ((), ())), preferred_element_type=F32)

        qt = proj_t(wt_ref[_F_BQ:_F_BV, :])
        for hh in range(B_HEADS):
            qh = qt[hh * B_HEAD_DIM:(hh + 1) * B_HEAD_DIM, :]
            ms = jnp.mean(qh * qh, axis=0, keepdims=True)
            qh = qh * lax.rsqrt(ms + EPS) * qng_ref[...]
            x1, x2 = qh[:half, :], qh[half:, :]
            c, s = cost_ref[:, rows], sint_ref[:, rows]
            base = hh * B_HEAD_DIM
            q_ref[0, j, base:base + half, :] = ((x1 * c - x2 * s) * att_scale).astype(BF16)
            q_ref[0, j, base + half:base + B_HEAD_DIM, :] = ((x1 * s + x2 * c) * att_scale).astype(BF16)

        v_ref[0, j] = proj_t(wt_ref[_F_BV:_F_BZ, :]).astype(BF16)
        zb_ref[0, j] = jax.nn.silu(proj_t(wt_ref[_F_BZ:_F_IQ, :])).astype(BF16)

        qit = proj_t(wt_ref[_F_IQ:_F_END, :])
        for hh in range(IDX_HEADS):
            xh = qit[hh * IDX_DIM:(hh + 1) * IDX_DIM, :]
            x1, x2 = xh[:ihalf, :], xh[ihalf:, :]
            c, s = cosit_ref[:, rows], sinit_ref[:, rows]
            base = hh * IDX_DIM
            qi_ref[0, j, base:base + ihalf, :] = (x1 * c - x2 * s).astype(BF16)
            qi_ref[0, j, base + ihalf:base + IDX_DIM, :] = (x1 * s + x2 * c).astype(BF16)

        wi_ref[0, j] = proj_t(ww_ref[...])


def _layer_spec(arr, layer):
    zeros = (0,) * (arr.ndim - 1)
    return pl.BlockSpec((None,) + arr.shape[1:], lambda i: (layer,) + zeros)


def _proj(x2d, layer, ng, w_std, gb, lng, lnb, ws, bias_full, kng, woa, cos2, sin2, cosi2, sini2,
          w_tr, w_trw, qng, cos_t, sin_t, cosi_t, sini_t, seq):
    m, d = x2d.shape
    tm = TM_STD
    nt = seq // tm
    b, nq, per = m // seq, seq // QB, tm // QB
    full = lambda arr: _layer_spec(arr, layer)
    tab = lambda: pl.BlockSpec((tm, LANES), lambda i: (i % nt, 0))
    tabt = lambda rows: pl.BlockSpec((rows, tm), lambda i: (0, i % nt))
    row = lambda width: pl.BlockSpec((tm, width), lambda i: (i, 0))
    outt = lambda rows: pl.BlockSpec((1, per, rows, QB), lambda i: (i // nt, i % nt, 0, 0))
    wrows = w_trw.shape[1]
    feat = lambda rows, dtype: jax.ShapeDtypeStruct((b, nq, rows, QB), dtype)
    return pl.pallas_call(
        _proj_body,
        grid=(m // tm,),
        in_specs=[row(d), full(ng), full(w_std), full(gb), full(lng),
                  full(lnb), full(ws), full(bias_full), full(kng),
                  full(woa), tab(), tab(), tab(), tab(),
                  full(w_tr), full(w_trw), full(qng),
                  tabt(B_HEAD_DIM // 2), tabt(B_HEAD_DIM // 2), tabt(IDX_DIM // 2), tabt(IDX_DIM // 2)],
        out_specs=[row(d), row(d), row(B_WIDTH), row(IDX_DIM),
                   outt(B_WIDTH), outt(B_WIDTH), outt(B_WIDTH), outt(IDX_HEADS * IDX_DIM), outt(wrows)],
        out_shape=[jax.ShapeDtypeStruct((m, d), BF16), jax.ShapeDtypeStruct((m, d), BF16),
                   jax.ShapeDtypeStruct((m, B_WIDTH), BF16), jax.ShapeDtypeStruct((m, IDX_DIM), BF16),
                   feat(B_WIDTH, BF16), feat(B_WIDTH, BF16), feat(B_WIDTH, BF16),
                   feat(IDX_HEADS * IDX_DIM, BF16), feat(wrows, F32)],
        compiler_params=pltpu.CompilerParams(dimension_semantics=("arbitrary",),
                                             vmem_limit_bytes=VMEM_LIMIT_BYTES),
        name="proj",
    )(x2d, ng, w_std, gb, lng, lnb, ws, bias_full, kng, woa, cos2, sin2, cosi2, sini2,
      w_tr, w_trw, qng, cos_t, sin_t, cosi_t, sini_t)


def _slab_reduce(x, op):
    parts = [x[r:r + SUBLANES, :] for r in range(0, x.shape[0], SUBLANES)]
    while len(parts) > 1:
        parts = [op(parts[i], parts[i + 1]) for i in range(0, len(parts), 2)]
    return parts[0]


def _attn_body(q_ref, qi_ref, wi_ref, zb_ref, k_ref, ki_ref, v_ref, y_ref,
               s_scr, r_scr, d_scr, att_scr, acc_scr, *, topk):
    n = pl.program_id(1)
    ntiles = n + 1
    idx_scale = (IDX_DIM ** -0.5) * (IDX_HEADS ** -0.5)
    rows_per_tile = TK // SUBLANES

    def score_tile(t, lo8, hi8, diagonal):
        ki_t = ki_ref[0, pl.ds(pl.multiple_of(t * TK, TK), TK), :]
        acc = jnp.zeros((TK, QB), F32)
        for hh in range(IDX_HEADS):
            logit = jnp.dot(ki_t, qi_ref[0, 0, hh * IDX_DIM:(hh + 1) * IDX_DIM, :],
                            preferred_element_type=F32)
            acc = acc + wi_ref[0, 0, hh:hh + 1, :] * jnp.maximum(logit, 0.0)
        sc = acc * idx_scale
        sc_for_min = sc
        if diagonal:
            kc = lax.broadcasted_iota(jnp.int32, (TK, QB), 0) // CHUNK
            qc = lax.broadcasted_iota(jnp.int32, (TK, QB), 1) // CHUNK
            admissible = kc <= qc
            sc_for_min = jnp.where(admissible, sc, jnp.inf)
            sc = jnp.where(admissible, sc, -jnp.inf)
        s_scr[t] = sc
        return (jnp.minimum(lo8, _slab_reduce(sc_for_min, jnp.minimum)),
                jnp.maximum(hi8, _slab_reduce(sc, jnp.maximum)))

    def tiles_grouped(count, tile_fn, carry):
        def group(width, first, c):
            for j in range(width):
                c = tile_fn(first + j, c)
            return c

        widest = TILE_GROUPS[0]
        carry = lax.fori_loop(0, count // widest, lambda p, c: group(widest, widest * p, c), carry)
        for width in TILE_GROUPS[1:]:
            first = count & ~(2 * width - 1)
            carry = lax.cond((count & width) == width, functools.partial(group, width, first),
                             lambda c: c, carry)
        return carry

    lo8, hi8 = tiles_grouped(
        n, lambda t, c: score_tile(t, c[0], c[1], False),
        (jnp.full((SUBLANES, QB), jnp.inf, F32), jnp.full((SUBLANES, QB), -jnp.inf, F32)))
    lo8, hi8 = score_tile(n, lo8, hi8, True)
    s_lo = jnp.min(lo8, axis=0, keepdims=True)
    s_hi = jnp.max(hi8, axis=0, keepdims=True)

    span = s_hi - s_lo
    rank_scale = jnp.where(span > 0.0, RANK_MAX / jnp.where(span > 0.0, span, 1.0), 0.0)

    def rank_tile(t, carry):
        r = jnp.minimum(jnp.floor((s_scr[t] - s_lo) * rank_scale), RANK_MAX)
        r = jnp.maximum(r, -1.0)
        r_scr[t] = r
        d1 = jnp.floor(r * (1.0 / DIGIT_BASE ** 2))
        rem = r - d1 * float(DIGIT_BASE ** 2)
        d2 = jnp.floor(rem * (1.0 / DIGIT_BASE))
        d_scr[0, t] = d1.astype(BF16)
        d_scr[1, t] = d2.astype(BF16)
        d_scr[2, t] = (rem - d2 * float(DIGIT_BASE)).astype(BF16)
        return carry

    lax.fori_loop(0, ntiles, rank_tile, 0)

    @pl.when((ntiles & 1) == 1)
    def _():
        for plane in range(3):
            d_scr[plane, ntiles] = jnp.full((TK, QB), -1.0, BF16)

    npairs = lax.shift_right_logical(ntiles + 1, 1)
    packed_rows = 2 * SUBLANES
    slabs_per_tile = TK // packed_rows
    one_bf, zero_bf = jnp.ones((), BF16), jnp.zeros((), BF16)

    def count_ge(plane, cand):
        cand_b = jnp.broadcast_to(cand.astype(BF16), (packed_rows, QB))

        def pair_body(p, accs):
            accs = list(accs)
            for half in range(2):
                for r in range(slabs_per_tile):
                    sl = d_scr[plane, 2 * p + half, r * packed_rows:(r + 1) * packed_rows, :]
                    c = r % COUNT_CHAINS
                    accs[c] = accs[c] + jnp.where(sl >= cand_b, one_bf, zero_bf)
            return tuple(accs)

        accs = lax.fori_loop(0, npairs, pair_body,
                             tuple(jnp.zeros((packed_rows, QB), BF16) for _ in range(COUNT_CHAINS)))
        total = functools.reduce(jnp.add, [a.astype(F32) for a in accs])
        return jnp.sum(total, axis=0, keepdims=True)

    def digit_search(plane, cnt_at):
        def bit_body(i, carry):
            prefix, cnt_at, step = carry
            trial = prefix + step
            cnt = count_ge(plane, trial)
            ok = cnt >= float(topk)
            return jnp.where(ok, trial, prefix), jnp.where(ok, cnt, cnt_at), step * 0.5
        digit, cnt_at, _ = lax.fori_loop(
            0, DIGIT_BITS, bit_body,
            (jnp.zeros((1, QB), F32), cnt_at, jnp.full((1, QB), DIGIT_BASE / 2.0, F32)))
        return digit, cnt_at

    def restrict_plane(plane, digit):
        digit_b = jnp.broadcast_to(digit.astype(BF16), (TK, QB))
        above, below = jnp.full((), float(DIGIT_BASE), BF16), jnp.full((), -1.0, BF16)

        def tile_body(t, carry):
            prev = d_scr[plane, t]
            d_scr[plane + 1, t] = jnp.where(prev > digit_b, above,
                                            jnp.where(prev < digit_b, below, d_scr[plane + 1, t]))
            return carry

        lax.fori_loop(0, ntiles, tile_body, 0)

    q_chunk = (n * QB + lax.broadcasted_iota(jnp.int32, (1, QB), 1)) // CHUNK
    n_admissible = ((q_chunk + 1) * CHUNK).astype(F32)
    digit1, cnt_at = digit_search(0, n_admissible)
    restrict_plane(0, digit1)
    digit2, cnt_at = digit_search(1, cnt_at)
    restrict_plane(1, digit2)
    digit3, cnt_at = digit_search(2, cnt_at)
    thr = (digit1 * float(DIGIT_BASE) + digit2) * float(DIGIT_BASE) + digit3

    excess0 = jnp.maximum(cnt_at - float(topk), 0.0)

    @pl.when(jnp.max(excess0) > 0.0)
    def _():
        thr_b = jnp.broadcast_to(thr, (SUBLANES, QB))
        later_or_same = (lax.broadcasted_iota(jnp.int32, (TK, TK), 1)
                         >= lax.broadcasted_iota(jnp.int32, (TK, TK), 0))
        suffix_ones = jnp.where(later_or_same, 1.0, 0.0).astype(BF16)

        def slab(r):
            return slice(r * SUBLANES, (r + 1) * SUBLANES)

        def drop_group(excess):
            def min_tile(t, ms):
                ms = list(ms)
                for r in range(rows_per_tile):
                    selected = r_scr[t, slab(r), :] >= thr_b
                    c = r % COUNT_CHAINS
                    ms[c] = jnp.minimum(ms[c], jnp.where(selected, s_scr[t, slab(r), :], jnp.inf))
                return tuple(ms)

            ms = lax.fori_loop(0, ntiles, min_tile,
                               tuple(jnp.full((SUBLANES, QB), jnp.inf, F32) for _ in range(COUNT_CHAINS)))
            m8 = functools.reduce(jnp.minimum, ms)
            m_row = jnp.min(m8, axis=0, keepdims=True)

            def drop_tile(i, later):
                t = ntiles - 1 - i
                r = r_scr[t]
                in_group = (r >= thr) & (s_scr[t] == m_row)
                member = jnp.where(in_group, 1.0, 0.0).astype(BF16)
                suffix = jnp.dot(suffix_ones, member, preferred_element_type=F32) + later
                r_scr[t] = jnp.where(in_group & (suffix <= excess), -1.0, r)
                return suffix[0:1, :]

            group_size = lax.fori_loop(0, ntiles, drop_tile, jnp.zeros((1, QB), F32))
            return excess - jnp.minimum(excess, group_size)

        lax.while_loop(lambda e: jnp.max(e) > 0.0, drop_group, excess0)

    slab_reduce = _slab_reduce

    def logit_tile(t, m8):
        sel = r_scr[t] >= thr
        krows = pl.ds(pl.multiple_of(t * TK, TK), TK)
        new_m8 = []
        for hh in range(B_HEADS):
            hs = slice(hh * B_HEAD_DIM, (hh + 1) * B_HEAD_DIM)
            att = jnp.dot(k_ref[0, krows, hs], q_ref[0, 0, hs, :], preferred_element_type=F32)
            att = jnp.where(sel, att, NEG_BIG)
            att_scr[t * B_HEADS + hh] = att
            new_m8.append(jnp.maximum(m8[hh], slab_reduce(att, jnp.maximum)))
        return tuple(new_m8)

    m8 = tiles_grouped(ntiles, logit_tile,
                        tuple(jnp.full((SUBLANES, QB), NEG_BIG, F32) for _ in range(B_HEADS)))
    m_row = [jnp.max(m, axis=0, keepdims=True) for m in m8]

    acc_scr[...] = jnp.zeros_like(acc_scr)

    def value_tile(t, l8):
        new_l8 = []
        for hh in range(B_HEADS):
            hs = slice(hh * B_HEAD_DIM, (hh + 1) * B_HEAD_DIM)
            p = jnp.exp(att_scr[t * B_HEADS + hh] - m_row[hh])
            new_l8.append(l8[hh] + slab_reduce(p, jnp.add))
            acc_scr[hs, :] += jnp.dot(v_ref[0, t, hs, :], p.astype(BF16), preferred_element_type=F32)
        return tuple(new_l8)

    l8 = tiles_grouped(ntiles, value_tile,
                        tuple(jnp.zeros((SUBLANES, QB), F32) for _ in range(B_HEADS)))

    parts = []
    for hh in range(B_HEADS):
        hs = slice(hh * B_HEAD_DIM, (hh + 1) * B_HEAD_DIM)
        l_row = jnp.sum(l8[hh], axis=0, keepdims=True)
        parts.append(acc_scr[hs, :] / l_row * zb_ref[0, 0, hs, :].astype(F32))
    y_ref[0] = jnp.concatenate(parts, axis=0).T.astype(BF16)


def _attn(qT, qiT, wiT, zbT, k, ki, vT, topk):
    b, nq = qT.shape[0], qT.shape[1]
    s = k.shape[1]
    blk = lambda rows: pl.BlockSpec((1, 1, rows, QB), lambda bi, i: (bi, i, 0, 0))
    return pl.pallas_call(
        functools.partial(_attn_body, topk=topk),
        grid=(b, nq),
        in_specs=[blk(B_WIDTH), blk(IDX_HEADS * IDX_DIM), blk(wiT.shape[2]), blk(B_WIDTH),
                  pl.BlockSpec((1, s, B_WIDTH), lambda bi, i: (bi, 0, 0), pipeline_mode=pl.Buffered(1)),
                  pl.BlockSpec((1, s, IDX_DIM), lambda bi, i: (bi, 0, 0), pipeline_mode=pl.Buffered(1)),
                  pl.BlockSpec((1, nq, B_WIDTH, QB), lambda bi, i: (bi, 0, 0, 0),
                               pipeline_mode=pl.Buffered(1))],
        out_specs=pl.BlockSpec((1, QB, B_WIDTH), lambda bi, i: (bi, i, 0)),
        out_shape=jax.ShapeDtypeStruct((b, s, B_WIDTH), BF16),
        scratch_shapes=[pltpu.VMEM((s // TK, TK, QB), F32),
                        pltpu.VMEM((s // TK, TK, QB), F32),
                        pltpu.VMEM((3, s // TK, TK, QB), BF16),
                        pltpu.VMEM((s // TK * B_HEADS, TK, QB), F32),
                        pltpu.VMEM((B_WIDTH, QB), F32)],
        compiler_params=pltpu.CompilerParams(dimension_semantics=("arbitrary", "arbitrary"),
                                             vmem_limit_bytes=VMEM_LIMIT_BYTES),
        name="attn",
    )(qT, qiT, wiT, zbT, k, ki, vT)


def _out_body(x_ref, yb_ref, ma_ref, sgb_ref, wob_ref, wout_ref, o_ref):
    o_b = jnp.dot(yb_ref[...], wob_ref[...], preferred_element_type=F32)
    merged = ma_ref[...].astype(F32) + sgb_ref[...].astype(F32) * o_b
    o_ref[...] = x_ref[...] + jnp.dot(merged.astype(BF16), wout_ref[...], preferred_element_type=F32)


def _out(x2d, layer, yb, ma, sgb, wob, wout):
    m, d = x2d.shape
    tm = TM_OUT
    row = lambda width: pl.BlockSpec((tm, width), lambda i: (i, 0))
    full = lambda arr: _layer_spec(arr, layer)
    return pl.pallas_call(
        _out_body,
        grid=(m // tm,),
        in_specs=[row(d), row(B_WIDTH), row(d), row(d), full(wob), full(wout)],
        out_specs=row(d),
        out_shape=jax.ShapeDtypeStruct((m, d), F32),
        compiler_params=pltpu.CompilerParams(dimension_semantics=("arbitrary",),
                                             vmem_limit_bytes=VMEM_LIMIT_BYTES),
        name="out_proj",
    )(x2d, yb, ma, sgb, wob, wout)


def _rope_tables(s, dim):
    pos = jnp.arange(s, dtype=F32)
    inv = ROPE_THETA ** (-jnp.arange(0, dim, 2, dtype=F32) / dim)
    ang = pos[:, None] * inv[None, :]
    return jnp.cos(ang), jnp.sin(ang)


def kernel(x, norm_g, w_in, gate_b, a_ln_g, a_ln_b, a_ws, a_bs, q_norm_g, k_norm_g, w_oa, w_ob, w_out):
    b, s, d = x.shape
    depth = w_in.shape[0]
    topk = min(TOPK_MAX, s // 4)
    assert s % TM_STD == 0 and s % QB == 0 and (b * s) % TM_OUT == 0 and QB == TK and TM_STD_SUB == QB
    assert w_in.shape[2] == _O_END and topk <= TK

    cos_q, sin_q = _rope_tables(s, B_HEAD_DIM)
    cos_i, sin_i = _rope_tables(s, IDX_DIM)
    cos2 = jnp.concatenate([cos_q, cos_q], axis=1)
    sin2 = jnp.concatenate([-sin_q, sin_q], axis=1)
    cosi2 = jnp.concatenate([cos_i, cos_i, cos_i, cos_i], axis=1)
    sini2 = jnp.concatenate([-sin_i, sin_i, -sin_i, sin_i], axis=1)
    cos_t, sin_t, cosi_t, sini_t = cos_q.T, sin_q.T, cos_i.T, sin_i.T

    w_t = jnp.swapaxes(w_in, 1, 2)
    ik_rows = w_t[:, _O_IK:_O_IW]
    w_std = jnp.concatenate([w_t[:, _O_AU:_O_BQ], w_t[:, _O_BK:_O_BV], ik_rows, ik_rows,
                             w_t[:, _O_GA:_O_END]], axis=1).astype(BF16)
    w_tr = jnp.concatenate([w_t[:, _O_BQ:_O_BK], w_t[:, _O_BV:_O_IK]], axis=1).astype(BF16)
    iw_t = w_t[:, _O_IW:_O_GA]
    w_trw = jnp.concatenate([iw_t, jnp.zeros_like(iw_t)], axis=1).astype(BF16)
    woa, wob, wout = w_oa.astype(BF16), w_ob.astype(BF16), w_out.astype(BF16)
    bias_full = jnp.repeat(jnp.swapaxes(a_bs, 1, 2), A_WIDTH // A_GROUPS, axis=2)
    kng = jnp.reshape(k_norm_g, (depth, 1, B_HEAD_DIM))
    qng = jnp.reshape(q_norm_g, (depth, B_HEAD_DIM, 1))
    ng = jnp.reshape(norm_g, (depth, 1, d))
    lng = jnp.reshape(a_ln_g, (depth, 1, A_WIDTH))
    lnb = jnp.reshape(a_ln_b, (depth, 1, A_WIDTH))

    x2d = jnp.reshape(x, (b * s, d))
    for l in range(depth):
        ma, sgb, k, ki, qT, vT, zbT, qiT, wiT = _proj(
            x2d, l, ng, w_std, gate_b, lng, lnb, a_ws, bias_full, kng, woa, cos2, sin2, cosi2, sini2,
            w_tr, w_trw, qng, cos_t, sin_t, cosi_t, sini_t, s)
        yb = _attn(qT, qiT, wiT, zbT, jnp.reshape(k, (b, s, B_WIDTH)), jnp.reshape(ki, (b, s, IDX_DIM)),
                   vT, topk)
        x2d = _out(x2d, l, jnp.reshape(yb, (b * s, B_WIDTH)), ma, sgb, wob, wout)
    return jnp.reshape(x2d, (b, s, d))
```

```python
import functools

import jax
import jax.numpy as jnp
from jax import lax
from jax.experimental import pallas as pl
from jax.experimental.pallas import tpu as pltpu

F32 = jnp.float32
BF16 = jnp.bfloat16

CHUNK = 64
EPS = 1e-6
ROPE_THETA = 10000.0

A_WIDTH = 512
A_GROUPS = 4
A_BLOCK = 128
B_HEADS = 4
B_HEAD_DIM = 128
B_WIDTH = B_HEADS * B_HEAD_DIM
IDX_HEADS = 8
IDX_DIM = 64
TOPK_MAX = 256

_O_AU, _O_AV, _O_AZ = 0, 512, 1024
_O_BQ, _O_BK, _O_BV, _O_BZ = 1536, 2048, 2560, 3072
_O_IQ, _O_IK, _O_IW = 3584, 4096, 4160
_O_GA, _O_GB, _O_END = 4168, 5192, 6216

_P_AU, _P_AV, _P_AZ, _P_BK, _P_IK, _P_GA, _P_GB, _P_END = 0, 512, 1024, 1536, 2048, 2176, 3200, 4224
_F_BQ, _F_BV, _F_BZ, _F_IQ, _F_END = 0, 512, 1024, 1536, 2048

LANES = 128
SUBLANES = 8
VMEM_LIMIT_BYTES = 52 * 1024 * 1024

QB = 256
TK = 256
TM_STD = 512
TM_STD_SUB = 256
TM_OUT = 512
NEG_BIG = -1e30
DIGIT_BITS = 8
DIGIT_BASE = 2 ** DIGIT_BITS
RANK_MAX = float(DIGIT_BASE ** 3 - 1)
COUNT_CHAINS = 4
TILE_GROUPS = (8, 4, 2, 1)


def _rms_rows(x, g):
    ms = jnp.mean(x * x, axis=-1, keepdims=True)
    return x * lax.rsqrt(ms + EPS) * g


def _proj_body(*refs, after_out):
    if after_out:
        x_ref, yb_ref, ma_prev_ref, sgb_prev_ref, wob_ref, wout_ref = refs[:6]
        refs = refs[6:]
    else:
        x_ref, refs = refs[0], refs[1:]
    (ng_ref, w_ref, gb_ref, lng_ref, lnb_ref, ws_ref, bias_ref, kng_ref, woa_ref,
     cos_ref, sin_ref, cosi_ref, sini_ref,
     wt_ref, ww_ref, qng_ref, cost_ref, sint_ref, cosit_ref, sinit_ref) = refs[:20]
    outs = refs[20:]
    if after_out:
        xnew_ref, outs = outs[0], outs[1:]
    ma_ref, sgb_ref, k_ref, ki_ref, q_ref, v_ref, zb_ref, qi_ref, wi_ref = outs
    tm = x_ref.shape[0]
    att_scale = B_HEAD_DIM ** -0.5
    half = B_HEAD_DIM // 2
    ihalf = IDX_DIM // 2
    ci = lax.broadcasted_iota(jnp.int32, (A_BLOCK, A_BLOCK), 0) // CHUNK
    cj = lax.broadcasted_iota(jnp.int32, (A_BLOCK, A_BLOCK), 1) // CHUNK
    causal = cj <= ci
    wm = [jnp.where(causal, ws_ref[g], 0.0).astype(BF16) for g in range(A_GROUPS)]

    for r0 in range(0, tm, TM_STD_SUB):
        rows = slice(r0, r0 + TM_STD_SUB)
        x_rows = x_ref[rows, :]
        if after_out:
            x_rows = _out_rows(x_rows, yb_ref[rows, :], ma_prev_ref[rows, :], sgb_prev_ref[rows, :],
                               wob_ref, wout_ref)
            xnew_ref[rows, :] = x_rows
        h = _rms_rows(x_rows, ng_ref[...]).astype(BF16)

        def proj(lo, hi, h=h):
            return lax.dot_general(h, w_ref[lo:hi, :], (((1,), (1,)), ((), ())), preferred_element_type=F32)

        gv = jax.nn.gelu(proj(_P_AV, _P_AZ))
        mu = jnp.mean(gv, axis=-1, keepdims=True)
        xc = gv - mu
        var = jnp.mean(xc * xc, axis=-1, keepdims=True)
        vn = (xc * lax.rsqrt(var + EPS) * lng_ref[...] + lnb_ref[...]).astype(BF16)

        row_blocks = []
        for r in range(TM_STD_SUB // A_BLOCK):
            cols = []
            for g in range(A_GROUPS):
                vb = vn[r * A_BLOCK:(r + 1) * A_BLOCK, g * LANES:(g + 1) * LANES]
                cols.append(jnp.dot(wm[g], vb, preferred_element_type=F32))
            row_blocks.append(jnp.concatenate(cols, axis=1) + bias_ref[...])
        mixed = jnp.concatenate(row_blocks, axis=0)

        y_a = jax.nn.gelu(proj(_P_AU, _P_AV)) * mixed * jax.nn.silu(proj(_P_AZ, _P_BK))
        o_a = jnp.dot(y_a.astype(BF16), woa_ref[...], preferred_element_type=F32)
        ma_ref[rows, :] = (jax.nn.sigmoid(proj(_P_GA, _P_GB) + gb_ref[0:1, :]) * o_a).astype(BF16)
        sgb_ref[rows, :] = jax.nn.sigmoid(proj(_P_GB, _P_END) + gb_ref[1:2, :]).astype(BF16)

        b_k = proj(_P_BK, _P_IK)
        for hh in range(B_HEADS):
            kh = _rms_rows(b_k[:, hh * LANES:(hh + 1) * LANES], kng_ref[...])
            kh = kh * cos_ref[rows, :] + pltpu.roll(kh, B_HEAD_DIM // 2, 1) * sin_ref[rows, :]
            k_ref[rows, hh * LANES:(hh + 1) * LANES] = kh.astype(BF16)

        ik = proj(_P_IK, _P_GA)
        ik = ik * cosi_ref[rows, :] + pltpu.roll(ik, IDX_DIM // 2, 1) * sini_ref[rows, :]
        ki_ref[rows, :] = ik[:, :IDX_DIM].astype(BF16)

        j = r0 // QB

        def proj_t(w, h=h):
            return lax.dot_general(w, h, (((1,), (1,)), ((), ())), preferred_element_type=F32)

        qt = proj_t(wt_ref[_F_BQ:_F_BV, :])
        for hh in range(B_HEADS):
            qh = qt[hh * B_HEAD_DIM:(hh + 1) * B_HEAD_DIM, :]
            ms = jnp.mean(qh * qh, axis=0, keepdims=True)
            qh = qh * lax.rsqrt(ms + EPS) * qng_ref[...]
            x1, x2 = qh[:half, :], qh[half:, :]
            c, s = cost_ref[:, rows], sint_ref[:, rows]
            base = hh * B_HEAD_DIM
            q_ref[0, j, base:base + half, :] = ((x1 * c - x2 * s) * att_scale).astype(BF16)
            q_ref[0, j, base + half:base + B_HEAD_DIM, :] = ((x1 * s + x2 * c) * att_scale).astype(BF16)

        v_ref[0, j] = proj_t(wt_ref[_F_BV:_F_BZ, :]).astype(BF16)
        zb_ref[0, j] = jax.nn.silu(proj_t(wt_ref[_F_BZ:_F_IQ, :])).astype(BF16)

        qit = proj_t(wt_ref[_F_IQ:_F_END, :])
        for hh in range(IDX_HEADS):
            xh = qit[hh * IDX_DIM:(hh + 1) * IDX_DIM, :]
            x1, x2 = xh[:ihalf, :], xh[ihalf:, :]
            c, s = cosit_ref[:, rows], sinit_ref[:, rows]
            base = hh * IDX_DIM
            qi_ref[0, j, base:base + ihalf, :] = (x1 * c - x2 * s).astype(BF16)
            qi_ref[0, j, base + ihalf:base + IDX_DIM, :] = (x1 * s + x2 * c).astype(BF16)

        wi_ref[0, j] = proj_t(ww_ref[...])


def _layer_spec(arr, layer):
    zeros = (0,) * (arr.ndim - 1)
    return pl.BlockSpec((None,) + arr.shape[1:], lambda i: (layer,) + zeros)


def _proj(x2d, layer, ng, w_std, gb, lng, lnb, ws, bias_full, kng, woa, cos2, sin2, cosi2, sini2,
          w_tr, w_trw, qng, cos_t, sin_t, cosi_t, sini_t, seq, prev=None):
    m, d = x2d.shape
    tm = TM_STD
    nt = seq // tm
    b, nq, per = m // seq, seq // QB, tm // QB
    full = lambda arr: _layer_spec(arr, layer)
    tab = lambda: pl.BlockSpec((tm, LANES), lambda i: (i % nt, 0))
    tabt = lambda rows: pl.BlockSpec((rows, tm), lambda i: (0, i % nt))
    row = lambda width: pl.BlockSpec((tm, width), lambda i: (i, 0))
    outt = lambda rows: pl.BlockSpec((1, per, rows, QB), lambda i: (i // nt, i % nt, 0, 0))
    wrows = w_trw.shape[1]
    feat = lambda rows, dtype: jax.ShapeDtypeStruct((b, nq, rows, QB), dtype)

    operands = [x2d]
    in_specs = [row(d)]
    out_specs, out_shape = [], []
    if prev is not None:
        yb, ma_prev, sgb_prev, wob, wout = prev
        operands += [yb, ma_prev, sgb_prev, wob, wout]
        in_specs += [row(B_WIDTH), row(d), row(d), _layer_spec(wob, layer - 1), _layer_spec(wout, layer - 1)]
        out_specs.append(row(d))
        out_shape.append(jax.ShapeDtypeStruct((m, d), F32))
    operands += [ng, w_std, gb, lng, lnb, ws, bias_full, kng, woa, cos2, sin2, cosi2, sini2,
                 w_tr, w_trw, qng, cos_t, sin_t, cosi_t, sini_t]
    in_specs += [full(ng), full(w_std), full(gb), full(lng), full(lnb), full(ws), full(bias_full),
                 full(kng), full(woa), tab(), tab(), tab(), tab(),
                 full(w_tr), full(w_trw), full(qng),
                 tabt(B_HEAD_DIM // 2), tabt(B_HEAD_DIM // 2), tabt(IDX_DIM // 2), tabt(IDX_DIM // 2)]
    out_specs += [row(d), row(d), row(B_WIDTH), row(IDX_DIM),
                  outt(B_WIDTH), outt(B_WIDTH), outt(B_WIDTH), outt(IDX_HEADS * IDX_DIM), outt(wrows)]
    out_shape += [jax.ShapeDtypeStruct((m, d), BF16), jax.ShapeDtypeStruct((m, d), BF16),
                  jax.ShapeDtypeStruct((m, B_WIDTH), BF16), jax.ShapeDtypeStruct((m, IDX_DIM), BF16),
                  feat(B_WIDTH, BF16), feat(B_WIDTH, BF16), feat(B_WIDTH, BF16),
                  feat(IDX_HEADS * IDX_DIM, BF16), feat(wrows, F32)]
    return pl.pallas_call(
        functools.partial(_proj_body, after_out=prev is not None),
        grid=(m // tm,),
        in_specs=in_specs,
        out_specs=out_specs,
        out_shape=out_shape,
        compiler_params=pltpu.CompilerParams(dimension_semantics=("arbitrary",),
                                             vmem_limit_bytes=VMEM_LIMIT_BYTES),
        name="proj" if prev is None else "out_proj_then_proj",
    )(*operands)


def _slab_reduce(x, op):
    parts = [x[r:r + SUBLANES, :] for r in range(0, x.shape[0], SUBLANES)]
    while len(parts) > 1:
        parts = [op(parts[i], parts[i + 1]) for i in range(0, len(parts), 2)]
    return parts[0]


def _attn_body(q_ref, qi_ref, wi_ref, zb_ref, k_ref, ki_ref, v_ref, y_ref,
               s_scr, r_scr, d_scr, att_scr, acc_scr, *, topk):
    n = pl.program_id(1)
    ntiles = n + 1
    idx_scale = (IDX_DIM ** -0.5) * (IDX_HEADS ** -0.5)
    rows_per_tile = TK // SUBLANES

    def score_tile(t, lo8, hi8, diagonal):
        ki_t = ki_ref[0, pl.ds(pl.multiple_of(t * TK, TK), TK), :]
        acc = jnp.zeros((TK, QB), F32)
        for hh in range(IDX_HEADS):
            logit = jnp.dot(ki_t, qi_ref[0, 0, hh * IDX_DIM:(hh + 1) * IDX_DIM, :],
                            preferred_element_type=F32)
            acc = acc + wi_ref[0, 0, hh:hh + 1, :] * jnp.maximum(logit, 0.0)
        sc = acc * idx_scale
        sc_for_min = sc
        if diagonal:
            kc = lax.broadcasted_iota(jnp.int32, (TK, QB), 0) // CHUNK
            qc = lax.broadcasted_iota(jnp.int32, (TK, QB), 1) // CHUNK
            admissible = kc <= qc
            sc_for_min = jnp.where(admissible, sc, jnp.inf)
            sc = jnp.where(admissible, sc, -jnp.inf)
        s_scr[t] = sc
        return (jnp.minimum(lo8, _slab_reduce(sc_for_min, jnp.minimum)),
                jnp.maximum(hi8, _slab_reduce(sc, jnp.maximum)))

    def tiles_grouped(count, tile_fn, carry):
        def group(width, first, c):
            for j in range(width):
                c = tile_fn(first + j, c)
            return c

        widest = TILE_GROUPS[0]
        carry = lax.fori_loop(0, count // widest, lambda p, c: group(widest, widest * p, c), carry)
        for width in TILE_GROUPS[1:]:
            first = count & ~(2 * width - 1)
            carry = lax.cond((count & width) == width, functools.partial(group, width, first),
                             lambda c: c, carry)
        return carry

    lo8, hi8 = tiles_grouped(
        n, lambda t, c: score_tile(t, c[0], c[1], False),
        (jnp.full((SUBLANES, QB), jnp.inf, F32), jnp.full((SUBLANES, QB), -jnp.inf, F32)))
    lo8, hi8 = score_tile(n, lo8, hi8, True)
    s_lo = jnp.min(lo8, axis=0, keepdims=True)
    s_hi = jnp.max(hi8, axis=0, keepdims=True)

    span = s_hi - s_lo
    rank_scale = jnp.where(span > 0.0, RANK_MAX / jnp.where(span > 0.0, span, 1.0), 0.0)

    def rank_tile(t, carry):
        r = jnp.minimum(jnp.floor((s_scr[t] - s_lo) * rank_scale), RANK_MAX)
        r = jnp.maximum(r, -1.0)
        r_scr[t] = r
        d1 = jnp.floor(r * (1.0 / DIGIT_BASE ** 2))
        rem = r - d1 * float(DIGIT_BASE ** 2)
        d2 = jnp.floor(rem * (1.0 / DIGIT_BASE))
        d_scr[0, t] = d1.astype(BF16)
        d_scr[1, t] = d2.astype(BF16)
        d_scr[2, t] = (rem - d2 * float(DIGIT_BASE)).astype(BF16)
        return carry

    lax.fori_loop(0, ntiles, rank_tile, 0)

    @pl.when((ntiles & 1) == 1)
    def _():
        for plane in range(3):
            d_scr[plane, ntiles] = jnp.full((TK, QB), -1.0, BF16)

    npairs = lax.shift_right_logical(ntiles + 1, 1)
    packed_rows = 2 * SUBLANES
    slabs_per_tile = TK // packed_rows
    one_bf, zero_bf = jnp.ones((), BF16), jnp.zeros((), BF16)

    def count_ge(plane, cand):
        cand_b = jnp.broadcast_to(cand.astype(BF16), (packed_rows, QB))

        def pair_body(p, accs):
            accs = list(accs)
            for half in range(2):
                for r in range(slabs_per_tile):
                    sl = d_scr[plane, 2 * p + half, r * packed_rows:(r + 1) * packed_rows, :]
                    c = r % COUNT_CHAINS
                    accs[c] = accs[c] + jnp.where(sl >= cand_b, one_bf, zero_bf)
            return tuple(accs)

        accs = lax.fori_loop(0, npairs, pair_body,
                             tuple(jnp.zeros((packed_rows, QB), BF16) for _ in range(COUNT_CHAINS)))
        total = functools.reduce(jnp.add, [a.astype(F32) for a in accs])
        return jnp.sum(total, axis=0, keepdims=True)

    def digit_search(plane, cnt_at):
        def bit_body(i, carry):
            prefix, cnt_at, step = carry
            trial = prefix + step
            cnt = count_ge(plane, trial)
            ok = cnt >= float(topk)
            return jnp.where(ok, trial, prefix), jnp.where(ok, cnt, cnt_at), step * 0.5
        digit, cnt_at, _ = lax.fori_loop(
            0, DIGIT_BITS, bit_body,
            (jnp.zeros((1, QB), F32), cnt_at, jnp.full((1, QB), DIGIT_BASE / 2.0, F32)))
        return digit, cnt_at

    def restrict_plane(plane, digit):
        digit_b = jnp.broadcast_to(digit.astype(BF16), (TK, QB))
        above, below = jnp.full((), float(DIGIT_BASE), BF16), jnp.full((), -1.0, BF16)

        def tile_body(t, carry):
            prev = d_scr[plane, t]
            d_scr[plane + 1, t] = jnp.where(prev > digit_b, above,
                                            jnp.where(prev < digit_b, below, d_scr[plane + 1, t]))
            return carry

        lax.fori_loop(0, ntiles, tile_body, 0)

    q_chunk = (n * QB + lax.broadcasted_iota(jnp.int32, (1, QB), 1)) // CHUNK
    n_admissible = ((q_chunk + 1) * CHUNK).astype(F32)
    digit1, cnt_at = digit_search(0, n_admissible)
    restrict_plane(0, digit1)
    digit2, cnt_at = digit_search(1, cnt_at)
    restrict_plane(1, digit2)
    digit3, cnt_at = digit_search(2, cnt_at)
    thr = (digit1 * float(DIGIT_BASE) + digit2) * float(DIGIT_BASE) + digit3

    excess0 = jnp.maximum(cnt_at - float(topk), 0.0)

    @pl.when(jnp.max(excess0) > 0.0)
    def _():
        thr_b = jnp.broadcast_to(thr, (SUBLANES, QB))
        later_or_same = (lax.broadcasted_iota(jnp.int32, (TK, TK), 1)
                         >= lax.broadcasted_iota(jnp.int32, (TK, TK), 0))
        suffix_ones = jnp.where(later_or_same, 1.0, 0.0).astype(BF16)

        def slab(r):
            return slice(r * SUBLANES, (r + 1) * SUBLANES)

        def drop_group(excess):
            def min_tile(t, ms):
                ms = list(ms)
                for r in range(rows_per_tile):
                    selected = r_scr[t, slab(r), :] >= thr_b
                    c = r % COUNT_CHAINS
                    ms[c] = jnp.minimum(ms[c], jnp.where(selected, s_scr[t, slab(r), :], jnp.inf))
                return tuple(ms)

            ms = lax.fori_loop(0, ntiles, min_tile,
                               tuple(jnp.full((SUBLANES, QB), jnp.inf, F32) for _ in range(COUNT_CHAINS)))
            m8 = functools.reduce(jnp.minimum, ms)
            m_row = jnp.min(m8, axis=0, keepdims=True)

            def drop_tile(i, later):
                t = ntiles - 1 - i
                r = r_scr[t]
                in_group = (r >= thr) & (s_scr[t] == m_row)
                member = jnp.where(in_group, 1.0, 0.0).astype(BF16)
                suffix = jnp.dot(suffix_ones, member, preferred_element_type=F32) + later
                r_scr[t] = jnp.where(in_group & (suffix <= excess), -1.0, r)
                return suffix[0:1, :]

            group_size = lax.fori_loop(0, ntiles, drop_tile, jnp.zeros((1, QB), F32))
            return excess - jnp.minimum(excess, group_size)

        lax.while_loop(lambda e: jnp.max(e) > 0.0, drop_group, excess0)

    slab_reduce = _slab_reduce

    def logit_tile(t, m8):
        sel = r_scr[t] >= thr
        krows = pl.ds(pl.multiple_of(t * TK, TK), TK)
        new_m8 = []
        for hh in range(B_HEADS):
            hs = slice(hh * B_HEAD_DIM, (hh + 1) * B_HEAD_DIM)
            att = jnp.dot(k_ref[0, krows, hs], q_ref[0, 0, hs, :], preferred_element_type=F32)
            att = jnp.where(sel, att, NEG_BIG)
            att_scr[t * B_HEADS + hh] = att
            new_m8.append(jnp.maximum(m8[hh], slab_reduce(att, jnp.maximum)))
        return tuple(new_m8)

    m8 = tiles_grouped(ntiles, logit_tile,
                        tuple(jnp.full((SUBLANES, QB), NEG_BIG, F32) for _ in range(B_HEADS)))
    m_row = [jnp.max(m, axis=0, keepdims=True) for m in m8]

    acc_scr[...] = jnp.zeros_like(acc_scr)

    def value_tile(t, l8):
        new_l8 = []
        for hh in range(B_HEADS):
            hs = slice(hh * B_HEAD_DIM, (hh + 1) * B_HEAD_DIM)
            p = jnp.exp(att_scr[t * B_HEADS + hh] - m_row[hh])
            new_l8.append(l8[hh] + slab_reduce(p, jnp.add))
            acc_scr[hs, :] += jnp.dot(v_ref[0, t, hs, :], p.astype(BF16), preferred_element_type=F32)
        return tuple(new_l8)

    l8 = tiles_grouped(ntiles, value_tile,
                        tuple(jnp.zeros((SUBLANES, QB), F32) for _ in range(B_HEADS)))

    parts = []
    for hh in range(B_HEADS):
        hs = slice(hh * B_HEAD_DIM, (hh + 1) * B_HEAD_DIM)
        l_row = jnp.sum(l8[hh], axis=0, keepdims=True)
        parts.append(acc_scr[hs, :] / l_row * zb_ref[0, 0, hs, :].astype(F32))
    y_ref[0] = jnp.concatenate(parts, axis=0).T.astype(BF16)


def _attn(qT, qiT, wiT, zbT, k, ki, vT, topk):
    b, nq = qT.shape[0], qT.shape[1]
    s = k.shape[1]
    blk = lambda rows: pl.BlockSpec((1, 1, rows, QB), lambda bi, i: (bi, i, 0, 0))
    return pl.pallas_call(
        functools.partial(_attn_body, topk=topk),
        grid=(b, nq),
        in_specs=[blk(B_WIDTH), blk(IDX_HEADS * IDX_DIM), blk(wiT.shape[2]), blk(B_WIDTH),
                  pl.BlockSpec((1, s, B_WIDTH), lambda bi, i: (bi, 0, 0), pipeline_mode=pl.Buffered(1)),
                  pl.BlockSpec((1, s, IDX_DIM), lambda bi, i: (bi, 0, 0), pipeline_mode=pl.Buffered(1)),
                  pl.BlockSpec((1, nq, B_WIDTH, QB), lambda bi, i: (bi, 0, 0, 0),
                               pipeline_mode=pl.Buffered(1))],
        out_specs=pl.BlockSpec((1, QB, B_WIDTH), lambda bi, i: (bi, i, 0)),
        out_shape=jax.ShapeDtypeStruct((b, s, B_WIDTH), BF16),
        scratch_shapes=[pltpu.VMEM((s // TK, TK, QB), F32),
                        pltpu.VMEM((s // TK, TK, QB), F32),
                        pltpu.VMEM((3, s // TK, TK, QB), BF16),
                        pltpu.VMEM((s // TK * B_HEADS, TK, QB), F32),
                        pltpu.VMEM((B_WIDTH, QB), F32)],
        compiler_params=pltpu.CompilerParams(dimension_semantics=("arbitrary", "arbitrary"),
                                             vmem_limit_bytes=VMEM_LIMIT_BYTES),
        name="attn",
    )(qT, qiT, wiT, zbT, k, ki, vT)


def _out_rows(x, yb, ma, sgb, wob_ref, wout_ref):
    o_b = jnp.dot(yb, wob_ref[...], preferred_element_type=F32)
    merged = ma.astype(F32) + sgb.astype(F32) * o_b
    return x + jnp.dot(merged.astype(BF16), wout_ref[...], preferred_element_type=F32)


def _out_body(x_ref, yb_ref, ma_ref, sgb_ref, wob_ref, wout_ref, o_ref):
    o_ref[...] = _out_rows(x_ref[...], yb_ref[...], ma_ref[...], sgb_ref[...], wob_ref, wout_ref)


def _out(x2d, layer, yb, ma, sgb, wob, wout):
    m, d = x2d.shape
    tm = TM_OUT
    row = lambda width: pl.BlockSpec((tm, width), lambda i: (i, 0))
    full = lambda arr: _layer_spec(arr, layer)
    return pl.pallas_call(
        _out_body,
        grid=(m // tm,),
        in_specs=[row(d), row(B_WIDTH), row(d), row(d), full(wob), full(wout)],
        out_specs=row(d),
        out_shape=jax.ShapeDtypeStruct((m, d), F32),
        compiler_params=pltpu.CompilerParams(dimension_semantics=("arbitrary",),
                                             vmem_limit_bytes=VMEM_LIMIT_BYTES),
        name="out_proj",
    )(x2d, yb, ma, sgb, wob, wout)


def _rope_tables(s, dim):
    pos = jnp.arange(s, dtype=F32)
    inv = ROPE_THETA ** (-jnp.arange(0, dim, 2, dtype=F32) / dim)
    ang = pos[:, None] * inv[None, :]
    return jnp.cos(ang), jnp.sin(ang)


def kernel(x, norm_g, w_in, gate_b, a_ln_g, a_ln_b, a_ws, a_bs, q_norm_g, k_norm_g, w_oa, w_ob, w_out):
    b, s, d = x.shape
    depth = w_in.shape[0]
    topk = min(TOPK_MAX, s // 4)
    assert s % TM_STD == 0 and s % QB == 0 and (b * s) % TM_OUT == 0 and QB == TK and TM_STD_SUB == QB
    assert w_in.shape[2] == _O_END and topk <= TK

    cos_q, sin_q = _rope_tables(s, B_HEAD_DIM)
    cos_i, sin_i = _rope_tables(s, IDX_DIM)
    cos2 = jnp.concatenate([cos_q, cos_q], axis=1)
    sin2 = jnp.concatenate([-sin_q, sin_q], axis=1)
    cosi2 = jnp.concatenate([cos_i, cos_i, cos_i, cos_i], axis=1)
    sini2 = jnp.concatenate([-sin_i, sin_i, -sin_i, sin_i], axis=1)
    cos_t, sin_t, cosi_t, sini_t = cos_q.T, sin_q.T, cos_i.T, sin_i.T

    w_t = jnp.swapaxes(w_in, 1, 2)
    ik_rows = w_t[:, _O_IK:_O_IW]
    w_std = jnp.concatenate([w_t[:, _O_AU:_O_BQ], w_t[:, _O_BK:_O_BV], ik_rows, ik_rows,
                             w_t[:, _O_GA:_O_END]], axis=1).astype(BF16)
    w_tr = jnp.concatenate([w_t[:, _O_BQ:_O_BK], w_t[:, _O_BV:_O_IK]], axis=1).astype(BF16)
    iw_t = w_t[:, _O_IW:_O_GA]
    w_trw = jnp.concatenate([iw_t, jnp.zeros_like(iw_t)], axis=1).astype(BF16)
    woa, wob, wout = w_oa.astype(BF16), w_ob.astype(BF16), w_out.astype(BF16)
    bias_full = jnp.repeat(jnp.swapaxes(a_bs, 1, 2), A_WIDTH // A_GROUPS, axis=2)
    kng = jnp.reshape(k_norm_g, (depth, 1, B_HEAD_DIM))
    qng = jnp.reshape(q_norm_g, (depth, B_HEAD_DIM, 1))
    ng = jnp.reshape(norm_g, (depth, 1, d))
    lng = jnp.reshape(a_ln_g, (depth, 1, A_WIDTH))
    lnb = jnp.reshape(a_ln_b, (depth, 1, A_WIDTH))

    proj = functools.partial(_proj, ng=ng, w_std=w_std, gb=gate_b, lng=lng, lnb=lnb, ws=a_ws,
                             bias_full=bias_full, kng=kng, woa=woa, cos2=cos2, sin2=sin2, cosi2=cosi2,
                             sini2=sini2, w_tr=w_tr, w_trw=w_trw, qng=qng, cos_t=cos_t, sin_t=sin_t,
                             cosi_t=cosi_t, sini_t=sini_t, seq=s)
    x2d = jnp.reshape(x, (b * s, d))
    ma, sgb, k, ki, qT, vT, zbT, qiT, wiT = proj(x2d, 0)
    for l in range(depth):
        yb = _attn(qT, qiT, wiT, zbT, jnp.reshape(k, (b, s, B_WIDTH)), jnp.reshape(ki, (b, s, IDX_DIM)),
                   vT, topk)
        yb = jnp.reshape(yb, (b * s, B_WIDTH))
        if l + 1 < depth:
            x2d, ma, sgb, k, ki, qT, vT, zbT, qiT, wiT = proj(x2d, l + 1, prev=(yb, ma, sgb, wob, wout))
        else:
            x2d = _out(x2d, l, yb, ma, sgb, wob, wout)
    return jnp.reshape(x2d, (b, s, d))
```

```python
import functools

import jax
import jax.numpy as jnp
from jax import lax
from jax.experimental import pallas as pl
from jax.experimental.pallas import tpu as pltpu

F32 = jnp.float32
BF16 = jnp.bfloat16

CHUNK = 64
EPS = 1e-6
ROPE_THETA = 10000.0

A_WIDTH = 512
A_GROUPS = 4
A_BLOCK = 128
B_HEADS = 4
B_HEAD_DIM = 128
B_WIDTH = B_HEADS * B_HEAD_DIM
IDX_HEADS = 8
IDX_DIM = 64
TOPK_MAX = 256

_O_AU, _O_AV, _O_AZ = 0, 512, 1024
_O_BQ, _O_BK, _O_BV, _O_BZ = 1536, 2048, 2560, 3072
_O_IQ, _O_IK, _O_IW = 3584, 4096, 4160
_O_GA, _O_GB, _O_END = 4168, 5192, 6216

_P_AU, _P_AV, _P_AZ, _P_BK, _P_IK, _P_GA, _P_GB, _P_END = 0, 512, 1024, 1536, 2048, 2176, 3200, 4224
_F_BQ, _F_BV, _F_BZ, _F_IQ, _F_END = 0, 512, 1024, 1536, 2048

LANES = 128
SUBLANES = 8
VMEM_LIMIT_BYTES = 56 * 1024 * 1024

QB = 256
TK = 256
TM_STD = 512
TM_STD_SUB = 256
TM_OUT = 512
NEG_BIG = -1e30
DIGIT_BITS = 8
DIGIT_BASE = 2 ** DIGIT_BITS
RANK_MAX = float(DIGIT_BASE ** 3 - 1)
COUNT_CHAINS = 4
TILE_GROUPS = (8, 4, 2, 1)


def _rms_rows(x, g):
    ms = jnp.mean(x * x, axis=-1, keepdims=True)
    return x * lax.rsqrt(ms + EPS) * g


def _proj_body(*refs, after_out):
    if after_out:
        x_ref, yb_ref, ma_prev_ref, sgb_prev_ref, wob_ref, wout_ref = refs[:6]
        refs = refs[6:]
    else:
        x_ref, refs = refs[0], refs[1:]
    (ng_ref, w_ref, gb_ref, lng_ref, lnb_ref, ws_ref, bias_ref, kng_ref, woa_ref,
     cos_ref, sin_ref, cosi_ref, sini_ref,
     wt_ref, ww_ref, qng_ref, cost_ref, sint_ref, cosit_ref, sinit_ref) = refs[:20]
    outs = refs[20:]
    if after_out:
        xnew_ref, outs = outs[0], outs[1:]
    ma_ref, sgb_ref, k_ref, ki_ref, q_ref, v_ref, zb_ref, qi_ref, wi_ref = outs
    tm = x_ref.shape[0]
    att_scale = B_HEAD_DIM ** -0.5
    half = B_HEAD_DIM // 2
    ihalf = IDX_DIM // 2
    ci = lax.broadcasted_iota(jnp.int32, (A_BLOCK, A_BLOCK), 0) // CHUNK
    cj = lax.broadcasted_iota(jnp.int32, (A_BLOCK, A_BLOCK), 1) // CHUNK
    causal = cj <= ci
    wm = [jnp.where(causal, ws_ref[g], 0.0).astype(BF16) for g in range(A_GROUPS)]

    for r0 in range(0, tm, TM_STD_SUB):
        rows = slice(r0, r0 + TM_STD_SUB)
        x_rows = x_ref[rows, :]
        if after_out:
            x_rows = _out_rows(x_rows, yb_ref[rows, :], ma_prev_ref[rows, :], sgb_prev_ref[rows, :],
                               wob_ref, wout_ref)
            xnew_ref[rows, :] = x_rows
        h = _rms_rows(x_rows, ng_ref[...]).astype(BF16)

        def proj(lo, hi, h=h):
            return lax.dot_general(h, w_ref[lo:hi, :], (((1,), (1,)), ((), ())), preferred_element_type=F32)

        gv = jax.nn.gelu(proj(_P_AV, _P_AZ))
        mu = jnp.mean(gv, axis=-1, keepdims=True)
        xc = gv - mu
        var = jnp.mean(xc * xc, axis=-1, keepdims=True)
        vn = (xc * lax.rsqrt(var + EPS) * lng_ref[...] + lnb_ref[...]).astype(BF16)

        row_blocks = []
        for r in range(TM_STD_SUB // A_BLOCK):
            cols = []
            for g in range(A_GROUPS):
                vb = vn[r * A_BLOCK:(r + 1) * A_BLOCK, g * LANES:(g + 1) * LANES]
                cols.append(jnp.dot(wm[g], vb, preferred_element_type=F32))
            row_blocks.append(jnp.concatenate(cols, axis=1) + bias_ref[...])
        mixed = jnp.concatenate(row_blocks, axis=0)

        y_a = jax.nn.gelu(proj(_P_AU, _P_AV)) * mixed * jax.nn.silu(proj(_P_AZ, _P_BK))
        o_a = jnp.dot(y_a.astype(BF16), woa_ref[...], preferred_element_type=F32)
        ma_ref[rows, :] = (jax.nn.sigmoid(proj(_P_GA, _P_GB) + gb_ref[0:1, :]) * o_a).astype(BF16)
        sgb_ref[rows, :] = jax.nn.sigmoid(proj(_P_GB, _P_END) + gb_ref[1:2, :]).astype(BF16)

        b_k = proj(_P_BK, _P_IK)
        for hh in range(B_HEADS):
            kh = _rms_rows(b_k[:, hh * LANES:(hh + 1) * LANES], kng_ref[...])
            kh = kh * cos_ref[rows, :] + pltpu.roll(kh, B_HEAD_DIM // 2, 1) * sin_ref[rows, :]
            k_ref[rows, hh * LANES:(hh + 1) * LANES] = kh.astype(BF16)

        ik = proj(_P_IK, _P_GA)
        ik = ik * cosi_ref[rows, :] + pltpu.roll(ik, IDX_DIM // 2, 1) * sini_ref[rows, :]
        ki_ref[rows, :] = ik[:, :IDX_DIM].astype(BF16)

        j = r0 // QB

        def proj_t(w, h=h):
            return lax.dot_general(w, h, (((1,), (1,)), ((), ())), preferred_element_type=F32)

        qt = proj_t(wt_ref[_F_BQ:_F_BV, :])
        for hh in range(B_HEADS):
            qh = qt[hh * B_HEAD_DIM:(hh + 1) * B_HEAD_DIM, :]
            ms = jnp.mean(qh * qh, axis=0, keepdims=True)
            qh = qh * lax.rsqrt(ms + EPS) * qng_ref[...]
            x1, x2 = qh[:half, :], qh[half:, :]
            c, s = cost_ref[:, rows], sint_ref[:, rows]
            base = hh * B_HEAD_DIM
            q_ref[0, j, base:base + half, :] = ((x1 * c - x2 * s) * att_scale).astype(BF16)
            q_ref[0, j, base + half:base + B_HEAD_DIM, :] = ((x1 * s + x2 * c) * att_scale).astype(BF16)

        v_ref[0, j] = proj_t(wt_ref[_F_BV:_F_BZ, :]).astype(BF16)
        zb_ref[0, j] = jax.nn.silu(proj_t(wt_ref[_F_BZ:_F_IQ, :])).astype(BF16)

        qit = proj_t(wt_ref[_F_IQ:_F_END, :])
        for hh in range(IDX_HEADS):
            xh = qit[hh * IDX_DIM:(hh + 1) * IDX_DIM, :]
            x1, x2 = xh[:ihalf, :], xh[ihalf:, :]
            c, s = cosit_ref[:, rows], sinit_ref[:, rows]
            base = hh * IDX_DIM
            qi_ref[0, j, base:base + ihalf, :] = (x1 * c - x2 * s).astype(BF16)
            qi_ref[0, j, base + ihalf:base + IDX_DIM, :] = (x1 * s + x2 * c).astype(BF16)

        wi_ref[0, j] = proj_t(ww_ref[...])


def _layer_spec(arr, layer):
    zeros = (0,) * (arr.ndim - 1)
    return pl.BlockSpec((None,) + arr.shape[1:], lambda i: (layer,) + zeros)


def _proj(x2d, layer, ng, w_std, gb, lng, lnb, ws, bias_full, kng, woa, cos2, sin2, cosi2, sini2,
          w_tr, w_trw, qng, cos_t, sin_t, cosi_t, sini_t, seq, prev=None):
    m, d = x2d.shape
    tm = TM_STD
    nt = seq // tm
    b, nq, per = m // seq, seq // QB, tm // QB
    full = lambda arr: _layer_spec(arr, layer)
    tab = lambda: pl.BlockSpec((tm, LANES), lambda i: (i % nt, 0))
    tabt = lambda rows: pl.BlockSpec((rows, tm), lambda i: (0, i % nt))
    row = lambda width: pl.BlockSpec((tm, width), lambda i: (i, 0))
    outt = lambda rows: pl.BlockSpec((1, per, rows, QB), lambda i: (i // nt, i % nt, 0, 0))
    wrows = w_trw.shape[1]
    feat = lambda rows, dtype: jax.ShapeDtypeStruct((b, nq, rows, QB), dtype)

    operands = [x2d]
    in_specs = [row(d)]
    out_specs, out_shape = [], []
    if prev is not None:
        yb, ma_prev, sgb_prev, wob, wout = prev
        operands += [yb, ma_prev, sgb_prev, wob, wout]
        in_specs += [row(B_WIDTH), row(d), row(d), _layer_spec(wob, layer - 1), _layer_spec(wout, layer - 1)]
        out_specs.append(row(d))
        out_shape.append(jax.ShapeDtypeStruct((m, d), F32))
    operands += [ng, w_std, gb, lng, lnb, ws, bias_full, kng, woa, cos2, sin2, cosi2, sini2,
                 w_tr, w_trw, qng, cos_t, sin_t, cosi_t, sini_t]
    in_specs += [full(ng), full(w_std), full(gb), full(lng), full(lnb), full(ws), full(bias_full),
                 full(kng), full(woa), tab(), tab(), tab(), tab(),
                 full(w_tr), full(w_trw), full(qng),
                 tabt(B_HEAD_DIM // 2), tabt(B_HEAD_DIM // 2), tabt(IDX_DIM // 2), tabt(IDX_DIM // 2)]
    out_specs += [row(d), row(d), row(B_WIDTH), row(IDX_DIM),
                  outt(B_WIDTH), outt(B_WIDTH), outt(B_WIDTH), outt(IDX_HEADS * IDX_DIM), outt(wrows)]
    out_shape += [jax.ShapeDtypeStruct((m, d), BF16), jax.ShapeDtypeStruct((m, d), BF16),
                  jax.ShapeDtypeStruct((m, B_WIDTH), BF16), jax.ShapeDtypeStruct((m, IDX_DIM), BF16),
                  feat(B_WIDTH, BF16), feat(B_WIDTH, BF16), feat(B_WIDTH, BF16),
                  feat(IDX_HEADS * IDX_DIM, BF16), feat(wrows, F32)]
    return pl.pallas_call(
        functools.partial(_proj_body, after_out=prev is not None),
        grid=(m // tm,),
        in_specs=in_specs,
        out_specs=out_specs,
        out_shape=out_shape,
        compiler_params=pltpu.CompilerParams(dimension_semantics=("arbitrary",),
                                             vmem_limit_bytes=VMEM_LIMIT_BYTES),
        name="proj" if prev is None else "out_proj_then_proj",
    )(*operands)


def _slab_reduce(x, op):
    parts = [x[r:r + SUBLANES, :] for r in range(0, x.shape[0], SUBLANES)]
    while len(parts) > 1:
        parts = [op(parts[i], parts[i + 1]) for i in range(0, len(parts), 2)]
    return parts[0]


def _attn_body(q_ref, qi_ref, wi_ref, zb_ref, k_ref, ki_ref, v_ref, y_ref,
               s_scr, r_scr, d_scr, att_scr, acc_scr, *, topk):
    n = pl.program_id(1)
    ntiles = n + 1
    idx_scale = (IDX_DIM ** -0.5) * (IDX_HEADS ** -0.5)
    rows_per_tile = TK // SUBLANES

    def score_tile(t, lo8, hi8, diagonal):
        ki_t = ki_ref[0, pl.ds(pl.multiple_of(t * TK, TK), TK), :]
        acc = jnp.zeros((TK, QB), F32)
        for hh in range(IDX_HEADS):
            logit = jnp.dot(ki_t, qi_ref[0, 0, hh * IDX_DIM:(hh + 1) * IDX_DIM, :],
                            preferred_element_type=F32)
            acc = acc + wi_ref[0, 0, hh:hh + 1, :] * jnp.maximum(logit, 0.0)
        sc = acc * idx_scale
        sc_for_min = sc
        if diagonal:
            kc = lax.broadcasted_iota(jnp.int32, (TK, QB), 0) // CHUNK
            qc = lax.broadcasted_iota(jnp.int32, (TK, QB), 1) // CHUNK
            admissible = kc <= qc
            sc_for_min = jnp.where(admissible, sc, jnp.inf)
            sc = jnp.where(admissible, sc, -jnp.inf)
        s_scr[t] = sc
        return (jnp.minimum(lo8, _slab_reduce(sc_for_min, jnp.minimum)),
                jnp.maximum(hi8, _slab_reduce(sc, jnp.maximum)))

    def tiles_grouped(count, tile_fn, carry):
        def group(width, first, c):
            for j in range(width):
                c = tile_fn(first + j, c)
            return c

        widest = TILE_GROUPS[0]
        carry = lax.fori_loop(0, count // widest, lambda p, c: group(widest, widest * p, c), carry)
        for width in TILE_GROUPS[1:]:
            first = count & ~(2 * width - 1)
            carry = lax.cond((count & width) == width, functools.partial(group, width, first),
                             lambda c: c, carry)
        return carry

    lo8, hi8 = tiles_grouped(
        n, lambda t, c: score_tile(t, c[0], c[1], False),
        (jnp.full((SUBLANES, QB), jnp.inf, F32), jnp.full((SUBLANES, QB), -jnp.inf, F32)))
    lo8, hi8 = score_tile(n, lo8, hi8, True)
    s_lo = jnp.min(lo8, axis=0, keepdims=True)
    s_hi = jnp.max(hi8, axis=0, keepdims=True)

    span = s_hi - s_lo
    rank_scale = jnp.where(span > 0.0, RANK_MAX / jnp.where(span > 0.0, span, 1.0), 0.0)

    def rank_tile(t, carry):
        r = jnp.minimum(jnp.floor((s_scr[t] - s_lo) * rank_scale), RANK_MAX)
        r = jnp.maximum(r, -1.0)
        r_scr[t] = r
        d1 = jnp.floor(r * (1.0 / DIGIT_BASE ** 2))
        rem = r - d1 * float(DIGIT_BASE ** 2)
        d2 = jnp.floor(rem * (1.0 / DIGIT_BASE))
        d_scr[0, t] = d1.astype(BF16)
        d_scr[1, t] = d2.astype(BF16)
        d_scr[2, t] = (rem - d2 * float(DIGIT_BASE)).astype(BF16)
        return carry

    lax.fori_loop(0, ntiles, rank_tile, 0)

    @pl.when((ntiles & 1) == 1)
    def _():
        for plane in range(3):
            d_scr[plane, ntiles] = jnp.full((TK, QB), -1.0, BF16)

    npairs = lax.shift_right_logical(ntiles + 1, 1)
    packed_rows = 2 * SUBLANES
    slabs_per_tile = TK // packed_rows
    one_bf, zero_bf = jnp.ones((), BF16), jnp.zeros((), BF16)

    def count_ge(plane, cand):
        cand_b = jnp.broadcast_to(cand.astype(BF16), (packed_rows, QB))

        def pair_body(p, accs):
            accs = list(accs)
            for half in range(2):
                for r in range(slabs_per_tile):
                    sl = d_scr[plane, 2 * p + half, r * packed_rows:(r + 1) * packed_rows, :]
                    c = r % COUNT_CHAINS
                    accs[c] = accs[c] + jnp.where(sl >= cand_b, one_bf, zero_bf)
            return tuple(accs)

        accs = lax.fori_loop(0, npairs, pair_body,
                             tuple(jnp.zeros((packed_rows, QB), BF16) for _ in range(COUNT_CHAINS)))
        total = functools.reduce(jnp.add, [a.astype(F32) for a in accs])
        return jnp.sum(total, axis=0, keepdims=True)

    def digit_search(plane, cnt_at):
        def bit_body(i, carry):
            prefix, cnt_at, step = carry
            trial = prefix + step
            cnt = count_ge(plane, trial)
            ok = cnt >= float(topk)
            return jnp.where(ok, trial, prefix), jnp.where(ok, cnt, cnt_at), step * 0.5
        digit, cnt_at, _ = lax.fori_loop(
            0, DIGIT_BITS, bit_body,
            (jnp.zeros((1, QB), F32), cnt_at, jnp.full((1, QB), DIGIT_BASE / 2.0, F32)))
        return digit, cnt_at

    def restrict_plane(plane, digit):
        digit_b = jnp.broadcast_to(digit.astype(BF16), (TK, QB))
        above, below = jnp.full((), float(DIGIT_BASE), BF16), jnp.full((), -1.0, BF16)

        def tile_body(t, carry):
            prev = d_scr[plane, t]
            d_scr[plane + 1, t] = jnp.where(prev > digit_b, above,
                                            jnp.where(prev < digit_b, below, d_scr[plane + 1, t]))
            return carry

        lax.fori_loop(0, ntiles, tile_body, 0)

    q_chunk = (n * QB + lax.broadcasted_iota(jnp.int32, (1, QB), 1)) // CHUNK
    n_admissible = ((q_chunk + 1) * CHUNK).astype(F32)
    digit1, cnt_at = digit_search(0, n_admissible)
    restrict_plane(0, digit1)
    digit2, cnt_at = digit_search(1, cnt_at)
    restrict_plane(1, digit2)
    digit3, cnt_at = digit_search(2, cnt_at)
    thr = (digit1 * float(DIGIT_BASE) + digit2) * float(DIGIT_BASE) + digit3

    excess0 = jnp.maximum(cnt_at - float(topk), 0.0)

    @pl.when(jnp.max(excess0) > 0.0)
    def _():
        thr_b = jnp.broadcast_to(thr, (SUBLANES, QB))
        later_or_same = (lax.broadcasted_iota(jnp.int32, (TK, TK), 1)
                         >= lax.broadcasted_iota(jnp.int32, (TK, TK), 0))
        suffix_ones = jnp.where(later_or_same, 1.0, 0.0).astype(BF16)

        def slab(r):
            return slice(r * SUBLANES, (r + 1) * SUBLANES)

        def drop_group(excess):
            def min_tile(t, ms):
                ms = list(ms)
                for r in range(rows_per_tile):
                    selected = r_scr[t, slab(r), :] >= thr_b
                    c = r % COUNT_CHAINS
                    ms[c] = jnp.minimum(ms[c], jnp.where(selected, s_scr[t, slab(r), :], jnp.inf))
                return tuple(ms)

            ms = lax.fori_loop(0, ntiles, min_tile,
                               tuple(jnp.full((SUBLANES, QB), jnp.inf, F32) for _ in range(COUNT_CHAINS)))
            m8 = functools.reduce(jnp.minimum, ms)
            m_row = jnp.min(m8, axis=0, keepdims=True)

            def drop_tile(i, later):
                t = ntiles - 1 - i
                r = r_scr[t]
                in_group = (r >= thr) & (s_scr[t] == m_row)
                member = jnp.where(in_group, 1.0, 0.0).astype(BF16)
                suffix = jnp.dot(suffix_ones, member, preferred_element_type=F32) + later
                r_scr[t] = jnp.where(in_group & (suffix <= excess), -1.0, r)
                return suffix[0:1, :]

            group_size = lax.fori_loop(0, ntiles, drop_tile, jnp.zeros((1, QB), F32))
            return excess - jnp.minimum(excess, group_size)

        lax.while_loop(lambda e: jnp.max(e) > 0.0, drop_group, excess0)

    slab_reduce = _slab_reduce

    def logit_tile(t, m8):
        sel = r_scr[t] >= thr
        krows = pl.ds(pl.multiple_of(t * TK, TK), TK)
        new_m8 = []
        for hh in range(B_HEADS):
            hs = slice(hh * B_HEAD_DIM, (hh + 1) * B_HEAD_DIM)
            att = jnp.dot(k_ref[0, krows, hs], q_ref[0, 0, hs, :], preferred_element_type=F32)
            att = jnp.where(sel, att, NEG_BIG)
            att_scr[t * B_HEADS + hh] = att
            new_m8.append(jnp.maximum(m8[hh], slab_reduce(att, jnp.maximum)))
        return tuple(new_m8)

    m8 = tiles_grouped(ntiles, logit_tile,
                        tuple(jnp.full((SUBLANES, QB), NEG_BIG, F32) for _ in range(B_HEADS)))
    m_row = [jnp.max(m, axis=0, keepdims=True) for m in m8]

    acc_scr[...] = jnp.zeros_like(acc_scr)

    def value_tile(t, l8):
        new_l8 = []
        for hh in range(B_HEADS):
            hs = slice(hh * B_HEAD_DIM, (hh + 1) * B_HEAD_DIM)
            p = jnp.exp(att_scr[t * B_HEADS + hh] - m_row[hh])
            new_l8.append(l8[hh] + slab_reduce(p, jnp.add))
            acc_scr[hs, :] += jnp.dot(v_ref[0, t, hs, :], p.astype(BF16), preferred_element_type=F32)
        return tuple(new_l8)

    l8 = tiles_grouped(ntiles, value_tile,
                        tuple(jnp.zeros((SUBLANES, QB), F32) for _ in range(B_HEADS)))

    parts = []
    for hh in range(B_HEADS):
        hs = slice(hh * B_HEAD_DIM, (hh + 1) * B_HEAD_DIM)
        l_row = jnp.sum(l8[hh], axis=0, keepdims=True)
        parts.append(acc_scr[hs, :] / l_row * zb_ref[0, 0, hs, :].astype(F32))
    y_ref[0] = jnp.concatenate(parts, axis=0).T.astype(BF16)


def _attn(qT, qiT, wiT, zbT, k, ki, vT, topk):
    b, nq = qT.shape[0], qT.shape[1]
    s = k.shape[1]
    blk = lambda rows: pl.BlockSpec((1, 1, rows, QB), lambda bi, i: (bi, i, 0, 0))
    return pl.pallas_call(
        functools.partial(_attn_body, topk=topk),
        grid=(b, nq),
        in_specs=[blk(B_WIDTH), blk(IDX_HEADS * IDX_DIM), blk(wiT.shape[2]), blk(B_WIDTH),
                  pl.BlockSpec((1, s, B_WIDTH), lambda bi, i: (bi, 0, 0)),
                  pl.BlockSpec((1, s, IDX_DIM), lambda bi, i: (bi, 0, 0)),
                  pl.BlockSpec((1, nq, B_WIDTH, QB), lambda bi, i: (bi, 0, 0, 0))],
        out_specs=pl.BlockSpec((1, QB, B_WIDTH), lambda bi, i: (bi, i, 0)),
        out_shape=jax.ShapeDtypeStruct((b, s, B_WIDTH), BF16),
        scratch_shapes=[pltpu.VMEM((s // TK, TK, QB), F32),
                        pltpu.VMEM((s // TK, TK, QB), F32),
                        pltpu.VMEM((3, s // TK, TK, QB), BF16),
                        pltpu.VMEM((s // TK * B_HEADS, TK, QB), F32),
                        pltpu.VMEM((B_WIDTH, QB), F32)],
        compiler_params=pltpu.CompilerParams(dimension_semantics=("arbitrary", "arbitrary"),
                                             vmem_limit_bytes=VMEM_LIMIT_BYTES),
        name="attn",
    )(qT, qiT, wiT, zbT, k, ki, vT)


def _out_rows(x, yb, ma, sgb, wob_ref, wout_ref):
    o_b = jnp.dot(yb, wob_ref[...], preferred_element_type=F32)
    merged = ma.astype(F32) + sgb.astype(F32) * o_b
    return x + jnp.dot(merged.astype(BF16), wout_ref[...], preferred_element_type=F32)


def _out_body(x_ref, yb_ref, ma_ref, sgb_ref, wob_ref, wout_ref, o_ref):
    o_ref[...] = _out_rows(x_ref[...], yb_ref[...], ma_ref[...], sgb_ref[...], wob_ref, wout_ref)


def _out(x2d, layer, yb, ma, sgb, wob, wout):
    m, d = x2d.shape
    tm = TM_OUT
    row = lambda width: pl.BlockSpec((tm, width), lambda i: (i, 0))
    full = lambda arr: _layer_spec(arr, layer)
    return pl.pallas_call(
        _out_body,
        grid=(m // tm,),
        in_specs=[row(d), row(B_WIDTH), row(d), row(d), full(wob), full(wout)],
        out_specs=row(d),
        out_shape=jax.ShapeDtypeStruct((m, d), F32),
        compiler_params=pltpu.CompilerParams(dimension_semantics=("arbitrary",),
                                             vmem_limit_bytes=VMEM_LIMIT_BYTES),
        name="out_proj",
    )(x2d, yb, ma, sgb, wob, wout)


def _rope_tables(s, dim):
    pos = jnp.arange(s, dtype=F32)
    inv = ROPE_THETA ** (-jnp.arange(0, dim, 2, dtype=F32) / dim)
    ang = pos[:, None] * inv[None, :]
    return jnp.cos(ang), jnp.sin(ang)


def kernel(x, norm_g, w_in, gate_b, a_ln_g, a_ln_b, a_ws, a_bs, q_norm_g, k_norm_g, w_oa, w_ob, w_out):
    b, s, d = x.shape
    depth = w_in.shape[0]
    topk = min(TOPK_MAX, s // 4)
    assert s % TM_STD == 0 and s % QB == 0 and (b * s) % TM_OUT == 0 and QB == TK and TM_STD_SUB == QB
    assert w_in.shape[2] == _O_END and topk <= TK

    cos_q, sin_q = _rope_tables(s, B_HEAD_DIM)
    cos_i, sin_i = _rope_tables(s, IDX_DIM)
    cos2 = jnp.concatenate([cos_q, cos_q], axis=1)
    sin2 = jnp.concatenate([-sin_q, sin_q], axis=1)
    cosi2 = jnp.concatenate([cos_i, cos_i, cos_i, cos_i], axis=1)
    sini2 = jnp.concatenate([-sin_i, sin_i, -sin_i, sin_i], axis=1)
    cos_t, sin_t, cosi_t, sini_t = cos_q.T, sin_q.T, cos_i.T, sin_i.T

    w_t = jnp.swapaxes(w_in, 1, 2)
    ik_rows = w_t[:, _O_IK:_O_IW]
    w_std = jnp.concatenate([w_t[:, _O_AU:_O_BQ], w_t[:, _O_BK:_O_BV], ik_rows, ik_rows,
                             w_t[:, _O_GA:_O_END]], axis=1).astype(BF16)
    w_tr = jnp.concatenate([w_t[:, _O_BQ:_O_BK], w_t[:, _O_BV:_O_IK]], axis=1).astype(BF16)
    iw_t = w_t[:, _O_IW:_O_GA]
    w_trw = jnp.concatenate([iw_t, jnp.zeros_like(iw_t)], axis=1).astype(BF16)
    woa, wob, wout = w_oa.astype(BF16), w_ob.astype(BF16), w_out.astype(BF16)
    bias_full = jnp.repeat(jnp.swapaxes(a_bs, 1, 2), A_WIDTH // A_GROUPS, axis=2)
    kng = jnp.reshape(k_norm_g, (depth, 1, B_HEAD_DIM))
    qng = jnp.reshape(q_norm_g, (depth, B_HEAD_DIM, 1))
    ng = jnp.reshape(norm_g, (depth, 1, d))
    lng = jnp.reshape(a_ln_g, (depth, 1, A_WIDTH))
    lnb = jnp.reshape(a_ln_b, (depth, 1, A_WIDTH))

    proj = functools.partial(_proj, ng=ng, w_std=w_std, gb=gate_b, lng=lng, lnb=lnb, ws=a_ws,
                             bias_full=bias_full, kng=kng, woa=woa, cos2=cos2, sin2=sin2, cosi2=cosi2,
                             sini2=sini2, w_tr=w_tr, w_trw=w_trw, qng=qng, cos_t=cos_t, sin_t=sin_t,
                             cosi_t=cosi_t, sini_t=sini_t, seq=s)
    x2d = jnp.reshape(x, (b * s, d))
    ma, sgb, k, ki, qT, vT, zbT, qiT, wiT = proj(x2d, 0)
    for l in range(depth):
        yb = _attn(qT, qiT, wiT, zbT, jnp.reshape(k, (b, s, B_WIDTH)), jnp.reshape(ki, (b, s, IDX_DIM)),
                   vT, topk)
        yb = jnp.reshape(yb, (b * s, B_WIDTH))
        if l + 1 < depth:
            x2d, ma, sgb, k, ki, qT, vT, zbT, qiT, wiT = proj(x2d, l + 1, prev=(yb, ma, sgb, wob, wout))
        else:
            x2d = _out(x2d, l, yb, ma, sgb, wob, wout)
    return jnp.reshape(x2d, (b, s, d))
```

```python
import functools

import jax
import jax.numpy as jnp
from jax import lax
from jax.experimental import pallas as pl
from jax.experimental.pallas import tpu as pltpu

F32 = jnp.float32
BF16 = jnp.bfloat16

CHUNK = 64
EPS = 1e-6
ROPE_THETA = 10000.0

A_WIDTH = 512
A_GROUPS = 4
A_BLOCK = 128
B_HEADS = 4
B_HEAD_DIM = 128
B_WIDTH = B_HEADS * B_HEAD_DIM
IDX_HEADS = 8
IDX_DIM = 64
TOPK_MAX = 256

_O_AU, _O_AV, _O_AZ = 0, 512, 1024
_O_BQ, _O_BK, _O_BV, _O_BZ = 1536, 2048, 2560, 3072
_O_IQ, _O_IK, _O_IW = 3584, 4096, 4160
_O_GA, _O_GB, _O_END = 4168, 5192, 6216

_P_AU, _P_AV, _P_AZ, _P_BK, _P_IK, _P_GA, _P_GB, _P_END = 0, 512, 1024, 1536, 2048, 2176, 3200, 4224
_F_BQ, _F_BV, _F_BZ, _F_IQ, _F_END = 0, 512, 1024, 1536, 2048

LANES = 128
SUBLANES = 8
VMEM_LIMIT_BYTES = 52 * 1024 * 1024
ATTN_VMEM_LIMIT_BYTES = 56 * 1024 * 1024

QB = 256
TK = 256
TM_STD = 512
TM_STD_SUB = 256
TM_OUT = 512
NEG_BIG = -1e30
DIGIT_BITS = 8
DIGIT_BASE = 2 ** DIGIT_BITS
RANK_MAX = float(DIGIT_BASE ** 3 - 1)
COUNT_CHAINS = 4
TILE_GROUPS = (8, 4, 2, 1)


def _rms_rows(x, g):
    ms = jnp.mean(x * x, axis=-1, keepdims=True)
    return x * lax.rsqrt(ms + EPS) * g


def _proj_body(*refs, after_out):
    if after_out:
        x_ref, yb_ref, ma_prev_ref, sgb_prev_ref, wob_ref, wout_ref = refs[:6]
        refs = refs[6:]
    else:
        x_ref, refs = refs[0], refs[1:]
    (ng_ref, w_ref, gb_ref, lng_ref, lnb_ref, ws_ref, bias_ref, kng_ref, woa_ref,
     cos_ref, sin_ref, cosi_ref, sini_ref,
     wt_ref, ww_ref, qng_ref, cost_ref, sint_ref, cosit_ref, sinit_ref) = refs[:20]
    outs = refs[20:]
    if after_out:
        xnew_ref, outs = outs[0], outs[1:]
    ma_ref, sgb_ref, k_ref, ki_ref, q_ref, v_ref, zb_ref, qi_ref, wi_ref = outs
    tm = x_ref.shape[0]
    att_scale = B_HEAD_DIM ** -0.5
    half = B_HEAD_DIM // 2
    ihalf = IDX_DIM // 2
    ci = lax.broadcasted_iota(jnp.int32, (A_BLOCK, A_BLOCK), 0) // CHUNK
    cj = lax.broadcasted_iota(jnp.int32, (A_BLOCK, A_BLOCK), 1) // CHUNK
    causal = cj <= ci
    wm = [jnp.where(causal, ws_ref[g], 0.0).astype(BF16) for g in range(A_GROUPS)]

    for r0 in range(0, tm, TM_STD_SUB):
        rows = slice(r0, r0 + TM_STD_SUB)
        x_rows = x_ref[rows, :]
        if after_out:
            x_rows = _out_rows(x_rows, yb_ref[rows, :], ma_prev_ref[rows, :], sgb_prev_ref[rows, :],
                               wob_ref, wout_ref)
            xnew_ref[rows, :] = x_rows
        h = _rms_rows(x_rows, ng_ref[...]).astype(BF16)

        def proj(lo, hi, h=h):
            return lax.dot_general(h, w_ref[lo:hi, :], (((1,), (1,)), ((), ())), preferred_element_type=F32)

        gv = jax.nn.gelu(proj(_P_AV, _P_AZ))
        mu = jnp.mean(gv, axis=-1, keepdims=True)
        xc = gv - mu
        var = jnp.mean(xc * xc, axis=-1, keepdims=True)
        vn = (xc * lax.rsqrt(var + EPS) * lng_ref[...] + lnb_ref[...]).astype(BF16)

        row_blocks = []
        for r in range(TM_STD_SUB // A_BLOCK):
            cols = []
            for g in range(A_GROUPS):
                vb = vn[r * A_BLOCK:(r + 1) * A_BLOCK, g * LANES:(g + 1) * LANES]
                cols.append(jnp.dot(wm[g], vb, preferred_element_type=F32))
            row_blocks.append(jnp.concatenate(cols, axis=1) + bias_ref[...])
        mixed = jnp.concatenate(row_blocks, axis=0)

        y_a = jax.nn.gelu(proj(_P_AU, _P_AV)) * mixed * jax.nn.silu(proj(_P_AZ, _P_BK))
        o_a = jnp.dot(y_a.astype(BF16), woa_ref[...], preferred_element_type=F32)
        ma_ref[rows, :] = (jax.nn.sigmoid(proj(_P_GA, _P_GB) + gb_ref[0:1, :]) * o_a).astype(BF16)
        sgb_ref[rows, :] = jax.nn.sigmoid(proj(_P_GB, _P_END) + gb_ref[1:2, :]).astype(BF16)

        b_k = proj(_P_BK, _P_IK)
        for hh in range(B_HEADS):
            kh = _rms_rows(b_k[:, hh * LANES:(hh + 1) * LANES], kng_ref[...])
            kh = kh * cos_ref[rows, :] + pltpu.roll(kh, B_HEAD_DIM // 2, 1) * sin_ref[rows, :]
            k_ref[rows, hh * LANES:(hh + 1) * LANES] = kh.astype(BF16)

        ik = proj(_P_IK, _P_GA)
        ik = ik * cosi_ref[rows, :] + pltpu.roll(ik, IDX_DIM // 2, 1) * sini_ref[rows, :]
        ki_ref[rows, :] = ik[:, :IDX_DIM].astype(BF16)

        j = r0 // QB

        def proj_t(w, h=h):
            return lax.dot_general(w, h, (((1,), (1,)), ((), ())), preferred_element_type=F32)

        qt = proj_t(wt_ref[_F_BQ:_F_BV, :])
        for hh in range(B_HEADS):
            qh = qt[hh * B_HEAD_DIM:(hh + 1) * B_HEAD_DIM, :]
            ms = jnp.mean(qh * qh, axis=0, keepdims=True)
            qh = qh * lax.rsqrt(ms + EPS) * qng_ref[...]
            x1, x2 = qh[:half, :], qh[half:, :]
            c, s = cost_ref[:, rows], sint_ref[:, rows]
            base = hh * B_HEAD_DIM
            q_ref[0, j, base:base + half, :] = ((x1 * c - x2 * s) * att_scale).astype(BF16)
            q_ref[0, j, base + half:base + B_HEAD_DIM, :] = ((x1 * s + x2 * c) * att_scale).astype(BF16)

        v_ref[0, j] = proj_t(wt_ref[_F_BV:_F_BZ, :]).astype(BF16)
        zb_ref[0, j] = jax.nn.silu(proj_t(wt_ref[_F_BZ:_F_IQ, :])).astype(BF16)

        qit = proj_t(wt_ref[_F_IQ:_F_END, :])
        for hh in range(IDX_HEADS):
            xh = qit[hh * IDX_DIM:(hh + 1) * IDX_DIM, :]
            x1, x2 = xh[:ihalf, :], xh[ihalf:, :]
            c, s = cosit_ref[:, rows], sinit_ref[:, rows]
            base = hh * IDX_DIM
            qi_ref[0, j, base:base + ihalf, :] = (x1 * c - x2 * s).astype(BF16)
            qi_ref[0, j, base + ihalf:base + IDX_DIM, :] = (x1 * s + x2 * c).astype(BF16)

        wi_ref[0, j] = proj_t(ww_ref[...])


def _layer_spec(arr, layer):
    zeros = (0,) * (arr.ndim - 1)
    return pl.BlockSpec((None,) + arr.shape[1:], lambda i: (layer,) + zeros)


def _proj(x2d, layer, ng, w_std, gb, lng, lnb, ws, bias_full, kng, woa, cos2, sin2, cosi2, sini2,
          w_tr, w_trw, qng, cos_t, sin_t, cosi_t, sini_t, seq, prev=None):
    m, d = x2d.shape
    tm = TM_STD
    nt = seq // tm
    b, nq, per = m // seq, seq // QB, tm // QB
    full = lambda arr: _layer_spec(arr, layer)
    tab = lambda: pl.BlockSpec((tm, LANES), lambda i: (i % nt, 0))
    tabt = lambda rows: pl.BlockSpec((rows, tm), lambda i: (0, i % nt))
    row = lambda width: pl.BlockSpec((tm, width), lambda i: (i, 0))
    outt = lambda rows: pl.BlockSpec((1, per, rows, QB), lambda i: (i // nt, i % nt, 0, 0))
    wrows = w_trw.shape[1]
    feat = lambda rows, dtype: jax.ShapeDtypeStruct((b, nq, rows, QB), dtype)

    operands = [x2d]
    in_specs = [row(d)]
    out_specs, out_shape = [], []
    if prev is not None:
        yb, ma_prev, sgb_prev, wob, wout = prev
        operands += [yb, ma_prev, sgb_prev, wob, wout]
        in_specs += [row(B_WIDTH), row(d), row(d), _layer_spec(wob, layer - 1), _layer_spec(wout, layer - 1)]
        out_specs.append(row(d))
        out_shape.append(jax.ShapeDtypeStruct((m, d), F32))
    operands += [ng, w_std, gb, lng, lnb, ws, bias_full, kng, woa, cos2, sin2, cosi2, sini2,
                 w_tr, w_trw, qng, cos_t, sin_t, cosi_t, sini_t]
    in_specs += [full(ng), full(w_std), full(gb), full(lng), full(lnb), full(ws), full(bias_full),
                 full(kng), full(woa), tab(), tab(), tab(), tab(),
                 full(w_tr), full(w_trw), full(qng),
                 tabt(B_HEAD_DIM // 2), tabt(B_HEAD_DIM // 2), tabt(IDX_DIM // 2), tabt(IDX_DIM // 2)]
    out_specs += [row(d), row(d), row(B_WIDTH), row(IDX_DIM),
                  outt(B_WIDTH), outt(B_WIDTH), outt(B_WIDTH), outt(IDX_HEADS * IDX_DIM), outt(wrows)]
    out_shape += [jax.ShapeDtypeStruct((m, d), BF16), jax.ShapeDtypeStruct((m, d), BF16),
                  jax.ShapeDtypeStruct((m, B_WIDTH), BF16), jax.ShapeDtypeStruct((m, IDX_DIM), BF16),
                  feat(B_WIDTH, BF16), feat(B_WIDTH, BF16), feat(B_WIDTH, BF16),
                  feat(IDX_HEADS * IDX_DIM, BF16), feat(wrows, F32)]
    return pl.pallas_call(
        functools.partial(_proj_body, after_out=prev is not None),
        grid=(m // tm,),
        in_specs=in_specs,
        out_specs=out_specs,
        out_shape=out_shape,
        compiler_params=pltpu.CompilerParams(dimension_semantics=("arbitrary",),
                                             vmem_limit_bytes=VMEM_LIMIT_BYTES),
        name="proj" if prev is None else "out_proj_then_proj",
    )(*operands)


def _slab_reduce(x, op):
    parts = [x[r:r + SUBLANES, :] for r in range(0, x.shape[0], SUBLANES)]
    while len(parts) > 1:
        parts = [op(parts[i], parts[i + 1]) for i in range(0, len(parts), 2)]
    return parts[0]


def _attn_body(q_ref, qi_ref, wi_ref, zb_ref, k_ref, ki_ref, v_ref, y_ref,
               s_scr, r_scr, d_scr, att_scr, acc_scr, *, topk):
    n = pl.program_id(1)
    ntiles = n + 1
    idx_scale = (IDX_DIM ** -0.5) * (IDX_HEADS ** -0.5)
    rows_per_tile = TK // SUBLANES

    def score_tile(t, lo8, hi8, diagonal):
        ki_t = ki_ref[0, pl.ds(pl.multiple_of(t * TK, TK), TK), :]
        acc = jnp.zeros((TK, QB), F32)
        for hh in range(IDX_HEADS):
            logit = jnp.dot(ki_t, qi_ref[0, 0, hh * IDX_DIM:(hh + 1) * IDX_DIM, :],
                            preferred_element_type=F32)
            acc = acc + wi_ref[0, 0, hh:hh + 1, :] * jnp.maximum(logit, 0.0)
        sc = acc * idx_scale
        sc_for_min = sc
        if diagonal:
            kc = lax.broadcasted_iota(jnp.int32, (TK, QB), 0) // CHUNK
            qc = lax.broadcasted_iota(jnp.int32, (TK, QB), 1) // CHUNK
            admissible = kc <= qc
            sc_for_min = jnp.where(admissible, sc, jnp.inf)
            sc = jnp.where(admissible, sc, -jnp.inf)
        s_scr[t] = sc
        return (jnp.minimum(lo8, _slab_reduce(sc_for_min, jnp.minimum)),
                jnp.maximum(hi8, _slab_reduce(sc, jnp.maximum)))

    def tiles_grouped(count, tile_fn, carry):
        def group(width, first, c):
            for j in range(width):
                c = tile_fn(first + j, c)
            return c

        widest = TILE_GROUPS[0]
        carry = lax.fori_loop(0, count // widest, lambda p, c: group(widest, widest * p, c), carry)
        for width in TILE_GROUPS[1:]:
            first = count & ~(2 * width - 1)
            carry = lax.cond((count & width) == width, functools.partial(group, width, first),
                             lambda c: c, carry)
        return carry

    lo8, hi8 = tiles_grouped(
        n, lambda t, c: score_tile(t, c[0], c[1], False),
        (jnp.full((SUBLANES, QB), jnp.inf, F32), jnp.full((SUBLANES, QB), -jnp.inf, F32)))
    lo8, hi8 = score_tile(n, lo8, hi8, True)
    s_lo = jnp.min(lo8, axis=0, keepdims=True)
    s_hi = jnp.max(hi8, axis=0, keepdims=True)

    span = s_hi - s_lo
    rank_scale = jnp.where(span > 0.0, RANK_MAX / jnp.where(span > 0.0, span, 1.0), 0.0)

    def rank_tile(t, carry):
        r = jnp.minimum(jnp.floor((s_scr[t] - s_lo) * rank_scale), RANK_MAX)
        r = jnp.maximum(r, -1.0)
        r_scr[t] = r
        d1 = jnp.floor(r * (1.0 / DIGIT_BASE ** 2))
        rem = r - d1 * float(DIGIT_BASE ** 2)
        d2 = jnp.floor(rem * (1.0 / DIGIT_BASE))
        d_scr[0, t] = d1.astype(BF16)
        d_scr[1, t] = d2.astype(BF16)
        d_scr[2, t] = (rem - d2 * float(DIGIT_BASE)).astype(BF16)
        return carry

    lax.fori_loop(0, ntiles, rank_tile, 0)

    @pl.when((ntiles & 1) == 1)
    def _():
        for plane in range(3):
            d_scr[plane, ntiles] = jnp.full((TK, QB), -1.0, BF16)

    npairs = lax.shift_right_logical(ntiles + 1, 1)
    packed_rows = 2 * SUBLANES
    slabs_per_tile = TK // packed_rows
    one_bf, zero_bf = jnp.ones((), BF16), jnp.zeros((), BF16)

    def count_ge(plane, cand):
        cand_b = jnp.broadcast_to(cand.astype(BF16), (packed_rows, QB))

        def pair_body(p, accs):
            accs = list(accs)
            for half in range(2):
                for r in range(slabs_per_tile):
                    sl = d_scr[plane, 2 * p + half, r * packed_rows:(r + 1) * packed_rows, :]
                    c = r % COUNT_CHAINS
                    accs[c] = accs[c] + jnp.where(sl >= cand_b, one_bf, zero_bf)
            return tuple(accs)

        accs = lax.fori_loop(0, npairs, pair_body,
                             tuple(jnp.zeros((packed_rows, QB), BF16) for _ in range(COUNT_CHAINS)))
        total = functools.reduce(jnp.add, [a.astype(F32) for a in accs])
        return jnp.sum(total, axis=0, keepdims=True)

    def digit_search(plane, cnt_at):
        def bit_body(i, carry):
            prefix, cnt_at, step = carry
            trial = prefix + step
            cnt = count_ge(plane, trial)
            ok = cnt >= float(topk)
            return jnp.where(ok, trial, prefix), jnp.where(ok, cnt, cnt_at), step * 0.5
        digit, cnt_at, _ = lax.fori_loop(
            0, DIGIT_BITS, bit_body,
            (jnp.zeros((1, QB), F32), cnt_at, jnp.full((1, QB), DIGIT_BASE / 2.0, F32)))
        return digit, cnt_at

    def restrict_plane(plane, digit):
        digit_b = jnp.broadcast_to(digit.astype(BF16), (TK, QB))
        above, below = jnp.full((), float(DIGIT_BASE), BF16), jnp.full((), -1.0, BF16)

        def tile_body(t, carry):
            prev = d_scr[plane, t]
            d_scr[plane + 1, t] = jnp.where(prev > digit_b, above,
                                            jnp.where(prev < digit_b, below, d_scr[plane + 1, t]))
            return carry

        lax.fori_loop(0, ntiles, tile_body, 0)

    q_chunk = (n * QB + lax.broadcasted_iota(jnp.int32, (1, QB), 1)) // CHUNK
    n_admissible = ((q_chunk + 1) * CHUNK).astype(F32)
    digit1, cnt_at = digit_search(0, n_admissible)
    restrict_plane(0, digit1)
    digit2, cnt_at = digit_search(1, cnt_at)
    restrict_plane(1, digit2)
    digit3, cnt_at = digit_search(2, cnt_at)
    thr = (digit1 * float(DIGIT_BASE) + digit2) * float(DIGIT_BASE) + digit3

    excess0 = jnp.maximum(cnt_at - float(topk), 0.0)

    @pl.when(jnp.max(excess0) > 0.0)
    def _():
        thr_b = jnp.broadcast_to(thr, (SUBLANES, QB))
        later_or_same = (lax.broadcasted_iota(jnp.int32, (TK, TK), 1)
                         >= lax.broadcasted_iota(jnp.int32, (TK, TK), 0))
        suffix_ones = jnp.where(later_or_same, 1.0, 0.0).astype(BF16)

        def slab(r):
            return slice(r * SUBLANES, (r + 1) * SUBLANES)

        def drop_group(excess):
            def min_tile(t, ms):
                ms = list(ms)
                for r in range(rows_per_tile):
                    selected = r_scr[t, slab(r), :] >= thr_b
                    c = r % COUNT_CHAINS
                    ms[c] = jnp.minimum(ms[c], jnp.where(selected, s_scr[t, slab(r), :], jnp.inf))
                return tuple(ms)

            ms = lax.fori_loop(0, ntiles, min_tile,
                               tuple(jnp.full((SUBLANES, QB), jnp.inf, F32) for _ in range(COUNT_CHAINS)))
            m8 = functools.reduce(jnp.minimum, ms)
            m_row = jnp.min(m8, axis=0, keepdims=True)

            def drop_tile(i, later):
                t = ntiles - 1 - i
                r = r_scr[t]
                in_group = (r >= thr) & (s_scr[t] == m_row)
                member = jnp.where(in_group, 1.0, 0.0).astype(BF16)
                suffix = jnp.dot(suffix_ones, member, preferred_element_type=F32) + later
                r_scr[t] = jnp.where(in_group & (suffix <= excess), -1.0, r)
                return suffix[0:1, :]

            group_size = lax.fori_loop(0, ntiles, drop_tile, jnp.zeros((1, QB), F32))
            return excess - jnp.minimum(excess, group_size)

        lax.while_loop(lambda e: jnp.max(e) > 0.0, drop_group, excess0)

    slab_reduce = _slab_reduce

    def logit_tile(t, m8):
        sel = r_scr[t] >= thr
        krows = pl.ds(pl.multiple_of(t * TK, TK), TK)
        new_m8 = []
        for hh in range(B_HEADS):
            hs = slice(hh * B_HEAD_DIM, (hh + 1) * B_HEAD_DIM)
            att = jnp.dot(k_ref[0, krows, hs], q_ref[0, 0, hs, :], preferred_element_type=F32)
            att = jnp.where(sel, att, NEG_BIG)
            att_scr[t * B_HEADS + hh] = att
            new_m8.append(jnp.maximum(m8[hh], slab_reduce(att, jnp.maximum)))
        return tuple(new_m8)

    m8 = tiles_grouped(ntiles, logit_tile,
                        tuple(jnp.full((SUBLANES, QB), NEG_BIG, F32) for _ in range(B_HEADS)))
    m_row = [jnp.max(m, axis=0, keepdims=True) for m in m8]

    acc_scr[...] = jnp.zeros_like(acc_scr)

    def value_tile(t, l8):
        new_l8 = []
        for hh in range(B_HEADS):
            hs = slice(hh * B_HEAD_DIM, (hh + 1) * B_HEAD_DIM)
            p = jnp.exp(att_scr[t * B_HEADS + hh] - m_row[hh])
            new_l8.append(l8[hh] + slab_reduce(p, jnp.add))
            acc_scr[hs, :] += jnp.dot(v_ref[0, t, hs, :], p.astype(BF16), preferred_element_type=F32)
        return tuple(new_l8)

    l8 = tiles_grouped(ntiles, value_tile,
                        tuple(jnp.zeros((SUBLANES, QB), F32) for _ in range(B_HEADS)))

    parts = []
    for hh in range(B_HEADS):
        hs = slice(hh * B_HEAD_DIM, (hh + 1) * B_HEAD_DIM)
        l_row = jnp.sum(l8[hh], axis=0, keepdims=True)
        parts.append(acc_scr[hs, :] / l_row * zb_ref[0, 0, hs, :].astype(F32))
    y_ref[0] = jnp.concatenate(parts, axis=0).T.astype(BF16)


def _attn(qT, qiT, wiT, zbT, k, ki, vT, topk):
    b, nq = qT.shape[0], qT.shape[1]
    s = k.shape[1]
    blk = lambda rows: pl.BlockSpec((1, 1, rows, QB), lambda bi, i: (bi, i, 0, 0))
    return pl.pallas_call(
        functools.partial(_attn_body, topk=topk),
        grid=(b, nq),
        in_specs=[blk(B_WIDTH), blk(IDX_HEADS * IDX_DIM), blk(wiT.shape[2]), blk(B_WIDTH),
                  pl.BlockSpec((1, s, B_WIDTH), lambda bi, i: (bi, 0, 0)),
                  pl.BlockSpec((1, s, IDX_DIM), lambda bi, i: (bi, 0, 0)),
                  pl.BlockSpec((1, nq, B_WIDTH, QB), lambda bi, i: (bi, 0, 0, 0))],
        out_specs=pl.BlockSpec((1, QB, B_WIDTH), lambda bi, i: (bi, i, 0)),
        out_shape=jax.ShapeDtypeStruct((b, s, B_WIDTH), BF16),
        scratch_shapes=[pltpu.VMEM((s // TK, TK, QB), F32),
                        pltpu.VMEM((s // TK, TK, QB), F32),
                        pltpu.VMEM((3, s // TK, TK, QB), BF16),
                        pltpu.VMEM((s // TK * B_HEADS, TK, QB), F32),
                        pltpu.VMEM((B_WIDTH, QB), F32)],
        compiler_params=pltpu.CompilerParams(dimension_semantics=("arbitrary", "arbitrary"),
                                             vmem_limit_bytes=ATTN_VMEM_LIMIT_BYTES),
        name="attn",
    )(qT, qiT, wiT, zbT, k, ki, vT)


def _out_rows(x, yb, ma, sgb, wob_ref, wout_ref):
    o_b = jnp.dot(yb, wob_ref[...], preferred_element_type=F32)
    merged = ma.astype(F32) + sgb.astype(F32) * o_b
    return x + jnp.dot(merged.astype(BF16), wout_ref[...], preferred_element_type=F32)


def _out_body(x_ref, yb_ref, ma_ref, sgb_ref, wob_ref, wout_ref, o_ref):
    o_ref[...] = _out_rows(x_ref[...], yb_ref[...], ma_ref[...], sgb_ref[...], wob_ref, wout_ref)


def _out(x2d, layer, yb, ma, sgb, wob, wout):
    m, d = x2d.shape
    tm = TM_OUT
    row = lambda width: pl.BlockSpec((tm, width), lambda i: (i, 0))
    full = lambda arr: _layer_spec(arr, layer)
    return pl.pallas_call(
        _out_body,
        grid=(m // tm,),
        in_specs=[row(d), row(B_WIDTH), row(d), row(d), full(wob), full(wout)],
        out_specs=row(d),
        out_shape=jax.ShapeDtypeStruct((m, d), F32),
        compiler_params=pltpu.CompilerParams(dimension_semantics=("arbitrary",),
                                             vmem_limit_bytes=VMEM_LIMIT_BYTES),
        name="out_proj",
    )(x2d, yb, ma, sgb, wob, wout)


def _rope_tables(s, dim):
    pos = jnp.arange(s, dtype=F32)
    inv = ROPE_THETA ** (-jnp.arange(0, dim, 2, dtype=F32) / dim)
    ang = pos[:, None] * inv[None, :]
    return jnp.cos(ang), jnp.sin(ang)


def kernel(x, norm_g, w_in, gate_b, a_ln_g, a_ln_b, a_ws, a_bs, q_norm_g, k_norm_g, w_oa, w_ob, w_out):
    b, s, d = x.shape
    depth = w_in.shape[0]
    topk = min(TOPK_MAX, s // 4)
    assert s % TM_STD == 0 and s % QB == 0 and (b * s) % TM_OUT == 0 and QB == TK and TM_STD_SUB == QB
    assert w_in.shape[2] == _O_END and topk <= TK

    cos_q, sin_q = _rope_tables(s, B_HEAD_DIM)
    cos_i, sin_i = _rope_tables(s, IDX_DIM)
    cos2 = jnp.concatenate([cos_q, cos_q], axis=1)
    sin2 = jnp.concatenate([-sin_q, sin_q], axis=1)
    cosi2 = jnp.concatenate([cos_i, cos_i, cos_i, cos_i], axis=1)
    sini2 = jnp.concatenate([-sin_i, sin_i, -sin_i, sin_i], axis=1)
    cos_t, sin_t, cosi_t, sini_t = cos_q.T, sin_q.T, cos_i.T, sin_i.T

    w_t = jnp.swapaxes(w_in, 1, 2)
    ik_rows = w_t[:, _O_IK:_O_IW]
    w_std = jnp.concatenate([w_t[:, _O_AU:_O_BQ], w_t[:, _O_BK:_O_BV], ik_rows, ik_rows,
                             w_t[:, _O_GA:_O_END]], axis=1).astype(BF16)
    w_tr = jnp.concatenate([w_t[:, _O_BQ:_O_BK], w_t[:, _O_BV:_O_IK]], axis=1).astype(BF16)
    iw_t = w_t[:, _O_IW:_O_GA]
    w_trw = jnp.concatenate([iw_t, jnp.zeros_like(iw_t)], axis=1).astype(BF16)
    woa, wob, wout = w_oa.astype(BF16), w_ob.astype(BF16), w_out.astype(BF16)
    bias_full = jnp.repeat(jnp.swapaxes(a_bs, 1, 2), A_WIDTH // A_GROUPS, axis=2)
    kng = jnp.reshape(k_norm_g, (depth, 1, B_HEAD_DIM))
    qng = jnp.reshape(q_norm_g, (depth, B_HEAD_DIM, 1))
    ng = jnp.reshape(norm_g, (depth, 1, d))
    lng = jnp.reshape(a_ln_g, (depth, 1, A_WIDTH))
    lnb = jnp.reshape(a_ln_b, (depth, 1, A_WIDTH))

    proj = functools.partial(_proj, ng=ng, w_std=w_std, gb=gate_b, lng=lng, lnb=lnb, ws=a_ws,
                             bias_full=bias_full, kng=kng, woa=woa, cos2=cos2, sin2=sin2, cosi2=cosi2,
                             sini2=sini2, w_tr=w_tr, w_trw=w_trw, qng=qng, cos_t=cos_t, sin_t=sin_t,
                             cosi_t=cosi_t, sini_t=sini_t, seq=s)
    x2d = jnp.reshape(x, (b * s, d))
    ma, sgb, k, ki, qT, vT, zbT, qiT, wiT = proj(x2d, 0)
    for l in range(depth):
        yb = _attn(qT, qiT, wiT, zbT, jnp.reshape(k, (b, s, B_WIDTH)), jnp.reshape(ki, (b, s, IDX_DIM)),
                   vT, topk)
        yb = jnp.reshape(yb, (b * s, B_WIDTH))
        if l + 1 < depth:
            x2d, ma, sgb, k, ki, qT, vT, zbT, qiT, wiT = proj(x2d, l + 1, prev=(yb, ma, sgb, wob, wout))
        else:
            x2d = _out(x2d, l, yb, ma, sgb, wob, wout)
    return jnp.reshape(x2d, (b, s, d))
```

```python
import functools

import jax
import jax.numpy as jnp
from jax import lax
from jax.experimental import pallas as pl
from jax.experimental.pallas import tpu as pltpu

F32 = jnp.float32
BF16 = jnp.bfloat16

CHUNK = 64
EPS = 1e-6
ROPE_THETA = 10000.0

A_WIDTH = 512
A_GROUPS = 4
A_BLOCK = 128
B_HEADS = 4
B_HEAD_DIM = 128
B_WIDTH = B_HEADS * B_HEAD_DIM
IDX_HEADS = 8
IDX_DIM = 64
TOPK_MAX = 256

_O_AU, _O_AV, _O_AZ = 0, 512, 1024
_O_BQ, _O_BK, _O_BV, _O_BZ = 1536, 2048, 2560, 3072
_O_IQ, _O_IK, _O_IW = 3584, 4096, 4160
_O_GA, _O_GB, _O_END = 4168, 5192, 6216

_P_AU, _P_AV, _P_AZ, _P_BK, _P_IK, _P_GA, _P_GB, _P_END = 0, 512, 1024, 1536, 2048, 2176, 3200, 4224
_F_BQ, _F_BV, _F_BZ, _F_IQ, _F_END = 0, 512, 1024, 1536, 2048

LANES = 128
SUBLANES = 8
VMEM_LIMIT_BYTES = 52 * 1024 * 1024
ATTN_VMEM_LIMIT_BYTES = 56 * 1024 * 1024

QB = 256
TK = 256
TM_STD = 512
TM_STD_SUB = 256
TM_OUT = 512
NEG_BIG = -1e30
DIGIT_BITS = 8
DIGIT_BASE = 2 ** DIGIT_BITS
RANK_MAX = float(DIGIT_BASE ** 3 - 1)
COUNT_CHAINS = 4
TILE_GROUPS = (8, 4, 2, 1)
COUNT_GROUPS = (4, 2, 1)


def _rms_rows(x, g):
    ms = jnp.mean(x * x, axis=-1, keepdims=True)
    return x * lax.rsqrt(ms + EPS) * g


def _proj_body(*refs, after_out):
    if after_out:
        x_ref, yb_ref, ma_prev_ref, sgb_prev_ref, wob_ref, wout_ref = refs[:6]
        refs = refs[6:]
    else:
        x_ref, refs = refs[0], refs[1:]
    (ng_ref, w_ref, gb_ref, lng_ref, lnb_ref, ws_ref, bias_ref, kng_ref, woa_ref,
     cos_ref, sin_ref, cosi_ref, sini_ref,
     wt_ref, ww_ref, qng_ref, cost_ref, sint_ref, cosit_ref, sinit_ref) = refs[:20]
    outs = refs[20:]
    if after_out:
        xnew_ref, outs = outs[0], outs[1:]
    ma_ref, sgb_ref, k_ref, ki_ref, q_ref, v_ref, zb_ref, qi_ref, wi_ref = outs
    tm = x_ref.shape[0]
    att_scale = B_HEAD_DIM ** -0.5
    half = B_HEAD_DIM // 2
    ihalf = IDX_DIM // 2
    ci = lax.broadcasted_iota(jnp.int32, (A_BLOCK, A_BLOCK), 0) // CHUNK
    cj = lax.broadcasted_iota(jnp.int32, (A_BLOCK, A_BLOCK), 1) // CHUNK
    causal = cj <= ci
    wm = [jnp.where(causal, ws_ref[g], 0.0).astype(BF16) for g in range(A_GROUPS)]

    for r0 in range(0, tm, TM_STD_SUB):
        rows = slice(r0, r0 + TM_STD_SUB)
        x_rows = x_ref[rows, :]
        if after_out:
            x_rows = _out_rows(x_rows, yb_ref[rows, :], ma_prev_ref[rows, :], sgb_prev_ref[rows, :],
                               wob_ref, wout_ref)
            xnew_ref[rows, :] = x_rows
        h = _rms_rows(x_rows, ng_ref[...]).astype(BF16)

        def proj(lo, hi, h=h):
            return lax.dot_general(h, w_ref[lo:hi, :], (((1,), (1,)), ((), ())), preferred_element_type=F32)

        gv = jax.nn.gelu(proj(_P_AV, _P_AZ))
        mu = jnp.mean(gv, axis=-1, keepdims=True)
        xc = gv - mu
        var = jnp.mean(xc * xc, axis=-1, keepdims=True)
        vn = (xc * lax.rsqrt(var + EPS) * lng_ref[...] + lnb_ref[...]).astype(BF16)

        row_blocks = []
        for r in range(TM_STD_SUB // A_BLOCK):
            cols = []
            for g in range(A_GROUPS):
                vb = vn[r * A_BLOCK:(r + 1) * A_BLOCK, g * LANES:(g + 1) * LANES]
                cols.append(jnp.dot(wm[g], vb, preferred_element_type=F32))
            row_blocks.append(jnp.concatenate(cols, axis=1) + bias_ref[...])
        mixed = jnp.concatenate(row_blocks, axis=0)

        y_a = jax.nn.gelu(proj(_P_AU, _P_AV)) * mixed * jax.nn.silu(proj(_P_AZ, _P_BK))
        o_a = jnp.dot(y_a.astype(BF16), woa_ref[...], preferred_element_type=F32)
        ma_ref[rows, :] = (jax.nn.sigmoid(proj(_P_GA, _P_GB) + gb_ref[0:1, :]) * o_a).astype(BF16)
        sgb_ref[rows, :] = jax.nn.sigmoid(proj(_P_GB, _P_END) + gb_ref[1:2, :]).astype(BF16)

        b_k = proj(_P_BK, _P_IK)
        for hh in range(B_HEADS):
            kh = _rms_rows(b_k[:, hh * LANES:(hh + 1) * LANES], kng_ref[...])
            kh = kh * cos_ref[rows, :] + pltpu.roll(kh, B_HEAD_DIM // 2, 1) * sin_ref[rows, :]
            k_ref[rows, hh * LANES:(hh + 1) * LANES] = kh.astype(BF16)

        ik = proj(_P_IK, _P_GA)
        ik = ik * cosi_ref[rows, :] + pltpu.roll(ik, IDX_DIM // 2, 1) * sini_ref[rows, :]
        ki_ref[rows, :] = ik[:, :IDX_DIM].astype(BF16)

        j = r0 // QB

        def proj_t(w, h=h):
            return lax.dot_general(w, h, (((1,), (1,)), ((), ())), preferred_element_type=F32)

        qt = proj_t(wt_ref[_F_BQ:_F_BV, :])
        for hh in range(B_HEADS):
            qh = qt[hh * B_HEAD_DIM:(hh + 1) * B_HEAD_DIM, :]
            ms = jnp.mean(qh * qh, axis=0, keepdims=True)
            qh = qh * lax.rsqrt(ms + EPS) * qng_ref[...]
            x1, x2 = qh[:half, :], qh[half:, :]
            c, s = cost_ref[:, rows], sint_ref[:, rows]
            base = hh * B_HEAD_DIM
            q_ref[0, j, base:base + half, :] = ((x1 * c - x2 * s) * att_scale).astype(BF16)
            q_ref[0, j, base + half:base + B_HEAD_DIM, :] = ((x1 * s + x2 * c) * att_scale).astype(BF16)

        v_ref[0, j] = proj_t(wt_ref[_F_BV:_F_BZ, :]).astype(BF16)
        zb_ref[0, j] = jax.nn.silu(proj_t(wt_ref[_F_BZ:_F_IQ, :])).astype(BF16)

        qit = proj_t(wt_ref[_F_IQ:_F_END, :])
        for hh in range(IDX_HEADS):
            xh = qit[hh * IDX_DIM:(hh + 1) * IDX_DIM, :]
            x1, x2 = xh[:ihalf, :], xh[ihalf:, :]
            c, s = cosit_ref[:, rows], sinit_ref[:, rows]
            base = hh * IDX_DIM
            qi_ref[0, j, base:base + ihalf, :] = (x1 * c - x2 * s).astype(BF16)
            qi_ref[0, j, base + ihalf:base + IDX_DIM, :] = (x1 * s + x2 * c).astype(BF16)

        wi_ref[0, j] = proj_t(ww_ref[...])


def _layer_spec(arr, layer):
    zeros = (0,) * (arr.ndim - 1)
    return pl.BlockSpec((None,) + arr.shape[1:], lambda i: (layer,) + zeros)


def _proj(x2d, layer, ng, w_std, gb, lng, lnb, ws, bias_full, kng, woa, cos2, sin2, cosi2, sini2,
          w_tr, w_trw, qng, cos_t, sin_t, cosi_t, sini_t, seq, prev=None):
    m, d = x2d.shape
    tm = TM_STD
    nt = seq // tm
    b, nq, per = m // seq, seq // QB, tm // QB
    full = lambda arr: _layer_spec(arr, layer)
    tab = lambda: pl.BlockSpec((tm, LANES), lambda i: (i % nt, 0))
    tabt = lambda rows: pl.BlockSpec((rows, tm), lambda i: (0, i % nt))
    row = lambda width: pl.BlockSpec((tm, width), lambda i: (i, 0))
    outt = lambda rows: pl.BlockSpec((1, per, rows, QB), lambda i: (i // nt, i % nt, 0, 0))
    wrows = w_trw.shape[1]
    feat = lambda rows, dtype: jax.ShapeDtypeStruct((b, nq, rows, QB), dtype)

    operands = [x2d]
    in_specs = [row(d)]
    out_specs, out_shape = [], []
    if prev is not None:
        yb, ma_prev, sgb_prev, wob, wout = prev
        operands += [yb, ma_prev, sgb_prev, wob, wout]
        in_specs += [row(B_WIDTH), row(d), row(d), _layer_spec(wob, layer - 1), _layer_spec(wout, layer - 1)]
        out_specs.append(row(d))
        out_shape.append(jax.ShapeDtypeStruct((m, d), F32))
    operands += [ng, w_std, gb, lng, lnb, ws, bias_full, kng, woa, cos2, sin2, cosi2, sini2,
                 w_tr, w_trw, qng, cos_t, sin_t, cosi_t, sini_t]
    in_specs += [full(ng), full(w_std), full(gb), full(lng), full(lnb), full(ws), full(bias_full),
                 full(kng), full(woa), tab(), tab(), tab(), tab(),
                 full(w_tr), full(w_trw), full(qng),
                 tabt(B_HEAD_DIM // 2), tabt(B_HEAD_DIM // 2), tabt(IDX_DIM // 2), tabt(IDX_DIM // 2)]
    out_specs += [row(d), row(d), row(B_WIDTH), row(IDX_DIM),
                  outt(B_WIDTH), outt(B_WIDTH), outt(B_WIDTH), outt(IDX_HEADS * IDX_DIM), outt(wrows)]
    out_shape += [jax.ShapeDtypeStruct((m, d), BF16), jax.ShapeDtypeStruct((m, d), BF16),
                  jax.ShapeDtypeStruct((m, B_WIDTH), BF16), jax.ShapeDtypeStruct((m, IDX_DIM), BF16),
                  feat(B_WIDTH, BF16), feat(B_WIDTH, BF16), feat(B_WIDTH, BF16),
                  feat(IDX_HEADS * IDX_DIM, BF16), feat(wrows, F32)]
    return pl.pallas_call(
        functools.partial(_proj_body, after_out=prev is not None),
        grid=(m // tm,),
        in_specs=in_specs,
        out_specs=out_specs,
        out_shape=out_shape,
        compiler_params=pltpu.CompilerParams(dimension_semantics=("arbitrary",),
                                             vmem_limit_bytes=VMEM_LIMIT_BYTES),
        name="proj" if prev is None else "out_proj_then_proj",
    )(*operands)


def _slab_reduce(x, op):
    parts = [x[r:r + SUBLANES, :] for r in range(0, x.shape[0], SUBLANES)]
    while len(parts) > 1:
        parts = [op(parts[i], parts[i + 1]) for i in range(0, len(parts), 2)]
    return parts[0]


def _attn_body(q_ref, qi_ref, wi_ref, zb_ref, k_ref, ki_ref, v_ref, y_ref,
               s_scr, r_scr, d_scr, att_scr, acc_scr, *, topk):
    n = pl.program_id(1)
    ntiles = n + 1
    idx_scale = (IDX_DIM ** -0.5) * (IDX_HEADS ** -0.5)
    rows_per_tile = TK // SUBLANES

    def score_tile(t, lo8, hi8, diagonal):
        ki_t = ki_ref[0, pl.ds(pl.multiple_of(t * TK, TK), TK), :]
        acc = jnp.zeros((TK, QB), F32)
        for hh in range(IDX_HEADS):
            logit = jnp.dot(ki_t, qi_ref[0, 0, hh * IDX_DIM:(hh + 1) * IDX_DIM, :],
                            preferred_element_type=F32)
            acc = acc + wi_ref[0, 0, hh:hh + 1, :] * jnp.maximum(logit, 0.0)
        sc = acc * idx_scale
        sc_for_min = sc
        if diagonal:
            kc = lax.broadcasted_iota(jnp.int32, (TK, QB), 0) // CHUNK
            qc = lax.broadcasted_iota(jnp.int32, (TK, QB), 1) // CHUNK
            admissible = kc <= qc
            sc_for_min = jnp.where(admissible, sc, jnp.inf)
            sc = jnp.where(admissible, sc, -jnp.inf)
        s_scr[t] = sc
        return (jnp.minimum(lo8, _slab_reduce(sc_for_min, jnp.minimum)),
                jnp.maximum(hi8, _slab_reduce(sc, jnp.maximum)))

    def tiles_grouped(count, tile_fn, carry, widths=TILE_GROUPS):
        def group(width, first, c):
            for j in range(width):
                c = tile_fn(first + j, c)
            return c

        widest = widths[0]
        carry = lax.fori_loop(0, count // widest, lambda p, c: group(widest, widest * p, c), carry)
        for width in widths[1:]:
            first = count & ~(2 * width - 1)
            carry = lax.cond((count & width) == width, functools.partial(group, width, first),
                             lambda c: c, carry)
        return carry

    lo8, hi8 = tiles_grouped(
        n, lambda t, c: score_tile(t, c[0], c[1], False),
        (jnp.full((SUBLANES, QB), jnp.inf, F32), jnp.full((SUBLANES, QB), -jnp.inf, F32)))
    lo8, hi8 = score_tile(n, lo8, hi8, True)
    s_lo = jnp.min(lo8, axis=0, keepdims=True)
    s_hi = jnp.max(hi8, axis=0, keepdims=True)

    span = s_hi - s_lo
    rank_scale = jnp.where(span > 0.0, RANK_MAX / jnp.where(span > 0.0, span, 1.0), 0.0)

    def rank_tile(t, carry):
        r = jnp.minimum(jnp.floor((s_scr[t] - s_lo) * rank_scale), RANK_MAX)
        r = jnp.maximum(r, -1.0)
        r_scr[t] = r
        d1 = jnp.floor(r * (1.0 / DIGIT_BASE ** 2))
        rem = r - d1 * float(DIGIT_BASE ** 2)
        d2 = jnp.floor(rem * (1.0 / DIGIT_BASE))
        d_scr[0, t] = d1.astype(BF16)
        d_scr[1, t] = d2.astype(BF16)
        d_scr[2, t] = (rem - d2 * float(DIGIT_BASE)).astype(BF16)
        return carry

    lax.fori_loop(0, ntiles, rank_tile, 0)

    packed_rows = 2 * SUBLANES
    slabs_per_tile = TK // packed_rows
    one_bf, zero_bf = jnp.ones((), BF16), jnp.zeros((), BF16)

    def count_ge(plane, cand):
        cand_b = jnp.broadcast_to(cand.astype(BF16), (packed_rows, QB))

        def count_tile(t, accs):
            accs = list(accs)
            for r in range(slabs_per_tile):
                sl = d_scr[plane, t, r * packed_rows:(r + 1) * packed_rows, :]
                c = r % COUNT_CHAINS
                accs[c] = accs[c] + jnp.where(sl >= cand_b, one_bf, zero_bf)
            return tuple(accs)

        accs = tiles_grouped(ntiles, count_tile,
                             tuple(jnp.zeros((packed_rows, QB), BF16) for _ in range(COUNT_CHAINS)),
                             widths=COUNT_GROUPS)
        total = functools.reduce(jnp.add, [a.astype(F32) for a in accs])
        return jnp.sum(total, axis=0, keepdims=True)

    def digit_search(plane, cnt_at):
        def bit_body(i, carry):
            prefix, cnt_at, step = carry
            trial = prefix + step
            cnt = count_ge(plane, trial)
            ok = cnt >= float(topk)
            return jnp.where(ok, trial, prefix), jnp.where(ok, cnt, cnt_at), step * 0.5
        digit, cnt_at, _ = lax.fori_loop(
            0, DIGIT_BITS, bit_body,
            (jnp.zeros((1, QB), F32), cnt_at, jnp.full((1, QB), DIGIT_BASE / 2.0, F32)))
        return digit, cnt_at

    def restrict_plane(plane, digit):
        digit_b = jnp.broadcast_to(digit.astype(BF16), (TK, QB))
        above, below = jnp.full((), float(DIGIT_BASE), BF16), jnp.full((), -1.0, BF16)

        def tile_body(t, carry):
            prev = d_scr[plane, t]
            d_scr[plane + 1, t] = jnp.where(prev > digit_b, above,
                                            jnp.where(prev < digit_b, below, d_scr[plane + 1, t]))
            return carry

        lax.fori_loop(0, ntiles, tile_body, 0)

    q_chunk = (n * QB + lax.broadcasted_iota(jnp.int32, (1, QB), 1)) // CHUNK
    n_admissible = ((q_chunk + 1) * CHUNK).astype(F32)
    digit1, cnt_at = digit_search(0, n_admissible)
    restrict_plane(0, digit1)
    digit2, cnt_at = digit_search(1, cnt_at)
    restrict_plane(1, digit2)
    digit3, cnt_at = digit_search(2, cnt_at)
    thr = (digit1 * float(DIGIT_BASE) + digit2) * float(DIGIT_BASE) + digit3

    excess0 = jnp.maximum(cnt_at - float(topk), 0.0)

    @pl.when(jnp.max(excess0) > 0.0)
    def _():
        thr_b = jnp.broadcast_to(thr, (SUBLANES, QB))
        later_or_same = (lax.broadcasted_iota(jnp.int32, (TK, TK), 1)
                         >= lax.broadcasted_iota(jnp.int32, (TK, TK), 0))
        suffix_ones = jnp.where(later_or_same, 1.0, 0.0).astype(BF16)

        def slab(r):
            return slice(r * SUBLANES, (r + 1) * SUBLANES)

        def drop_group(excess):
            def min_tile(t, ms):
                ms = list(ms)
                for r in range(rows_per_tile):
                    selected = r_scr[t, slab(r), :] >= thr_b
                    c = r % COUNT_CHAINS
                    ms[c] = jnp.minimum(ms[c], jnp.where(selected, s_scr[t, slab(r), :], jnp.inf))
                return tuple(ms)

            ms = lax.fori_loop(0, ntiles, min_tile,
                               tuple(jnp.full((SUBLANES, QB), jnp.inf, F32) for _ in range(COUNT_CHAINS)))
            m8 = functools.reduce(jnp.minimum, ms)
            m_row = jnp.min(m8, axis=0, keepdims=True)

            def drop_tile(i, later):
                t = ntiles - 1 - i
                r = r_scr[t]
                in_group = (r >= thr) & (s_scr[t] == m_row)
                member = jnp.where(in_group, 1.0, 0.0).astype(BF16)
                suffix = jnp.dot(suffix_ones, member, preferred_element_type=F32) + later
                r_scr[t] = jnp.where(in_group & (suffix <= excess), -1.0, r)
                return suffix[0:1, :]

            group_size = lax.fori_loop(0, ntiles, drop_tile, jnp.zeros((1, QB), F32))
            return excess - jnp.minimum(excess, group_size)

        lax.while_loop(lambda e: jnp.max(e) > 0.0, drop_group, excess0)

    slab_reduce = _slab_reduce

    def logit_tile(t, m8):
        sel = r_scr[t] >= thr
        krows = pl.ds(pl.multiple_of(t * TK, TK), TK)
        new_m8 = []
        for hh in range(B_HEADS):
            hs = slice(hh * B_HEAD_DIM, (hh + 1) * B_HEAD_DIM)
            att = jnp.dot(k_ref[0, krows, hs], q_ref[0, 0, hs, :], preferred_element_type=F32)
            att = jnp.where(sel, att, NEG_BIG)
            att_scr[t * B_HEADS + hh] = att
            new_m8.append(jnp.maximum(m8[hh], slab_reduce(att, jnp.maximum)))
        return tuple(new_m8)

    m8 = tiles_grouped(ntiles, logit_tile,
                        tuple(jnp.full((SUBLANES, QB), NEG_BIG, F32) for _ in range(B_HEADS)))
    m_row = [jnp.max(m, axis=0, keepdims=True) for m in m8]

    acc_scr[...] = jnp.zeros_like(acc_scr)

    def value_tile(t, l8):
        new_l8 = []
        for hh in range(B_HEADS):
            hs = slice(hh * B_HEAD_DIM, (hh + 1) * B_HEAD_DIM)
            p = jnp.exp(att_scr[t * B_HEADS + hh] - m_row[hh])
            new_l8.append(l8[hh] + slab_reduce(p, jnp.add))
            acc_scr[hs, :] += jnp.dot(v_ref[0, t, hs, :], p.astype(BF16), preferred_element_type=F32)
        return tuple(new_l8)

    l8 = tiles_grouped(ntiles, value_tile,
                        tuple(jnp.zeros((SUBLANES, QB), F32) for _ in range(B_HEADS)))

    parts = []
    for hh in range(B_HEADS):
        hs = slice(hh * B_HEAD_DIM, (hh + 1) * B_HEAD_DIM)
        l_row = jnp.sum(l8[hh], axis=0, keepdims=True)
        parts.append(acc_scr[hs, :] / l_row * zb_ref[0, 0, hs, :].astype(F32))
    y_ref[0] = jnp.concatenate(parts, axis=0).T.astype(BF16)


def _attn(qT, qiT, wiT, zbT, k, ki, vT, topk):
    b, nq = qT.shape[0], qT.shape[1]
    s = k.shape[1]
    blk = lambda rows: pl.BlockSpec((1, 1, rows, QB), lambda bi, i: (bi, i, 0, 0))
    return pl.pallas_call(
        functools.partial(_attn_body, topk=topk),
        grid=(b, nq),
        in_specs=[blk(B_WIDTH), blk(IDX_HEADS * IDX_DIM), blk(wiT.shape[2]), blk(B_WIDTH),
                  pl.BlockSpec((1, s, B_WIDTH), lambda bi, i: (bi, 0, 0)),
                  pl.BlockSpec((1, s, IDX_DIM), lambda bi, i: (bi, 0, 0)),
                  pl.BlockSpec((1, nq, B_WIDTH, QB), lambda bi, i: (bi, 0, 0, 0))],
        out_specs=pl.BlockSpec((1, QB, B_WIDTH), lambda bi, i: (bi, i, 0)),
        out_shape=jax.ShapeDtypeStruct((b, s, B_WIDTH), BF16),
        scratch_shapes=[pltpu.VMEM((s // TK, TK, QB), F32),
                        pltpu.VMEM((s // TK, TK, QB), F32),
                        pltpu.VMEM((3, s // TK, TK, QB), BF16),
                        pltpu.VMEM((s // TK * B_HEADS, TK, QB), F32),
                        pltpu.VMEM((B_WIDTH, QB), F32)],
        compiler_params=pltpu.CompilerParams(dimension_semantics=("arbitrary", "arbitrary"),
                                             vmem_limit_bytes=ATTN_VMEM_LIMIT_BYTES),
        name="attn",
    )(qT, qiT, wiT, zbT, k, ki, vT)


def _out_rows(x, yb, ma, sgb, wob_ref, wout_ref):
    o_b = jnp.dot(yb, wob_ref[...], preferred_element_type=F32)
    merged = ma.astype(F32) + sgb.astype(F32) * o_b
    return x + jnp.dot(merged.astype(BF16), wout_ref[...], preferred_element_type=F32)


def _out_body(x_ref, yb_ref, ma_ref, sgb_ref, wob_ref, wout_ref, o_ref):
    o_ref[...] = _out_rows(x_ref[...], yb_ref[...], ma_ref[...], sgb_ref[...], wob_ref, wout_ref)


def _out(x2d, layer, yb, ma, sgb, wob, wout):
    m, d = x2d.shape
    tm = TM_OUT
    row = lambda width: pl.BlockSpec((tm, width), lambda i: (i, 0))
    full = lambda arr: _layer_spec(arr, layer)
    return pl.pallas_call(
        _out_body,
        grid=(m // tm,),
        in_specs=[row(d), row(B_WIDTH), row(d), row(d), full(wob), full(wout)],
        out_specs=row(d),
        out_shape=jax.ShapeDtypeStruct((m, d), F32),
        compiler_params=pltpu.CompilerParams(dimension_semantics=("arbitrary",),
                                             vmem_limit_bytes=VMEM_LIMIT_BYTES),
        name="out_proj",
    )(x2d, yb, ma, sgb, wob, wout)


def _rope_tables(s, dim):
    pos = jnp.arange(s, dtype=F32)
    inv = ROPE_THETA ** (-jnp.arange(0, dim, 2, dtype=F32) / dim)
    ang = pos[:, None] * inv[None, :]
    return jnp.cos(ang), jnp.sin(ang)


def kernel(x, norm_g, w_in, gate_b, a_ln_g, a_ln_b, a_ws, a_bs, q_norm_g, k_norm_g, w_oa, w_ob, w_out):
    b, s, d = x.shape
    depth = w_in.shape[0]
    topk = min(TOPK_MAX, s // 4)
    assert s % TM_STD == 0 and s % QB == 0 and (b * s) % TM_OUT == 0 and QB == TK and TM_STD_SUB == QB
    assert w_in.shape[2] == _O_END and topk <= TK

    cos_q, sin_q = _rope_tables(s, B_HEAD_DIM)
    cos_i, sin_i = _rope_tables(s, IDX_DIM)
    cos2 = jnp.concatenate([cos_q, cos_q], axis=1)
    sin2 = jnp.concatenate([-sin_q, sin_q], axis=1)
    cosi2 = jnp.concatenate([cos_i, cos_i, cos_i, cos_i], axis=1)
    sini2 = jnp.concatenate([-sin_i, sin_i, -sin_i, sin_i], axis=1)
    cos_t, sin_t, cosi_t, sini_t = cos_q.T, sin_q.T, cos_i.T, sin_i.T

    w_t = jnp.swapaxes(w_in, 1, 2)
    ik_rows = w_t[:, _O_IK:_O_IW]
    w_std = jnp.concatenate([w_t[:, _O_AU:_O_BQ], w_t[:, _O_BK:_O_BV], ik_rows, ik_rows,
                             w_t[:, _O_GA:_O_END]], axis=1).astype(BF16)
    w_tr = jnp.concatenate([w_t[:, _O_BQ:_O_BK], w_t[:, _O_BV:_O_IK]], axis=1).astype(BF16)
    iw_t = w_t[:, _O_IW:_O_GA]
    w_trw = jnp.concatenate([iw_t, jnp.zeros_like(iw_t)], axis=1).astype(BF16)
    woa, wob, wout = w_oa.astype(BF16), w_ob.astype(BF16), w_out.astype(BF16)
    bias_full = jnp.repeat(jnp.swapaxes(a_bs, 1, 2), A_WIDTH // A_GROUPS, axis=2)
    kng = jnp.reshape(k_norm_g, (depth, 1, B_HEAD_DIM))
    qng = jnp.reshape(q_norm_g, (depth, B_HEAD_DIM, 1))
    ng = jnp.reshape(norm_g, (depth, 1, d))
    lng = jnp.reshape(a_ln_g, (depth, 1, A_WIDTH))
    lnb = jnp.reshape(a_ln_b, (depth, 1, A_WIDTH))

    proj = functools.partial(_proj, ng=ng, w_std=w_std, gb=gate_b, lng=lng, lnb=lnb, ws=a_ws,
                             bias_full=bias_full, kng=kng, woa=woa, cos2=cos2, sin2=sin2, cosi2=cosi2,
                             sini2=sini2, w_tr=w_tr, w_trw=w_trw, qng=qng, cos_t=cos_t, sin_t=sin_t,
                             cosi_t=cosi_t, sini_t=sini_t, seq=s)
    x2d = jnp.reshape(x, (b * s, d))
    ma, sgb, k, ki, qT, vT, zbT, qiT, wiT = proj(x2d, 0)
    for l in range(depth):
        yb = _attn(qT, qiT, wiT, zbT, jnp.reshape(k, (b, s, B_WIDTH)), jnp.reshape(ki, (b, s, IDX_DIM)),
                   vT, topk)
        yb = jnp.reshape(yb, (b * s, B_WIDTH))
        if l + 1 < depth:
            x2d, ma, sgb, k, ki, qT, vT, zbT, qiT, wiT = proj(x2d, l + 1, prev=(yb, ma, sgb, wob, wout))
        else:
            x2d = _out(x2d, l, yb, ma, sgb, wob, wout)
    return jnp.reshape(x2d, (b, s, d))
```

```python
import functools

import jax
import jax.numpy as jnp
from jax import lax
from jax.experimental import pallas as pl
from jax.experimental.pallas import tpu as pltpu

F32 = jnp.float32
BF16 = jnp.bfloat16

CHUNK = 64
EPS = 1e-6
ROPE_THETA = 10000.0

A_WIDTH = 512
A_GROUPS = 4
A_BLOCK = 128
B_HEADS = 4
B_HEAD_DIM = 128
B_WIDTH = B_HEADS * B_HEAD_DIM
IDX_HEADS = 8
IDX_DIM = 64
TOPK_MAX = 256

_O_AU, _O_AV, _O_AZ = 0, 512, 1024
_O_BQ, _O_BK, _O_BV, _O_BZ = 1536, 2048, 2560, 3072
_O_IQ, _O_IK, _O_IW = 3584, 4096, 4160
_O_GA, _O_GB, _O_END = 4168, 5192, 6216

_P_AU, _P_AV, _P_AZ, _P_BK, _P_IK, _P_GA, _P_GB, _P_END = 0, 512, 1024, 1536, 2048, 2176, 3200, 4224
_F_BQ, _F_BV, _F_BZ, _F_IQ, _F_END = 0, 512, 1024, 1536, 2048

LANES = 128
SUBLANES = 8
VMEM_LIMIT_BYTES = 52 * 1024 * 1024
ATTN_VMEM_LIMIT_BYTES = 56 * 1024 * 1024

QB = 256
TK = 256
TM_STD = 512
TM_STD_SUB = 256
TM_OUT = 512
NEG_BIG = -1e30
DIGIT_BITS = 8
DIGIT_BASE = 2 ** DIGIT_BITS
RANK_MAX = float(DIGIT_BASE ** 3 - 1)
COUNT_CHAINS = 4
TILE_GROUPS = (8, 4, 2, 1)
COUNT_GROUPS = (4, 2, 1)


def _rms_rows(x, g):
    ms = jnp.mean(x * x, axis=-1, keepdims=True)
    return x * lax.rsqrt(ms + EPS) * g


def _proj_body(*refs, after_out):
    if after_out:
        x_ref, yb_ref, ma_prev_ref, sgb_prev_ref, wob_ref, wout_ref = refs[:6]
        refs = refs[6:]
    else:
        x_ref, refs = refs[0], refs[1:]
    (ng_ref, w_ref, gb_ref, lng_ref, lnb_ref, ws_ref, bias_ref, kng_ref, woa_ref,
     cos_ref, sin_ref, cosi_ref, sini_ref,
     wt_ref, ww_ref, qng_ref, cost_ref, sint_ref, cosit_ref, sinit_ref) = refs[:20]
    outs = refs[20:]
    if after_out:
        xnew_ref, outs = outs[0], outs[1:]
    ma_ref, sgb_ref, k_ref, ki_ref, q_ref, v_ref, zb_ref, qi_ref, wi_ref = outs
    tm = x_ref.shape[0]
    att_scale = B_HEAD_DIM ** -0.5
    half = B_HEAD_DIM // 2
    ihalf = IDX_DIM // 2
    ci = lax.broadcasted_iota(jnp.int32, (A_BLOCK, A_BLOCK), 0) // CHUNK
    cj = lax.broadcasted_iota(jnp.int32, (A_BLOCK, A_BLOCK), 1) // CHUNK
    causal = cj <= ci
    wm = [jnp.where(causal, ws_ref[g], 0.0).astype(BF16) for g in range(A_GROUPS)]

    for r0 in range(0, tm, TM_STD_SUB):
        rows = slice(r0, r0 + TM_STD_SUB)
        x_rows = x_ref[rows, :]
        if after_out:
            x_rows = _out_rows(x_rows, yb_ref[rows, :], ma_prev_ref[rows, :], sgb_prev_ref[rows, :],
                               wob_ref, wout_ref)
            xnew_ref[rows, :] = x_rows
        h = _rms_rows(x_rows, ng_ref[...]).astype(BF16)

        def proj(lo, hi, h=h):
            return lax.dot_general(h, w_ref[lo:hi, :], (((1,), (1,)), ((), ())), preferred_element_type=F32)

        gv = jax.nn.gelu(proj(_P_AV, _P_AZ))
        mu = jnp.mean(gv, axis=-1, keepdims=True)
        xc = gv - mu
        var = jnp.mean(xc * xc, axis=-1, keepdims=True)
        vn = (xc * lax.rsqrt(var + EPS) * lng_ref[...] + lnb_ref[...]).astype(BF16)

        row_blocks = []
        for r in range(TM_STD_SUB // A_BLOCK):
            cols = []
            for g in range(A_GROUPS):
                vb = vn[r * A_BLOCK:(r + 1) * A_BLOCK, g * LANES:(g + 1) * LANES]
                cols.append(jnp.dot(wm[g], vb, preferred_element_type=F32))
            row_blocks.append(jnp.concatenate(cols, axis=1) + bias_ref[...])
        mixed = jnp.concatenate(row_blocks, axis=0)

        y_a = jax.nn.gelu(proj(_P_AU, _P_AV)) * mixed * jax.nn.silu(proj(_P_AZ, _P_BK))
        o_a = jnp.dot(y_a.astype(BF16), woa_ref[...], preferred_element_type=F32)
        ma_ref[rows, :] = (jax.nn.sigmoid(proj(_P_GA, _P_GB) + gb_ref[0:1, :]) * o_a).astype(BF16)
        sgb_ref[rows, :] = jax.nn.sigmoid(proj(_P_GB, _P_END) + gb_ref[1:2, :]).astype(BF16)

        b_k = proj(_P_BK, _P_IK)
        for hh in range(B_HEADS):
            kh = _rms_rows(b_k[:, hh * LANES:(hh + 1) * LANES], kng_ref[...])
            kh = kh * cos_ref[rows, :] + pltpu.roll(kh, B_HEAD_DIM // 2, 1) * sin_ref[rows, :]
            k_ref[rows, hh * LANES:(hh + 1) * LANES] = kh.astype(BF16)

        ik = proj(_P_IK, _P_GA)
        ik = ik * cosi_ref[rows, :] + pltpu.roll(ik, IDX_DIM // 2, 1) * sini_ref[rows, :]
        ki_ref[rows, :] = ik[:, :IDX_DIM].astype(BF16)

        j = r0 // QB

        def proj_t(w, h=h):
            return lax.dot_general(w, h, (((1,), (1,)), ((), ())), preferred_element_type=F32)

        qt = proj_t(wt_ref[_F_BQ:_F_BV, :])
        for hh in range(B_HEADS):
            qh = qt[hh * B_HEAD_DIM:(hh + 1) * B_HEAD_DIM, :]
            ms = jnp.mean(qh * qh, axis=0, keepdims=True)
            qh = qh * lax.rsqrt(ms + EPS) * qng_ref[...]
            x1, x2 = qh[:half, :], qh[half:, :]
            c, s = cost_ref[:, rows], sint_ref[:, rows]
            base = hh * B_HEAD_DIM
            q_ref[0, j, base:base + half, :] = ((x1 * c - x2 * s) * att_scale).astype(BF16)
            q_ref[0, j, base + half:base + B_HEAD_DIM, :] = ((x1 * s + x2 * c) * att_scale).astype(BF16)

        v_ref[0, j] = proj_t(wt_ref[_F_BV:_F_BZ, :]).astype(BF16)
        zb_ref[0, j] = jax.nn.silu(proj_t(wt_ref[_F_BZ:_F_IQ, :])).astype(BF16)

        qit = proj_t(wt_ref[_F_IQ:_F_END, :])
        for hh in range(IDX_HEADS):
            xh = qit[hh * IDX_DIM:(hh + 1) * IDX_DIM, :]
            x1, x2 = xh[:ihalf, :], xh[ihalf:, :]
            c, s = cosit_ref[:, rows], sinit_ref[:, rows]
            base = hh * IDX_DIM
            qi_ref[0, j, base:base + ihalf, :] = (x1 * c - x2 * s).astype(BF16)
            qi_ref[0, j, base + ihalf:base + IDX_DIM, :] = (x1 * s + x2 * c).astype(BF16)

        wi_ref[0, j] = proj_t(ww_ref[...])


def _layer_spec(arr, layer):
    zeros = (0,) * (arr.ndim - 1)
    return pl.BlockSpec((None,) + arr.shape[1:], lambda i: (layer,) + zeros)


def _proj(x2d, layer, ng, w_std, gb, lng, lnb, ws, bias_full, kng, woa, cos2, sin2, cosi2, sini2,
          w_tr, w_trw, qng, cos_t, sin_t, cosi_t, sini_t, seq, prev=None):
    m, d = x2d.shape
    tm = TM_STD
    nt = seq // tm
    b, nq, per = m // seq, seq // QB, tm // QB
    full = lambda arr: _layer_spec(arr, layer)
    tab = lambda: pl.BlockSpec((tm, LANES), lambda i: (i % nt, 0))
    tabt = lambda rows: pl.BlockSpec((rows, tm), lambda i: (0, i % nt))
    row = lambda width: pl.BlockSpec((tm, width), lambda i: (i, 0))
    outt = lambda rows: pl.BlockSpec((1, per, rows, QB), lambda i: (i // nt, i % nt, 0, 0))
    wrows = w_trw.shape[1]
    feat = lambda rows, dtype: jax.ShapeDtypeStruct((b, nq, rows, QB), dtype)

    operands = [x2d]
    in_specs = [row(d)]
    out_specs, out_shape = [], []
    if prev is not None:
        yb, ma_prev, sgb_prev, wob, wout = prev
        operands += [yb, ma_prev, sgb_prev, wob, wout]
        in_specs += [row(B_WIDTH), row(d), row(d), _layer_spec(wob, layer - 1), _layer_spec(wout, layer - 1)]
        out_specs.append(row(d))
        out_shape.append(jax.ShapeDtypeStruct((m, d), F32))
    operands += [ng, w_std, gb, lng, lnb, ws, bias_full, kng, woa, cos2, sin2, cosi2, sini2,
                 w_tr, w_trw, qng, cos_t, sin_t, cosi_t, sini_t]
    in_specs += [full(ng), full(w_std), full(gb), full(lng), full(lnb), full(ws), full(bias_full),
                 full(kng), full(woa), tab(), tab(), tab(), tab(),
                 full(w_tr), full(w_trw), full(qng),
                 tabt(B_HEAD_DIM // 2), tabt(B_HEAD_DIM // 2), tabt(IDX_DIM // 2), tabt(IDX_DIM // 2)]
    out_specs += [row(d), row(d), row(B_WIDTH), row(IDX_DIM),
                  outt(B_WIDTH), outt(B_WIDTH), outt(B_WIDTH), outt(IDX_HEADS * IDX_DIM), outt(wrows)]
    out_shape += [jax.ShapeDtypeStruct((m, d), BF16), jax.ShapeDtypeStruct((m, d), BF16),
                  jax.ShapeDtypeStruct((m, B_WIDTH), BF16), jax.ShapeDtypeStruct((m, IDX_DIM), BF16),
                  feat(B_WIDTH, BF16), feat(B_WIDTH, BF16), feat(B_WIDTH, BF16),
                  feat(IDX_HEADS * IDX_DIM, BF16), feat(wrows, F32)]
    return pl.pallas_call(
        functools.partial(_proj_body, after_out=prev is not None),
        grid=(m // tm,),
        in_specs=in_specs,
        out_specs=out_specs,
        out_shape=out_shape,
        compiler_params=pltpu.CompilerParams(dimension_semantics=("arbitrary",),
                                             vmem_limit_bytes=VMEM_LIMIT_BYTES),
        name="proj" if prev is None else "out_proj_then_proj",
    )(*operands)


def _slab_reduce(x, op):
    parts = [x[r:r + SUBLANES, :] for r in range(0, x.shape[0], SUBLANES)]
    while len(parts) > 1:
        parts = [op(parts[i], parts[i + 1]) for i in range(0, len(parts), 2)]
    return parts[0]


def _attn_body(q_ref, qi_ref, wi_ref, zb_ref, k_ref, ki_ref, v_ref, y_ref,
               s_scr, r_scr, d_scr, att_scr, acc_scr, *, topk):
    n = pl.program_id(1)
    ntiles = n + 1
    idx_scale = (IDX_DIM ** -0.5) * (IDX_HEADS ** -0.5)
    rows_per_tile = TK // SUBLANES

    def score_tile(t, carry):
        lo8, hi8 = carry
        ki_t = ki_ref[0, pl.ds(pl.multiple_of(t * TK, TK), TK), :]
        acc = jnp.zeros((TK, QB), F32)
        for hh in range(IDX_HEADS):
            logit = jnp.dot(ki_t, qi_ref[0, 0, hh * IDX_DIM:(hh + 1) * IDX_DIM, :],
                            preferred_element_type=F32)
            acc = acc + wi_ref[0, 0, hh:hh + 1, :] * jnp.maximum(logit, 0.0)
        sc = acc * idx_scale
        kc = lax.broadcasted_iota(jnp.int32, (TK, QB), 0) // CHUNK
        qc = lax.broadcasted_iota(jnp.int32, (TK, QB), 1) // CHUNK
        admissible = (kc <= qc) | (t < n)
        s_scr[t] = jnp.where(admissible, sc, -jnp.inf)
        return (jnp.minimum(lo8, _slab_reduce(jnp.where(admissible, sc, jnp.inf), jnp.minimum)),
                jnp.maximum(hi8, _slab_reduce(jnp.where(admissible, sc, -jnp.inf), jnp.maximum)))

    def tiles_grouped(count, tile_fn, carry, widths=TILE_GROUPS):
        def group(width, first, c):
            for j in range(width):
                c = tile_fn(first + j, c)
            return c

        widest = widths[0]
        carry = lax.fori_loop(0, count // widest, lambda p, c: group(widest, widest * p, c), carry)
        for width in widths[1:]:
            first = count & ~(2 * width - 1)
            carry = lax.cond((count & width) == width, functools.partial(group, width, first),
                             lambda c: c, carry)
        return carry

    lo8, hi8 = tiles_grouped(
        ntiles, score_tile,
        (jnp.full((SUBLANES, QB), jnp.inf, F32), jnp.full((SUBLANES, QB), -jnp.inf, F32)))
    s_lo = jnp.min(lo8, axis=0, keepdims=True)
    s_hi = jnp.max(hi8, axis=0, keepdims=True)

    span = s_hi - s_lo
    rank_scale = jnp.where(span > 0.0, RANK_MAX / jnp.where(span > 0.0, span, 1.0), 0.0)

    def rank_tile(t, carry):
        r = jnp.minimum(jnp.floor((s_scr[t] - s_lo) * rank_scale), RANK_MAX)
        r = jnp.maximum(r, -1.0)
        r_scr[t] = r
        d1 = jnp.floor(r * (1.0 / DIGIT_BASE ** 2))
        rem = r - d1 * float(DIGIT_BASE ** 2)
        d2 = jnp.floor(rem * (1.0 / DIGIT_BASE))
        d_scr[0, t] = d1.astype(BF16)
        d_scr[1, t] = d2.astype(BF16)
        d_scr[2, t] = (rem - d2 * float(DIGIT_BASE)).astype(BF16)
        return carry

    lax.fori_loop(0, ntiles, rank_tile, 0)

    packed_rows = 2 * SUBLANES
    slabs_per_tile = TK // packed_rows
    one_bf, zero_bf = jnp.ones((), BF16), jnp.zeros((), BF16)

    def count_ge(plane, cand):
        cand_b = jnp.broadcast_to(cand.astype(BF16), (packed_rows, QB))

        def count_tile(t, accs):
            accs = list(accs)
            for r in range(slabs_per_tile):
                sl = d_scr[plane, t, r * packed_rows:(r + 1) * packed_rows, :]
                c = r % COUNT_CHAINS
                accs[c] = accs[c] + jnp.where(sl >= cand_b, one_bf, zero_bf)
            return tuple(accs)

        accs = tiles_grouped(ntiles, count_tile,
                             tuple(jnp.zeros((packed_rows, QB), BF16) for _ in range(COUNT_CHAINS)),
                             widths=COUNT_GROUPS)
        total = functools.reduce(jnp.add, [a.astype(F32) for a in accs])
        return jnp.sum(total, axis=0, keepdims=True)

    def digit_search(plane, cnt_at):
        def bit_body(i, carry):
            prefix, cnt_at, step = carry
            trial = prefix + step
            cnt = count_ge(plane, trial)
            ok = cnt >= float(topk)
            return jnp.where(ok, trial, prefix), jnp.where(ok, cnt, cnt_at), step * 0.5
        digit, cnt_at, _ = lax.fori_loop(
            0, DIGIT_BITS, bit_body,
            (jnp.zeros((1, QB), F32), cnt_at, jnp.full((1, QB), DIGIT_BASE / 2.0, F32)))
        return digit, cnt_at

    def restrict_plane(plane, digit):
        digit_b = jnp.broadcast_to(digit.astype(BF16), (TK, QB))
        above, below = jnp.full((), float(DIGIT_BASE), BF16), jnp.full((), -1.0, BF16)

        def tile_body(t, carry):
            prev = d_scr[plane, t]
            d_scr[plane + 1, t] = jnp.where(prev > digit_b, above,
                                            jnp.where(prev < digit_b, below, d_scr[plane + 1, t]))
            return carry

        lax.fori_loop(0, ntiles, tile_body, 0)

    q_chunk = (n * QB + lax.broadcasted_iota(jnp.int32, (1, QB), 1)) // CHUNK
    n_admissible = ((q_chunk + 1) * CHUNK).astype(F32)
    digit1, cnt_at = digit_search(0, n_admissible)
    restrict_plane(0, digit1)
    digit2, cnt_at = digit_search(1, cnt_at)
    restrict_plane(1, digit2)
    digit3, cnt_at = digit_search(2, cnt_at)
    thr = (digit1 * float(DIGIT_BASE) + digit2) * float(DIGIT_BASE) + digit3

    excess0 = jnp.maximum(cnt_at - float(topk), 0.0)

    @pl.when(jnp.max(excess0) > 0.0)
    def _():
        thr_b = jnp.broadcast_to(thr, (SUBLANES, QB))
        later_or_same = (lax.broadcasted_iota(jnp.int32, (TK, TK), 1)
                         >= lax.broadcasted_iota(jnp.int32, (TK, TK), 0))
        suffix_ones = jnp.where(later_or_same, 1.0, 0.0).astype(BF16)

        def slab(r):
            return slice(r * SUBLANES, (r + 1) * SUBLANES)

        def drop_group(excess):
            def min_tile(t, ms):
                ms = list(ms)
                for r in range(rows_per_tile):
                    selected = r_scr[t, slab(r), :] >= thr_b
                    c = r % COUNT_CHAINS
                    ms[c] = jnp.minimum(ms[c], jnp.where(selected, s_scr[t, slab(r), :], jnp.inf))
                return tuple(ms)

            ms = lax.fori_loop(0, ntiles, min_tile,
                               tuple(jnp.full((SUBLANES, QB), jnp.inf, F32) for _ in range(COUNT_CHAINS)))
            m8 = functools.reduce(jnp.minimum, ms)
            m_row = jnp.min(m8, axis=0, keepdims=True)

            def drop_tile(i, later):
                t = ntiles - 1 - i
                r = r_scr[t]
                in_group = (r >= thr) & (s_scr[t] == m_row)
                member = jnp.where(in_group, 1.0, 0.0).astype(BF16)
                suffix = jnp.dot(suffix_ones, member, preferred_element_type=F32) + later
                r_scr[t] = jnp.where(in_group & (suffix <= excess), -1.0, r)
                return suffix[0:1, :]

            group_size = lax.fori_loop(0, ntiles, drop_tile, jnp.zeros((1, QB), F32))
            return excess - jnp.minimum(excess, group_size)

        lax.while_loop(lambda e: jnp.max(e) > 0.0, drop_group, excess0)

    slab_reduce = _slab_reduce

    def logit_tile(t, m8):
        sel = r_scr[t] >= thr
        krows = pl.ds(pl.multiple_of(t * TK, TK), TK)
        new_m8 = []
        for hh in range(B_HEADS):
            hs = slice(hh * B_HEAD_DIM, (hh + 1) * B_HEAD_DIM)
            att = jnp.dot(k_ref[0, krows, hs], q_ref[0, 0, hs, :], preferred_element_type=F32)
            att = jnp.where(sel, att, NEG_BIG)
            att_scr[t * B_HEADS + hh] = att
            new_m8.append(jnp.maximum(m8[hh], slab_reduce(att, jnp.maximum)))
        return tuple(new_m8)

    m8 = tiles_grouped(ntiles, logit_tile,
                        tuple(jnp.full((SUBLANES, QB), NEG_BIG, F32) for _ in range(B_HEADS)))
    m_row = [jnp.max(m, axis=0, keepdims=True) for m in m8]

    acc_scr[...] = jnp.zeros_like(acc_scr)

    def value_tile(t, l8):
        new_l8 = []
        for hh in range(B_HEADS):
            hs = slice(hh * B_HEAD_DIM, (hh + 1) * B_HEAD_DIM)
            p = jnp.exp(att_scr[t * B_HEADS + hh] - m_row[hh])
            new_l8.append(l8[hh] + slab_reduce(p, jnp.add))
            acc_scr[hs, :] += jnp.dot(v_ref[0, t, hs, :], p.astype(BF16), preferred_element_type=F32)
        return tuple(new_l8)

    l8 = tiles_grouped(ntiles, value_tile,
                        tuple(jnp.zeros((SUBLANES, QB), F32) for _ in range(B_HEADS)))

    parts = []
    for hh in range(B_HEADS):
        hs = slice(hh * B_HEAD_DIM, (hh + 1) * B_HEAD_DIM)
        l_row = jnp.sum(l8[hh], axis=0, keepdims=True)
        parts.append(acc_scr[hs, :] / l_row * zb_ref[0, 0, hs, :].astype(F32))
    y_ref[0] = jnp.concatenate(parts, axis=0).T.astype(BF16)


def _attn(qT, qiT, wiT, zbT, k, ki, vT, topk):
    b, nq = qT.shape[0], qT.shape[1]
    s = k.shape[1]
    blk = lambda rows: pl.BlockSpec((1, 1, rows, QB), lambda bi, i: (bi, i, 0, 0))
    return pl.pallas_call(
        functools.partial(_attn_body, topk=topk),
        grid=(b, nq),
        in_specs=[blk(B_WIDTH), blk(IDX_HEADS * IDX_DIM), blk(wiT.shape[2]), blk(B_WIDTH),
                  pl.BlockSpec((1, s, B_WIDTH), lambda bi, i: (bi, 0, 0)),
                  pl.BlockSpec((1, s, IDX_DIM), lambda bi, i: (bi, 0, 0)),
                  pl.BlockSpec((1, nq, B_WIDTH, QB), lambda bi, i: (bi, 0, 0, 0))],
        out_specs=pl.BlockSpec((1, QB, B_WIDTH), lambda bi, i: (bi, i, 0)),
        out_shape=jax.ShapeDtypeStruct((b, s, B_WIDTH), BF16),
        scratch_shapes=[pltpu.VMEM((s // TK, TK, QB), F32),
                        pltpu.VMEM((s // TK, TK, QB), F32),
                        pltpu.VMEM((3, s // TK, TK, QB), BF16),
                        pltpu.VMEM((s // TK * B_HEADS, TK, QB), F32),
                        pltpu.VMEM((B_WIDTH, QB), F32)],
        compiler_params=pltpu.CompilerParams(dimension_semantics=("arbitrary", "arbitrary"),
                                             vmem_limit_bytes=ATTN_VMEM_LIMIT_BYTES),
        name="attn",
    )(qT, qiT, wiT, zbT, k, ki, vT)


def _out_rows(x, yb, ma, sgb, wob_ref, wout_ref):
    o_b = jnp.dot(yb, wob_ref[...], preferred_element_type=F32)
    merged = ma.astype(F32) + sgb.astype(F32) * o_b
    return x + jnp.dot(merged.astype(BF16), wout_ref[...], preferred_element_type=F32)


def _out_body(x_ref, yb_ref, ma_ref, sgb_ref, wob_ref, wout_ref, o_ref):
    o_ref[...] = _out_rows(x_ref[...], yb_ref[...], ma_ref[...], sgb_ref[...], wob_ref, wout_ref)


def _out(x2d, layer, yb, ma, sgb, wob, wout):
    m, d = x2d.shape
    tm = TM_OUT
    row = lambda width: pl.BlockSpec((tm, width), lambda i: (i, 0))
    full = lambda arr: _layer_spec(arr, layer)
    return pl.pallas_call(
        _out_body,
        grid=(m // tm,),
        in_specs=[row(d), row(B_WIDTH), row(d), row(d), full(wob), full(wout)],
        out_specs=row(d),
        out_shape=jax.ShapeDtypeStruct((m, d), F32),
        compiler_params=pltpu.CompilerParams(dimension_semantics=("arbitrary",),
                                             vmem_limit_bytes=VMEM_LIMIT_BYTES),
        name="out_proj",
    )(x2d, yb, ma, sgb, wob, wout)


def _rope_tables(s, dim):
    pos = jnp.arange(s, dtype=F32)
    inv = ROPE_THETA ** (-jnp.arange(0, dim, 2, dtype=F32) / dim)
    ang = pos[:, None] * inv[None, :]
    return jnp.cos(ang), jnp.sin(ang)


def kernel(x, norm_g, w_in, gate_b, a_ln_g, a_ln_b, a_ws, a_bs, q_norm_g, k_norm_g, w_oa, w_ob, w_out):
    b, s, d = x.shape
    depth = w_in.shape[0]
    topk = min(TOPK_MAX, s // 4)
    assert s % TM_STD == 0 and s % QB == 0 and (b * s) % TM_OUT == 0 and QB == TK and TM_STD_SUB == QB
    assert w_in.shape[2] == _O_END and topk <= TK

    cos_q, sin_q = _rope_tables(s, B_HEAD_DIM)
    cos_i, sin_i = _rope_tables(s, IDX_DIM)
    cos2 = jnp.concatenate([cos_q, cos_q], axis=1)
    sin2 = jnp.concatenate([-sin_q, sin_q], axis=1)
    cosi2 = jnp.concatenate([cos_i, cos_i, cos_i, cos_i], axis=1)
    sini2 = jnp.concatenate([-sin_i, sin_i, -sin_i, sin_i], axis=1)
    cos_t, sin_t, cosi_t, sini_t = cos_q.T, sin_q.T, cos_i.T, sin_i.T

    w_t = jnp.swapaxes(w_in, 1, 2)
    ik_rows = w_t[:, _O_IK:_O_IW]
    w_std = jnp.concatenate([w_t[:, _O_AU:_O_BQ], w_t[:, _O_BK:_O_BV], ik_rows, ik_rows,
                             w_t[:, _O_GA:_O_END]], axis=1).astype(BF16)
    w_tr = jnp.concatenate([w_t[:, _O_BQ:_O_BK], w_t[:, _O_BV:_O_IK]], axis=1).astype(BF16)
    iw_t = w_t[:, _O_IW:_O_GA]
    w_trw = jnp.concatenate([iw_t, jnp.zeros_like(iw_t)], axis=1).astype(BF16)
    woa, wob, wout = w_oa.astype(BF16), w_ob.astype(BF16), w_out.astype(BF16)
    bias_full = jnp.repeat(jnp.swapaxes(a_bs, 1, 2), A_WIDTH // A_GROUPS, axis=2)
    kng = jnp.reshape(k_norm_g, (depth, 1, B_HEAD_DIM))
    qng = jnp.reshape(q_norm_g, (depth, B_HEAD_DIM, 1))
    ng = jnp.reshape(norm_g, (depth, 1, d))
    lng = jnp.reshape(a_ln_g, (depth, 1, A_WIDTH))
    lnb = jnp.reshape(a_ln_b, (depth, 1, A_WIDTH))

    proj = functools.partial(_proj, ng=ng, w_std=w_std, gb=gate_b, lng=lng, lnb=lnb, ws=a_ws,
                             bias_full=bias_full, kng=kng, woa=woa, cos2=cos2, sin2=sin2, cosi2=cosi2,
                             sini2=sini2, w_tr=w_tr, w_trw=w_trw, qng=qng, cos_t=cos_t, sin_t=sin_t,
                             cosi_t=cosi_t, sini_t=sini_t, seq=s)
    x2d = jnp.reshape(x, (b * s, d))
    ma, sgb, k, ki, qT, vT, zbT, qiT, wiT = proj(x2d, 0)
    for l in range(depth):
        yb = _attn(qT, qiT, wiT, zbT, jnp.reshape(k, (b, s, B_WIDTH)), jnp.reshape(ki, (b, s, IDX_DIM)),
                   vT, topk)
        yb = jnp.reshape(yb, (b * s, B_WIDTH))
        if l + 1 < depth:
            x2d, ma, sgb, k, ki, qT, vT, zbT, qiT, wiT = proj(x2d, l + 1, prev=(yb, ma, sgb, wob, wout))
        else:
            x2d = _out(x2d, l, yb, ma, sgb, wob, wout)
    return jnp.reshape(x2d, (b, s, d))
```
